```python
import math
import jax, jax.numpy as jnp
from jax import lax
import numpy as np

D_MODEL = 2048
BATCH = 8
SEQ = 2048
DEPTH = 2

W_RWKV = D_MODEL // 4
HEAD_DIM_RWKV = 64
N_HEADS_RWKV = W_RWKV // HEAD_DIM_RWKV
LORA_DECAY = 64
LORA_ICLR = 64
LORA_VRES = 32
LORA_GATE = 128
RWKV_GN_EPS = 64e-5
W_GDN = D_MODEL // 4
HEAD_DIM_GDN = 128
N_HEADS_GDN = W_GDN // HEAD_DIM_GDN
CONV_WIDTH = 4
GDN_CHUNK = 64
W_DIFF = D_MODEL // 2
HEAD_DIM_DIFF = 64
N_HEADS_DIFF = W_DIFF // (2 * HEAD_DIM_DIFF)
Q_BLOCK = 128
RWKV_COLS = 3 * W_RWKV + LORA_DECAY + LORA_ICLR + LORA_GATE
GDN_COLS = 4 * W_GDN + 2 * N_HEADS_GDN
DIFF_COLS = 3 * W_DIFF
N_IN = RWKV_COLS + GDN_COLS + DIFF_COLS
MIX_WIDTH = W_RWKV + W_GDN + W_DIFF
FFN_HIDDEN = -(-8 * D_MODEL // (3 * 256)) * 256
NORM_EPS = 1e-6

kernel_name = "hybrid_rwkv7_gdn_diffattn_block"


def rms_norm(x, w, eps=NORM_EPS):
    xf = x.astype(jnp.float32)
    y = xf * lax.rsqrt(jnp.mean(xf * xf, axis=-1, keepdims=True) + eps)
    return (y * w.astype(jnp.float32)).astype(x.dtype)


def l2_normalize(x, eps=NORM_EPS):
    xf = x.astype(jnp.float32)
    return xf * lax.rsqrt(jnp.sum(xf * xf, axis=-1, keepdims=True) + eps)


def token_shift(x):
    return jnp.pad(x, ((0, 0), (1, 0), (0, 0)))[:, :-1]


def causal_depthwise_conv(x, w):
    k, c = w.shape
    return lax.conv_general_dilated(
        x, w[:, None, :].astype(x.dtype), window_strides=(1,), padding=[(k - 1, 0)],
        dimension_numbers=("NWC", "WIO", "NWC"), feature_group_count=c)


def wkv7_scan(r, decay, k, v, a, b):
    bsz, _, h, n = r.shape

    def step(s, inp):
        r_t, w_t, k_t, v_t, a_t, b_t = inp
        sa = jnp.einsum("bhij,bhj->bhi", s, a_t)
        s = s * w_t[:, :, None, :] + sa[..., None] * b_t[:, :, None, :] + v_t[..., None] * k_t[:, :, None, :]
        return s, jnp.einsum("bhij,bhj->bhi", s, r_t)

    xs = tuple(jnp.moveaxis(t, 1, 0) for t in (r, decay, k, v, a, b))
    _, ys = lax.scan(step, jnp.zeros((bsz, h, n, n), jnp.float32), xs)
    return jnp.moveaxis(ys, 0, 1)


def rwkv7_time_mix(p, v_first, w0, w_lora_b, a0, a_lora_b, g_lora_b, k_k, k_a, r_k,
                   ln_w, ln_b, v0, v_lora_b):
    bsz, t, _ = p.shape
    h, n, wd = N_HEADS_RWKV, HEAD_DIM_RWKV, W_RWKV
    r = p[..., 0:wd]
    k = p[..., wd:2 * wd]
    v = p[..., 2 * wd:3 * wd]
    o = 3 * wd
    xw = p[..., o:o + LORA_DECAY]
    o += LORA_DECAY
    xa = p[..., o:o + LORA_ICLR]
    o += LORA_ICLR
    xg = p[..., o:o + LORA_GATE]
    w_log = -jax.nn.softplus(-(w0 + jnp.tanh(xw) @ w_lora_b)) - 0.5
    if v0 is None:
        v_first = v
    else:
        xv = p[..., RWKV_COLS:]
        v = v + (v_first - v) * jax.nn.sigmoid(v0 + xv @ v_lora_b)
    a = jax.nn.sigmoid(a0 + xa @ a_lora_b)
    g = jax.nn.sigmoid(xg) @ g_lora_b
    heads = lambda z: z.reshape(bsz, t, h, n).astype(jnp.float32)
    kk = l2_normalize(heads(k * k_k))
    k = k * (1.0 + (a - 1.0) * k_a)
    rh, kh, vh, ah = heads(r), heads(k), heads(v), heads(a)
    decay = jnp.exp(-jnp.exp(heads(w_log)))
    y = wkv7_scan(rh, decay, kh, vh, -kk, kk * ah)
    mean = jnp.mean(y, axis=-1, keepdims=True)
    var = jnp.mean(jnp.square(y - mean), axis=-1, keepdims=True)
    y = (y - mean) * lax.rsqrt(var + RWKV_GN_EPS)
    y = y * ln_w.reshape(h, n).astype(jnp.float32) + ln_b.reshape(h, n).astype(jnp.float32)
    bonus = jnp.sum(rh * kh * r_k.astype(jnp.float32), axis=-1, keepdims=True) * vh
    out = (y + bonus).reshape(bsz, t, wd).astype(p.dtype) * g
    return out, v_first


def gated_delta_rule_chunked(q, k, v, g, beta):
    bsz, t, h, dk = q.shape
    dv = v.shape[-1]
    c = GDN_CHUNK
    nc = t // c
    ch4 = lambda z: z.reshape(bsz, nc, c, h, z.shape[-1]).transpose(0, 3, 1, 2, 4)
    ch3 = lambda z: z.reshape(bsz, nc, c, h).transpose(0, 3, 1, 2)
    q, k, v = ch4(q * dk ** -0.5), ch4(k), ch4(v)
    g, beta = ch3(g), ch3(beta)
    gc = jnp.cumsum(g, axis=-1)
    idx = jnp.arange(c)
    causal = idx[:, None] >= idx[None, :]
    strict = idx[:, None] > idx[None, :]
    decay = jnp.exp(jnp.where(causal, gc[..., :, None] - gc[..., None, :], -jnp.inf))
    kb = k * beta[..., None]
    m = jnp.where(strict, jnp.einsum("bhnik,bhnjk->bhnij", kb, k) * decay, 0.0)
    lhs = jnp.eye(c, dtype=jnp.float32) + m
    rhs = jnp.concatenate([v * beta[..., None], kb * jnp.exp(gc)[..., None]], axis=-1)
    sol = lax.linalg.triangular_solve(lhs, rhs, left_side=True, lower=True, unit_diagonal=True)
    u, w = sol[..., :dv], sol[..., dv:]
    attn = jnp.einsum("bhnik,bhnjk->bhnij", q, k) * decay
    q_dec = q * jnp.exp(gc)[..., None]
    k_dec = k * jnp.exp(gc[..., -1:] - gc)[..., None]
    g_last = jnp.exp(gc[..., -1])

    def step(s, inp):
        u_c, w_c, q_c, k_c, a_c, gl = inp
        v_new = u_c - jnp.einsum("bhck,bhkv->bhcv", w_c, s)
        o = jnp.einsum("bhck,bhkv->bhcv", q_c, s) + jnp.einsum("bhcj,bhjv->bhcv", a_c, v_new)
        s = s * gl[..., None, None] + jnp.einsum("bhck,bhcv->bhkv", k_c, v_new)
        return s, o

    xs = tuple(jnp.moveaxis(z, 2, 0) for z in (u, w, q_dec, k_dec, attn, g_last))
    _, o = lax.scan(step, jnp.zeros((bsz, h, dk, dv), jnp.float32), xs)
    return o.transpose(1, 0, 3, 2, 4).reshape(bsz, t, h, dv)


def gated_deltanet(p, conv_w, a_log, dt_bias, norm_w):
    bsz, t, _ = p.shape
    h, d, wd = N_HEADS_GDN, HEAD_DIM_GDN, W_GDN
    qkv = jax.nn.silu(causal_depthwise_conv(p[..., :3 * wd], conv_w))
    z = p[..., 3 * wd:4 * wd].reshape(bsz, t, h, d)
    a = p[..., 4 * wd:4 * wd + h].astype(jnp.float32)
    b = p[..., 4 * wd + h:4 * wd + 2 * h].astype(jnp.float32)
    q = l2_normalize(qkv[..., :wd].reshape(bsz, t, h, d))
    k = l2_normalize(qkv[..., wd:2 * wd].reshape(bsz, t, h, d))
    v = qkv[..., 2 * wd:].reshape(bsz, t, h, d).astype(jnp.float32)
    beta = jax.nn.sigmoid(b)
    g = -jnp.exp(a_log.astype(jnp.float32)) * jax.nn.softplus(a + dt_bias.astype(jnp.float32))
    o = gated_delta_rule_chunked(q, k, v, g, beta).astype(p.dtype)
    o = rms_norm(o, norm_w) * jax.nn.silu(z)
    return o.reshape(bsz, t, wd)


def differential_attention(p, q_norm_w, k_norm_w, lq1, lk1, lq2, lk2, subln_w, lambda_init):
    bsz, t, _ = p.shape
    h, d, wd = N_HEADS_DIFF, HEAD_DIM_DIFF, W_DIFF
    q = rms_norm(p[..., :wd].reshape(bsz, t, h, 2, d), q_norm_w)
    k = rms_norm(p[..., wd:2 * wd].reshape(bsz, t, h, 2, d), k_norm_w)
    v = p[..., 2 * wd:3 * wd].reshape(bsz, t, h, 2 * d)
    f32 = jnp.float32
    lam = (jnp.exp(jnp.sum(lq1.astype(f32) * lk1.astype(f32)))
           - jnp.exp(jnp.sum(lq2.astype(f32) * lk2.astype(f32))) + lambda_init)
    scale = d ** -0.5
    outs = []
    for i in range(t // Q_BLOCK):
        s0 = i * Q_BLOCK
        end = s0 + Q_BLOCK
        s = jnp.einsum("bqhmd,bkhmd->bhmqk", q[:, s0:end], k[:, :end]).astype(f32) * scale
        mask = jnp.arange(end)[None, :] <= (s0 + jnp.arange(Q_BLOCK))[:, None]
        prob = jax.nn.softmax(jnp.where(mask, s, -jnp.inf), axis=-1)
        pdiff = prob[:, :, 0] - lam * prob[:, :, 1]
        outs.append(jnp.einsum("bhqk,bkhe->bqhe", pdiff.astype(v.dtype), v[:, :end]))
    o = jnp.concatenate(outs, axis=1)
    o = rms_norm(o, subln_w) * (1.0 - lambda_init)
    return o.reshape(bsz, t, wd)


def setup_inputs(seed: int = 0) -> dict:
    key = jax.random.key(seed)
    ks = iter(jax.random.split(key, 48))
    nrm = lambda shape, scale: scale * jax.random.normal(next(ks), shape, jnp.float32)
    uni = lambda shape, lo, hi: jax.random.uniform(next(ks), shape, jnp.float32, lo, hi)
    L, Lv, D = DEPTH, DEPTH - 1, D_MODEL
    dt = jnp.exp(uni((L, N_HEADS_GDN), math.log(1e-3), math.log(1e-1)))
    return {
        "x": nrm((BATCH, SEQ, D), 1.0),
        "attn_norm_w": 1.0 + nrm((L, D), 0.02),
        "w_in": nrm((L, D, N_IN), D ** -0.5),
        "w_vres_a": nrm((Lv, D, LORA_VRES), D ** -0.5),
        "mu_rwkv": uni((L, RWKV_COLS), 0.0, 1.0),
        "mu_vres": uni((Lv, LORA_VRES), 0.0, 1.0),
        "rwkv_w0": uni((L, W_RWKV), -5.0, 1.0),
        "rwkv_w_lora_b": nrm((L, LORA_DECAY, W_RWKV), 0.5 * LORA_DECAY ** -0.5),
        "rwkv_a0": nrm((L, W_RWKV), 0.5),
        "rwkv_a_lora_b": nrm((L, LORA_ICLR, W_RWKV), 0.5 * LORA_ICLR ** -0.5),
        "rwkv_g_lora_b": nrm((L, LORA_GATE, W_RWKV), LORA_GATE ** -0.5),
        "rwkv_v0": nrm((Lv, W_RWKV), 0.5),
        "rwkv_v_lora_b": nrm((Lv, LORA_VRES, W_RWKV), 0.5 * LORA_VRES ** -0.5),
        "rwkv_k_k": 0.85 + nrm((L, W_RWKV), 0.02),
        "rwkv_k_a": 1.0 + nrm((L, W_RWKV), 0.02),
        "rwkv_r_k": nrm((L, N_HEADS_RWKV, HEAD_DIM_RWKV), 0.1),
        "rwkv_ln_w": 1.0 + nrm((L, W_RWKV), 0.02),
        "rwkv_ln_b": nrm((L, W_RWKV), 0.02),
        "gdn_conv_w": nrm((L, CONV_WIDTH, 3 * W_GDN), CONV_WIDTH ** -0.5),
        "gdn_A_log": jnp.log(uni((L, N_HEADS_GDN), 1.0, 16.0)),
        "gdn_dt_bias": dt + jnp.log(-jnp.expm1(-dt)),
        "gdn_norm_w": 1.0 + nrm((L, HEAD_DIM_GDN), 0.02),
        "diff_q_norm_w": 1.0 + nrm((L, HEAD_DIM_DIFF), 0.02),
        "diff_k_norm_w": 1.0 + nrm((L, HEAD_DIM_DIFF), 0.02),
        "diff_lambda_q1": nrm((L, HEAD_DIM_DIFF), 0.1),
        "diff_lambda_k1": nrm((L, HEAD_DIM_DIFF), 0.1),
        "diff_lambda_q2": nrm((L, HEAD_DIM_DIFF), 0.1),
        "diff_lambda_k2": nrm((L, HEAD_DIM_DIFF), 0.1),
        "diff_subln_w": 1.0 + nrm((L, 2 * HEAD_DIM_DIFF), 0.02),
        "w_out": nrm((L, MIX_WIDTH, D), MIX_WIDTH ** -0.5),
        "ffn_norm_w": 1.0 + nrm((L, D), 0.02),
        "w_ffn_in": nrm((L, D, 2 * FFN_HIDDEN), D ** -0.5),
        "w_ffn_out": nrm((L, FFN_HIDDEN, D), FFN_HIDDEN ** -0.5),
    }


def reference(x, attn_norm_w, w_in, w_vres_a, mu_rwkv, mu_vres, rwkv_w0, rwkv_w_lora_b,
              rwkv_a0, rwkv_a_lora_b, rwkv_g_lora_b, rwkv_v0, rwkv_v_lora_b, rwkv_k_k,
              rwkv_k_a, rwkv_r_k, rwkv_ln_w, rwkv_ln_b, gdn_conv_w, gdn_A_log, gdn_dt_bias,
              gdn_norm_w, diff_q_norm_w, diff_k_norm_w, diff_lambda_q1, diff_lambda_k1,
              diff_lambda_q2, diff_lambda_k2, diff_subln_w, w_out, ffn_norm_w, w_ffn_in,
              w_ffn_out):
    v_first = None
    for l in range(DEPTH):
        h = rms_norm(x, attn_norm_w[l])
        if l == 0:
            w_proj, mu = w_in[l], mu_rwkv[l]
        else:
            w_proj = jnp.concatenate([w_in[l], w_vres_a[l - 1]], axis=1)
            mu = jnp.concatenate([mu_rwkv[l], mu_vres[l - 1]])
        p = h @ w_proj
        p_rw = p[..., :RWKV_COLS] if l == 0 else jnp.concatenate([p[..., :RWKV_COLS], p[..., N_IN:]], axis=-1)
        p_rw = p_rw + (token_shift(p_rw) - p_rw) * mu
        p_gdn = p[..., RWKV_COLS:RWKV_COLS + GDN_COLS]
        p_diff = p[..., RWKV_COLS + GDN_COLS:N_IN]
        if l == 0:
            y_rw, v_first = rwkv7_time_mix(
                p_rw, None, rwkv_w0[l], rwkv_w_lora_b[l], rwkv_a0[l], rwkv_a_lora_b[l],
                rwkv_g_lora_b[l], rwkv_k_k[l], rwkv_k_a[l], rwkv_r_k[l], rwkv_ln_w[l],
                rwkv_ln_b[l], None, None)
        else:
            y_rw, _ = rwkv7_time_mix(
                p_rw, v_first, rwkv_w0[l], rwkv_w_lora_b[l], rwkv_a0[l], rwkv_a_lora_b[l],
                rwkv_g_lora_b[l], rwkv_k_k[l], rwkv_k_a[l], rwkv_r_k[l], rwkv_ln_w[l],
                rwkv_ln_b[l], rwkv_v0[l - 1], rwkv_v_lora_b[l - 1])
        y_gdn = gated_deltanet(p_gdn, gdn_conv_w[l], gdn_A_log[l], gdn_dt_bias[l], gdn_norm_w[l])
        lambda_init = 0.8 - 0.6 * math.exp(-0.3 * l)
        y_diff = differential_attention(
            p_diff, diff_q_norm_w[l], diff_k_norm_w[l], diff_lambda_q1[l], diff_lambda_k1[l],
            diff_lambda_q2[l], diff_lambda_k2[l], diff_subln_w[l], lambda_init)
        mixed = jnp.concatenate([y_rw, y_gdn, y_diff], axis=-1)
        x = x + mixed @ w_out[l]
        h = rms_norm(x, ffn_norm_w[l])
        gate, up = jnp.split(h @ w_ffn_in[l], 2, axis=-1)
        x = x + (jax.nn.silu(gate) * up) @ w_ffn_out[l]
    return x
```

```python
import functools
import math

import jax
import jax.numpy as jnp
from jax import lax
from jax.experimental import pallas as pl
from jax.experimental.pallas import tpu as pltpu

F32 = jnp.float32
BF16 = jnp.bfloat16
HI = lax.Precision.HIGHEST

D_MODEL = 2048
W_RWKV = D_MODEL // 4
HEAD_DIM_RWKV = 64
LORA_DECAY = 64
LORA_ICLR = 64
LORA_VRES = 32
LORA_GATE = 128
RWKV_GN_EPS = 64e-5
W_GDN = D_MODEL // 4
HEAD_DIM_GDN = 128
N_HEADS_GDN = W_GDN // HEAD_DIM_GDN
CONV_WIDTH = 4
W_DIFF = D_MODEL // 2
HEAD_DIM_DIFF = 64
N_HEADS_DIFF = W_DIFF // (2 * HEAD_DIM_DIFF)
RWKV_COLS = 3 * W_RWKV + LORA_DECAY + LORA_ICLR + LORA_GATE
GDN_COLS = 4 * W_GDN + 2 * N_HEADS_GDN
DIFF_COLS = 3 * W_DIFF
N_IN = RWKV_COLS + GDN_COLS + DIFF_COLS
FFN_HIDDEN = -(-8 * D_MODEL // (3 * 256)) * 256
NORM_EPS = 1e-6

LANES = 128
SOLVE_BLOCK = 16

COL_VRES = RWKV_COLS
COL_GDN_AB = COL_VRES + LANES
COL_GDN = COL_GDN_AB + LANES
COL_DIFF = COL_GDN + 4 * W_GDN
N_PAD = COL_DIFF + DIFF_COLS
RW_BLOCK = COL_GDN

VMEM_LIMIT = 48 * 1024 * 1024


def _dot(a, b, precision=None):
    return jnp.dot(a, b, preferred_element_type=F32, precision=precision)


def _dot_nt(a, b, precision=None):
    return lax.dot_general(a, b, (((1,), (1,)), ((), ())), preferred_element_type=F32, precision=precision)


def _dot_tn(a, b, precision=None):
    return lax.dot_general(a, b, (((0,), (0,)), ((), ())), preferred_element_type=F32, precision=precision)


def _sigmoid(x):
    return 1.0 / (1.0 + jnp.exp(-x))


def _softplus(x):
    return jnp.maximum(x, 0.0) + jnp.log(1.0 + jnp.exp(-jnp.abs(x)))


def _unit_lower_inverse(lo_tri, n_sub, precision):
    n = lo_tri.shape[0]
    row = lax.broadcasted_iota(jnp.int32, (n, n), 0)
    col = lax.broadcasted_iota(jnp.int32, (n, n), 1)
    eye = (row == col).astype(F32)
    same = (row // SOLVE_BLOCK) == (col // SOLVE_BLOCK)
    l_diag = jnp.where(same, lo_tri, 0.0)
    l_off = jnp.where(same, 0.0, lo_tri)
    t_diag = eye + l_diag
    x = l_diag
    width = 2
    while width < SOLVE_BLOCK:
        x = _dot(x, x, precision)
        t_diag = t_diag + _dot(t_diag, x, precision)
        width *= 2
    nmat = _dot(t_diag, l_off, precision)
    t = t_diag + _dot(nmat, t_diag, precision)
    x = nmat
    width = 2
    while width < n_sub:
        x = _dot(x, x, precision)
        t = t + _dot(x, t, precision)
        width *= 2
    return t


def _norm_matmul_kernel(x_ref, nw_ref, w_ref, o_ref, h_ref):
    @pl.when(pl.program_id(1) == 0)
    def _():
        x = x_ref[...]
        ms = jnp.mean(x * x, axis=-1, keepdims=True)
        h_ref[...] = (x * lax.rsqrt(ms + NORM_EPS) * nw_ref[...]).astype(BF16)

    o_ref[...] = _dot(h_ref[...], w_ref[...])


def _norm_matmul(x2d, nw, w_bf16, tm=512, tn=512):
    n, d = x2d.shape
    npad = w_bf16.shape[1]
    return pl.pallas_call(
        _norm_matmul_kernel,
        grid=(n // tm, npad // tn),
        in_specs=[
            pl.BlockSpec((tm, d), lambda i, j: (i, 0)),
            pl.BlockSpec((1, d), lambda i, j: (0, 0)),
            pl.BlockSpec((d, tn), lambda i, j: (0, j)),
        ],
        out_specs=pl.BlockSpec((tm, tn), lambda i, j: (i, j)),
        out_shape=jax.ShapeDtypeStruct((n, npad), F32),
        scratch_shapes=[pltpu.VMEM((tm, d), BF16)],
        compiler_params=pltpu.CompilerParams(
            dimension_semantics=("parallel", "arbitrary"), vmem_limit_bytes=VMEM_LIMIT),
    )(x2d, nw.reshape(1, d), w_bf16)


def _rwkv_kernel(*refs, chunk, has_vres):
    if has_vres:
        (p_ref, vf_ref, mu_ref, w0_ref, wbw_ref, a0_ref, wba_ref, wbg_ref, v0_ref, wbv_ref,
         kk_ref, ka_ref, rk_ref, lnw_ref, lnb_ref, gmat_ref, tri_ref,
         y_ref, buf_ref, s_ref) = refs
    else:
        (p_ref, mu_ref, w0_ref, wbw_ref, a0_ref, wba_ref, wbg_ref,
         kk_ref, ka_ref, rk_ref, lnw_ref, lnb_ref, gmat_ref, tri_ref,
         y_ref, vout_ref, buf_ref, s_ref) = refs
    c = chunk
    n_pairs = W_RWKV // LANES

    @pl.when(pl.program_id(1) == 0)
    def _():
        buf_ref[0:8, :] = jnp.zeros((8, RW_BLOCK), F32)
        s_ref[...] = jnp.zeros_like(s_ref)

    x = p_ref[0]
    buf_ref[8:8 + c, :] = x
    prev = buf_ref[7:7 + c, :]
    buf_ref[0:8, :] = x[c - 8:c, :]
    pm = x + (prev - x) * mu_ref[...]

    r = pm[:, 0:W_RWKV]
    k = pm[:, W_RWKV:2 * W_RWKV]
    v = pm[:, 2 * W_RWKV:3 * W_RWKV]
    lw = pm[:, 3 * W_RWKV:3 * W_RWKV + LANES]
    xg = pm[:, 3 * W_RWKV + LANES:3 * W_RWKV + 2 * LANES]
    w_log = -_softplus(-(w0_ref[...] + _dot(jnp.tanh(lw), wbw_ref[...], HI))) - 0.5
    logd = -jnp.exp(w_log)
    a = _sigmoid(a0_ref[...] + _dot(lw, wba_ref[...], HI))
    g = _dot(_sigmoid(xg), wbg_ref[...], HI)
    if has_vres:
        xv = pm[:, COL_VRES:COL_VRES + LANES]
        v = v + (vf_ref[0] - v) * _sigmoid(v0_ref[...] + _dot(xv, wbv_ref[...], HI))
    else:
        vout_ref[0] = v

    gmat = gmat_ref[...]
    tri = tri_ref[...]
    lane = lax.broadcasted_iota(jnp.int32, (1, LANES), 1)
    m0 = (lane < HEAD_DIM_RWKV).astype(F32)
    m1 = 1.0 - m0
    n2 = 2 * c
    row = lax.broadcasted_iota(jnp.int32, (n2, n2), 0) % c
    col = lax.broadcasted_iota(jnp.int32, (n2, n2), 1) % c
    strict = row > col
    incl = row >= col

    def stack(z):
        return jnp.concatenate([z * m0, z * m1], axis=0)

    kk_all = k * kk_ref[...]
    k2_all = k * (1.0 + (a - 1.0) * ka_ref[...])
    for pi in range(n_pairs):
        sl = slice(pi * LANES, (pi + 1) * LANES)
        rp, vp, ap, k2 = r[:, sl], v[:, sl], a[:, sl], k2_all[:, sl]
        kk = kk_all[:, sl]
        kk = kk * lax.rsqrt(_dot(kk * kk, gmat, HI) + NORM_EPS)
        ld = logd[:, sl]
        gc = _dot(tri, ld, HI)
        g_end = gc[c - 1:c, :]
        e_inv = jnp.exp(-gc)
        e_end = jnp.exp(g_end - gc)
        b = kk * ap
        at = stack(-kk * jnp.exp(gc - ld))
        rt = stack(rp * jnp.exp(gc))
        bt = stack(b * e_inv)
        kt = stack(k2 * e_inv)
        vst = stack(vp)
        quad = _dot_nt(jnp.concatenate([at, rt], axis=0), jnp.concatenate([bt, kt], axis=0), HI)
        l_ab = jnp.where(strict, quad[:n2, :n2], 0.0)
        m_ak = jnp.where(strict, quad[:n2, n2:], 0.0)
        a_rb = jnp.where(incl, quad[n2:, :n2], 0.0)
        a_rk = jnp.where(incl, quad[n2:, n2:], 0.0)
        t_inv = _unit_lower_inverse(l_ab, c // SOLVE_BLOCK, HI)
        s = s_ref[pi]
        u = _dot(t_inv, _dot_nt(at, s, HI) + _dot(m_ak, vst, HI), HI)
        y_st = _dot_nt(rt, s, HI) + _dot(a_rb, u, HI) + _dot(a_rk, vst, HI)
        y = y_st[:c] + y_st[c:]
        s_ref[pi] = (s * jnp.exp(g_end) + _dot_tn(u, stack(b * e_end), HI)
                     + _dot_tn(vst, stack(k2 * e_end), HI))
        inv_n = 1.0 / HEAD_DIM_RWKV
        mean = _dot(y, gmat, HI) * inv_n
        dlt = y - mean
        var = _dot(dlt * dlt, gmat, HI) * inv_n
        yn = dlt * lax.rsqrt(var + RWKV_GN_EPS) * lnw_ref[:, sl] + lnb_ref[:, sl]
        bonus = _dot(rp * k2 * rk_ref[:, sl], gmat, HI) * vp
        y_ref[0, :, sl] = ((yn + bonus) * g[:, sl]).astype(y_ref.dtype)


def _rwkv_mix(p3d, v_first, mu_pad, w0, wbw, a0, wba, wbg, v0, wbv, k_k, k_a, r_k, ln_w, ln_b, chunk=64):
    bsz, t, _ = p3d.shape
    has_vres = v_first is not None
    row = lambda z: z.reshape(1, -1).astype(F32)
    lane = jnp.arange(LANES)
    gmat = ((lane[:, None] // HEAD_DIM_RWKV) == (lane[None, :] // HEAD_DIM_RWKV)).astype(F32)
    idx = jnp.arange(chunk)
    tri = (idx[:, None] >= idx[None, :]).astype(F32)
    zeros = jnp.zeros((LORA_DECAY, W_RWKV), F32)
    wbw_pad = jnp.concatenate([wbw, zeros], axis=0)
    wba_pad = jnp.concatenate([zeros, wba], axis=0)
    const = lambda shape: pl.BlockSpec(shape, lambda b, c: (0,) * len(shape))
    p_spec = pl.BlockSpec((1, chunk, RW_BLOCK), lambda b, c: (b, c, 0))
    seq_spec = pl.BlockSpec((1, chunk, W_RWKV), lambda b, c: (b, c, 0))
    args = [p3d]
    specs = [p_spec]
    if has_vres:
        args.append(v_first)
        specs.append(seq_spec)
    args += [mu_pad, row(w0), wbw_pad, row(a0), wba_pad, wbg]
    specs += [const((1, RW_BLOCK)), const((1, W_RWKV)), const((LANES, W_RWKV)), const((1, W_RWKV)),
              const((LANES, W_RWKV)), const((LORA_GATE, W_RWKV))]
    if has_vres:
        wbv_pad = jnp.concatenate([wbv, jnp.zeros((LANES - LORA_VRES, W_RWKV), F32)], axis=0)
        args += [row(v0), wbv_pad]
        specs += [const((1, W_RWKV)), const((LANES, W_RWKV))]
    args += [row(k_k), row(k_a), row(r_k), row(ln_w), row(ln_b), gmat, tri]
    specs += [const((1, W_RWKV))] * 5 + [const((LANES, LANES)), const((chunk, chunk))]
    y_shape = jax.ShapeDtypeStruct((bsz, t, W_RWKV), BF16)
    if has_vres:
        out_shape, out_specs = y_shape, seq_spec
    else:
        out_shape = (y_shape, jax.ShapeDtypeStruct((bsz, t, W_RWKV), F32))
        out_specs = (seq_spec, seq_spec)
    return pl.pallas_call(
        functools.partial(_rwkv_kernel, chunk=chunk, has_vres=has_vres),
        grid=(bsz, t // chunk),
        in_specs=specs,
        out_specs=out_specs,
        out_shape=out_shape,
        scratch_shapes=[pltpu.VMEM((chunk + 8, RW_BLOCK), F32),
                        pltpu.VMEM((W_RWKV // LANES, LANES, LANES), F32)],
        compiler_params=pltpu.CompilerParams(
            dimension_semantics=("parallel", "arbitrary"), vmem_limit_bytes=VMEM_LIMIT),
    )(*args)


def _gdn_kernel(x_ref, ab_ref, convw_ref, hp_ref, normw_ref, tri_ref, eye_ref, y_ref, buf_ref, s_ref, *, chunk):
    c = chunk
    wq = 3 * W_GDN

    @pl.when(pl.program_id(1) == 0)
    def _():
        buf_ref[0:8, :] = jnp.zeros((8, wq), F32)
        s_ref[...] = jnp.zeros_like(s_ref)

    x = x_ref[0]
    xc = x[:, :wq]
    buf_ref[8:8 + c, :] = xc
    conv = xc * convw_ref[3:4, :]
    for i in range(CONV_WIDTH - 1):
        conv = conv + buf_ref[5 + i:5 + i + c, :] * convw_ref[i:i + 1, :]
    buf_ref[0:8, :] = xc[c - 8:c, :]
    qkv = conv * _sigmoid(conv)

    ab = ab_ref[0]
    g_all = -jnp.exp(hp_ref[0:1, :]) * _softplus(ab + hp_ref[1:2, :])
    beta_all = _sigmoid(ab)
    gc_all = _dot(tri_ref[...], g_all, HI)
    gc_t = _dot_nt(eye_ref[...], gc_all, HI)

    row = lax.broadcasted_iota(jnp.int32, (c, c), 0)
    col = lax.broadcasted_iota(jnp.int32, (c, c), 1)
    causal = row >= col
    strict = row > col
    for h in range(N_HEADS_GDN):
        hs = slice(h * LANES, (h + 1) * LANES)
        q = qkv[:, h * LANES:(h + 1) * LANES]
        k = qkv[:, W_GDN + h * LANES:W_GDN + (h + 1) * LANES]
        v = qkv[:, 2 * W_GDN + h * LANES:2 * W_GDN + (h + 1) * LANES]
        z = x[:, wq + h * LANES:wq + (h + 1) * LANES]
        q = q * lax.rsqrt(jnp.sum(q * q, axis=-1, keepdims=True) + NORM_EPS) * (HEAD_DIM_GDN ** -0.5)
        k = k * lax.rsqrt(jnp.sum(k * k, axis=-1, keepdims=True) + NORM_EPS)
        gc = gc_all[:, h:h + 1]
        gc_row = gc_t[h:h + 1, :]
        beta = beta_all[:, N_HEADS_GDN + h:N_HEADS_GDN + h + 1]
        decay = jnp.exp(jnp.where(causal, gc - gc_row, -jnp.inf))
        kb = k * beta
        kkt = _dot_nt(kb, k, HI)
        qkt = _dot_nt(q, k, HI)
        t_inv = _unit_lower_inverse(-jnp.where(strict, kkt * decay, 0.0), c // SOLVE_BLOCK, HI)
        e_gc = jnp.exp(gc)
        u = _dot(t_inv, v * beta, HI)
        w = _dot(t_inv, kb * e_gc, HI)
        attn = qkt * decay
        g_end = gc[c - 1:c, :]
        s = s_ref[h]
        v_new = u - _dot(w, s, HI)
        o = _dot(q * e_gc, s, HI) + _dot(attn, v_new, HI)
        s_ref[h] = s * jnp.exp(g_end) + _dot_tn(k * jnp.exp(g_end - gc), v_new, HI)
        on = o * lax.rsqrt(jnp.mean(o * o, axis=-1, keepdims=True) + NORM_EPS) * normw_ref[...]
        y_ref[0, :, hs] = (on * (z * _sigmoid(z))).astype(y_ref.dtype)


def _gdn_mix(p3d, conv_w, a_log, dt_bias, norm_w, chunk=64):
    bsz, t, _ = p3d.shape
    pad = jnp.zeros((LANES - N_HEADS_GDN,), F32)
    hp = jnp.stack([jnp.concatenate([a_log.astype(F32), pad]), jnp.concatenate([dt_bias.astype(F32), pad])])
    idx = jnp.arange(chunk)
    tri = (idx[:, None] >= idx[None, :]).astype(F32)
    eye = jnp.eye(LANES, dtype=F32)
    const = lambda shape: pl.BlockSpec(shape, lambda b, c: (0,) * len(shape))
    return pl.pallas_call(
        functools.partial(_gdn_kernel, chunk=chunk),
        grid=(bsz, t // chunk),
        in_specs=[
            pl.BlockSpec((1, chunk, 4 * W_GDN), lambda b, c: (b, c, COL_GDN // (4 * W_GDN))),
            pl.BlockSpec((1, chunk, LANES), lambda b, c: (b, c, COL_GDN_AB // LANES)),
            const((CONV_WIDTH, 3 * W_GDN)), const((2, LANES)), const((1, HEAD_DIM_GDN)),
            const((chunk, chunk)), const((LANES, LANES)),
        ],
        out_specs=pl.BlockSpec((1, chunk, W_GDN), lambda b, c: (b, c, 0)),
        out_shape=jax.ShapeDtypeStruct((bsz, t, W_GDN), BF16),
        scratch_shapes=[pltpu.VMEM((chunk + 8, 3 * W_GDN), F32),
                        pltpu.VMEM((N_HEADS_GDN, HEAD_DIM_GDN, HEAD_DIM_GDN), F32)],
        compiler_params=pltpu.CompilerParams(
            dimension_semantics=("parallel", "arbitrary"), vmem_limit_bytes=VMEM_LIMIT),
    )(p3d, p3d, conv_w.astype(F32), hp, norm_w.reshape(1, -1).astype(F32), tri, eye)


def _diff_prep_kernel(q_ref, k_ref, v_ref, qw_ref, kw_ref, gmat_ref, qo_ref, ko_ref, vo_ref):
    gmat = gmat_ref[...]
    inv_d = 1.0 / HEAD_DIM_DIFF
    for hb in range(W_DIFF // LANES):
        sl = slice(hb * LANES, (hb + 1) * LANES)
        for src, wref, dst in ((q_ref, qw_ref, qo_ref), (k_ref, kw_ref, ko_ref)):
            x = src[0, :, sl]
            ms = _dot(x * x, gmat, HI) * inv_d
            dst[0, :, sl] = (x * lax.rsqrt(ms + NORM_EPS) * wref[...]).astype(BF16)
    vo_ref[...] = v_ref[...].astype(BF16)


def _diff_prep(p3d, q_norm_w, k_norm_w, tm=512):
    bsz, t, _ = p3d.shape
    tm = min(tm, t)
    lane = jnp.arange(LANES)
    gmat = ((lane[:, None] // HEAD_DIM_DIFF) == (lane[None, :] // HEAD_DIM_DIFF)).astype(F32)
    qw = (jnp.tile(q_norm_w.astype(F32), 2) * (HEAD_DIM_DIFF ** -0.5)).reshape(1, LANES)
    kw = jnp.tile(k_norm_w.astype(F32), 2).reshape(1, LANES)
    base = COL_DIFF // W_DIFF
    sec = lambda j: pl.BlockSpec((1, tm, W_DIFF), lambda b, i: (b, i, base + j))
    const = lambda shape: pl.BlockSpec(shape, lambda b, i: (0,) * len(shape))
    out_spec = pl.BlockSpec((1, tm, W_DIFF), lambda b, i: (b, i, 0))
    out = jax.ShapeDtypeStruct((bsz, t, W_DIFF), BF16)
    return pl.pallas_call(
        _diff_prep_kernel,
        grid=(bsz, t // tm),
        in_specs=[sec(0), sec(1), sec(2), const((1, LANES)), const((1, LANES)), const((LANES, LANES))],
        out_specs=(out_spec, out_spec, out_spec),
        out_shape=(out, out, out),
        compiler_params=pltpu.CompilerParams(
            dimension_semantics=("parallel", "parallel"), vmem_limit_bytes=VMEM_LIMIT),
    )(p3d, p3d, p3d, qw, kw, gmat)


def _diff_attn_kernel(q_ref, k_ref, v_ref, lam_ref, subw_ref, o_ref, m_ref, l_ref, acc_ref, *, tile, lambda_init):
    qi = pl.program_id(2)
    ki = pl.program_id(3)

    @pl.when(ki == 0)
    def _():
        m_ref[...] = jnp.full_like(m_ref, -jnp.inf)
        l_ref[...] = jnp.zeros_like(l_ref)
        acc_ref[...] = jnp.zeros_like(acc_ref)

    @pl.when(ki <= qi)
    def _():
        q = q_ref[0]
        k = k_ref[0]
        v = v_ref[0]
        lane = lax.broadcasted_iota(jnp.int32, (1, LANES), 1)
        row = qi * tile + lax.broadcasted_iota(jnp.int32, (tile, tile), 0)
        col = ki * tile + lax.broadcasted_iota(jnp.int32, (tile, tile), 1)
        visible = col <= row
        for m in range(2):
            in_map = (lane >= m * HEAD_DIM_DIFF) & (lane < (m + 1) * HEAD_DIM_DIFF)
            qm = jnp.where(in_map, q, jnp.zeros_like(q))
            s = jnp.where(visible, _dot_nt(qm, k), -jnp.inf)
            m_prev = m_ref[m]
            m_new = jnp.maximum(m_prev, jnp.max(s, axis=-1, keepdims=True))
            alpha = jnp.exp(m_prev - m_new)
            p = jnp.exp(s - m_new)
            l_ref[m] = alpha * l_ref[m] + jnp.sum(p, axis=-1, keepdims=True)
            acc_ref[m] = alpha * acc_ref[m] + _dot(p.astype(BF16), v)
            m_ref[m] = m_new

    @pl.when(ki == pl.num_programs(3) - 1)
    def _():
        lv = lam_ref[...]
        lam = (jnp.exp(jnp.sum(lv[0:1] * lv[1:2], axis=-1, keepdims=True))
               - jnp.exp(jnp.sum(lv[2:3] * lv[3:4], axis=-1, keepdims=True)) + lambda_init)
        o = acc_ref[0] / l_ref[0] - lam * (acc_ref[1] / l_ref[1])
        on = o * lax.rsqrt(jnp.mean(o * o, axis=-1, keepdims=True) + NORM_EPS) * subw_ref[...]
        o_ref[0] = (on * (1.0 - lambda_init)).astype(o_ref.dtype)


def _diff_attn(qn, kn, vb, lam_vecs, subln_w, lambda_init, tile=256):
    bsz, t, _ = qn.shape
    nq = t // tile
    q_spec = pl.BlockSpec((1, tile, LANES), lambda b, h, i, j: (b, i, h))
    kv_spec = pl.BlockSpec((1, tile, LANES), lambda b, h, i, j: (b, jnp.minimum(j, i), h))
    const = lambda shape: pl.BlockSpec(shape, lambda b, h, i, j: (0,) * len(shape))
    return pl.pallas_call(
        functools.partial(_diff_attn_kernel, tile=tile, lambda_init=lambda_init),
        grid=(bsz, N_HEADS_DIFF, nq, nq),
        in_specs=[q_spec, kv_spec, kv_spec, const((4, HEAD_DIM_DIFF)), const((1, LANES))],
        out_specs=pl.BlockSpec((1, tile, LANES), lambda b, h, i, j: (b, i, h)),
        out_shape=jax.ShapeDtypeStruct((bsz, t, W_DIFF), BF16),
        scratch_shapes=[pltpu.VMEM((2, tile, 1), F32), pltpu.VMEM((2, tile, 1), F32),
                        pltpu.VMEM((2, tile, LANES), F32)],
        compiler_params=pltpu.CompilerParams(
            dimension_semantics=("parallel", "parallel", "parallel", "arbitrary"),
            vmem_limit_bytes=VMEM_LIMIT),
    )(qn, kn, vb, lam_vecs, subln_w.reshape(1, LANES).astype(F32))


def _out_proj_kernel(x_ref, yr_ref, yg_ref, yd_ref, w1_ref, w2_ref, w3_ref, o_ref):
    o_ref[...] = (x_ref[...] + _dot(yr_ref[...], w1_ref[...]) + _dot(yg_ref[...], w2_ref[...])
                  + _dot(yd_ref[...], w3_ref[...]))


def _out_proj(x2d, y_rw, y_gdn, y_diff, w_out_bf16, tm=512):
    n, d = x2d.shape
    rowblk = lambda w: pl.BlockSpec((tm, w), lambda i: (i, 0))
    return pl.pallas_call(
        _out_proj_kernel,
        grid=(n // tm,),
        in_specs=[rowblk(d), rowblk(W_RWKV), rowblk(W_GDN), rowblk(W_DIFF),
                  pl.BlockSpec((W_RWKV, d), lambda i: (0, 0)),
                  pl.BlockSpec((W_GDN, d), lambda i: (1, 0)),
                  pl.BlockSpec((W_DIFF, d), lambda i: (1, 0))],
        out_specs=rowblk(d),
        out_shape=jax.ShapeDtypeStruct((n, d), F32),
        compiler_params=pltpu.CompilerParams(
            dimension_semantics=("parallel",), vmem_limit_bytes=VMEM_LIMIT),
    )(x2d, y_rw, y_gdn, y_diff, w_out_bf16, w_out_bf16, w_out_bf16)


def _ffn_kernel(x_ref, nw_ref, wg_ref, wu_ref, wo_ref, o_ref, h_ref, acc_ref):
    f = pl.program_id(1)

    @pl.when(f == 0)
    def _():
        x = x_ref[...]
        ms = jnp.mean(x * x, axis=-1, keepdims=True)
        h_ref[...] = (x * lax.rsqrt(ms + NORM_EPS) * nw_ref[...]).astype(BF16)
        acc_ref[...] = x

    h = h_ref[...]
    gate = _dot(h, wg_ref[...])
    up = _dot(h, wu_ref[...])
    act = (gate * _sigmoid(gate) * up).astype(BF16)
    acc_ref[...] += _dot(act, wo_ref[...])

    @pl.when(f == pl.num_programs(1) - 1)
    def _():
        o_ref[...] = acc_ref[...]


def _ffn(x2d, nw, w_in_bf16, w_out_bf16, tm=512, tf=512):
    n, d = x2d.shape
    hidden = w_out_bf16.shape[0]
    nf = hidden // tf
    return pl.pallas_call(
        _ffn_kernel,
        grid=(n // tm, nf),
        in_specs=[
            pl.BlockSpec((tm, d), lambda i, f: (i, 0)),
            pl.BlockSpec((1, d), lambda i, f: (0, 0)),
            pl.BlockSpec((d, tf), lambda i, f: (0, f)),
            pl.BlockSpec((d, tf), lambda i, f: (0, nf + f)),
            pl.BlockSpec((tf, d), lambda i, f: (f, 0)),
        ],
        out_specs=pl.BlockSpec((tm, d), lambda i, f: (i, 0)),
        out_shape=jax.ShapeDtypeStruct((n, d), F32),
        scratch_shapes=[pltpu.VMEM((tm, d), BF16), pltpu.VMEM((tm, d), F32)],
        compiler_params=pltpu.CompilerParams(
            dimension_semantics=("parallel", "arbitrary"), vmem_limit_bytes=VMEM_LIMIT),
    )(x2d, nw.reshape(1, d), w_in_bf16, w_in_bf16, w_out_bf16)


def _pack_in_proj(w_in_l, w_vres_l):
    d = w_in_l.shape[0]
    z = lambda n: jnp.zeros((d, n), w_in_l.dtype)
    gdn0 = RWKV_COLS
    vres = z(LANES) if w_vres_l is None else jnp.concatenate([w_vres_l, z(LANES - LORA_VRES)], axis=1)
    cols = [w_in_l[:, :RWKV_COLS], vres,
            w_in_l[:, gdn0 + 4 * W_GDN:gdn0 + GDN_COLS], z(LANES - 2 * N_HEADS_GDN),
            w_in_l[:, gdn0:gdn0 + 4 * W_GDN],
            w_in_l[:, gdn0 + GDN_COLS:]]
    return jnp.concatenate(cols, axis=1).astype(BF16)


def kernel(x, attn_norm_w, w_in, w_vres_a, mu_rwkv, mu_vres, rwkv_w0, rwkv_w_lora_b, rwkv_a0, rwkv_a_lora_b, rwkv_g_lora_b, rwkv_v0, rwkv_v_lora_b, rwkv_k_k, rwkv_k_a, rwkv_r_k, rwkv_ln_w, rwkv_ln_b, gdn_conv_w, gdn_A_log, gdn_dt_bias, gdn_norm_w, diff_q_norm_w, diff_k_norm_w, diff_lambda_q1, diff_lambda_k1, diff_lambda_q2, diff_lambda_k2, diff_subln_w, w_out, ffn_norm_w, w_ffn_in, w_ffn_out):
    bsz, t, d = x.shape
    depth = w_in.shape[0]
    x2d = x.reshape(bsz * t, d)
    v_first = None
    for l in range(depth):
        w_proj = _pack_in_proj(w_in[l], None if l == 0 else w_vres_a[l - 1])
        mu_tail = jnp.zeros((RW_BLOCK - RWKV_COLS,), F32)
        if l > 0:
            mu_tail = mu_tail.at[:LORA_VRES].set(mu_vres[l - 1])
        mu_pad = jnp.concatenate([mu_rwkv[l], mu_tail]).reshape(1, RW_BLOCK)
        p3d = _norm_matmul(x2d, attn_norm_w[l], w_proj).reshape(bsz, t, N_PAD)
        rw_args = (rwkv_w0[l], rwkv_w_lora_b[l], rwkv_a0[l], rwkv_a_lora_b[l], rwkv_g_lora_b[l])
        rw_tail = (rwkv_k_k[l], rwkv_k_a[l], rwkv_r_k[l], rwkv_ln_w[l], rwkv_ln_b[l])
        if l == 0:
            y_rw, v_first = _rwkv_mix(p3d, None, mu_pad, *rw_args, None, None, *rw_tail)
        else:
            y_rw = _rwkv_mix(p3d, v_first, mu_pad, *rw_args, rwkv_v0[l - 1], rwkv_v_lora_b[l - 1], *rw_tail)
        y_gdn = _gdn_mix(p3d, gdn_conv_w[l], gdn_A_log[l], gdn_dt_bias[l], gdn_norm_w[l])
        lambda_init = 0.8 - 0.6 * math.exp(-0.3 * l)
        qn, kn, vb = _diff_prep(p3d, diff_q_norm_w[l], diff_k_norm_w[l])
        lam_vecs = jnp.stack([diff_lambda_q1[l], diff_lambda_k1[l], diff_lambda_q2[l], diff_lambda_k2[l]]).astype(F32)
        y_diff = _diff_attn(qn, kn, vb, lam_vecs, diff_subln_w[l], lambda_init)
        n = bsz * t
        x2d = _out_proj(x2d, y_rw.reshape(n, W_RWKV), y_gdn.reshape(n, W_GDN), y_diff.reshape(n, W_DIFF),
                        w_out[l].astype(BF16))
        x2d = _ffn(x2d, ffn_norm_w[l], w_ffn_in[l].astype(BF16), w_ffn_out[l].astype(BF16))
    return x2d.reshape(bsz, t, d)
```

```python
import functools
import math

import jax
import jax.numpy as jnp
from jax import lax
from jax.experimental import pallas as pl
from jax.experimental.pallas import tpu as pltpu

F32 = jnp.float32
BF16 = jnp.bfloat16

D_MODEL = 2048
W_RWKV = D_MODEL // 4
HEAD_DIM_RWKV = 64
LORA_DECAY = 64
LORA_ICLR = 64
LORA_VRES = 32
LORA_GATE = 128
RWKV_GN_EPS = 64e-5
W_GDN = D_MODEL // 4
HEAD_DIM_GDN = 128
N_HEADS_GDN = W_GDN // HEAD_DIM_GDN
CONV_WIDTH = 4
W_DIFF = D_MODEL // 2
HEAD_DIM_DIFF = 64
N_HEADS_DIFF = W_DIFF // (2 * HEAD_DIM_DIFF)
RWKV_COLS = 3 * W_RWKV + LORA_DECAY + LORA_ICLR + LORA_GATE
GDN_COLS = 4 * W_GDN + 2 * N_HEADS_GDN
DIFF_COLS = 3 * W_DIFF
N_IN = RWKV_COLS + GDN_COLS + DIFF_COLS
FFN_HIDDEN = -(-8 * D_MODEL // (3 * 256)) * 256
NORM_EPS = 1e-6

LANES = 128
SOLVE_BLOCK = 16

COL_VRES = RWKV_COLS
COL_GDN_AB = COL_VRES + LANES
COL_GDN = COL_GDN_AB + LANES
COL_DIFF = COL_GDN + 4 * W_GDN
N_PAD = COL_DIFF + DIFF_COLS
RW_BLOCK = COL_GDN

VMEM_LIMIT = 48 * 1024 * 1024

PREC = {
    "lora": (1, 1),
    "ones": (1, 3),
    "gsum": (1, 1),
    "quad": (1, 1),
    "inv": (1, 1),
    "apply": (1, 1),
}


def _pieces(x, n):
    if isinstance(x, (list, tuple)):
        return list(x)[:n]
    if x.dtype == BF16:
        return [x]
    out, rem = [], x
    for i in range(n):
        piece = rem.astype(BF16)
        out.append(piece)
        if i + 1 < n:
            rem = rem - piece.astype(F32)
    return out


_DIMS = {"nn": (((1,), (0,)), ((), ())), "nt": (((1,), (1,)), ((), ())), "tn": (((0,), (0,)), ((), ()))}


def _mm(a, b, prec=(1, 1), form="nn"):
    pa, pb = _pieces(a, prec[0]), _pieces(b, prec[1])
    depth = max(len(pa), len(pb))
    terms = sorted(((i, j) for i in range(len(pa)) for j in range(len(pb)) if i + j < depth),
                   key=lambda ij: -(ij[0] + ij[1]))
    acc = None
    for i, j in terms:
        d = lax.dot_general(pa[i], pb[j], _DIMS[form], preferred_element_type=F32)
        acc = d if acc is None else acc + d
    return acc


def _dot(a, b):
    return jnp.dot(a, b, preferred_element_type=F32)


def _sigmoid(x):
    return 1.0 / (1.0 + jnp.exp(-x))


def _softplus(x):
    return jnp.maximum(x, 0.0) + jnp.log(1.0 + jnp.exp(-jnp.abs(x)))


def _split_bf16(w, n):
    out, rem = [], w.astype(F32)
    for _ in range(n):
        piece = rem.astype(BF16)
        out.append(piece)
        rem = rem - piece.astype(F32)
    return jnp.stack(out)


def _unit_lower_inverse(lo_tri, n_sub, prec):
    n = lo_tri.shape[0]
    row = lax.broadcasted_iota(jnp.int32, (n, n), 0)
    col = lax.broadcasted_iota(jnp.int32, (n, n), 1)
    eye = (row == col).astype(F32)
    same = (row // SOLVE_BLOCK) == (col // SOLVE_BLOCK)
    l_diag = jnp.where(same, lo_tri, 0.0)
    l_off = jnp.where(same, 0.0, lo_tri)
    t_diag = eye + l_diag
    x = l_diag
    width = 2
    while width < SOLVE_BLOCK:
        x = _mm(x, x, prec)
        t_diag = t_diag + _mm(t_diag, x, prec)
        width *= 2
    nmat = _mm(t_diag, l_off, prec)
    t = t_diag + _mm(nmat, t_diag, prec)
    x = nmat
    width = 2
    while width < n_sub:
        x = _mm(x, x, prec)
        t = t + _mm(x, t, prec)
        width *= 2
    return t


def _norm_matmul_kernel(x_ref, nw_ref, w_ref, o_ref, h_ref):
    @pl.when(pl.program_id(1) == 0)
    def _():
        x = x_ref[...]
        ms = jnp.mean(x * x, axis=-1, keepdims=True)
        h_ref[...] = (x * lax.rsqrt(ms + NORM_EPS) * nw_ref[...]).astype(BF16)

    o_ref[...] = _dot(h_ref[...], w_ref[...])


def _norm_matmul(x2d, nw, w_bf16, tm=512, tn=512):
    n, d = x2d.shape
    npad = w_bf16.shape[1]
    return pl.pallas_call(
        _norm_matmul_kernel,
        grid=(n // tm, npad // tn),
        in_specs=[
            pl.BlockSpec((tm, d), lambda i, j: (i, 0)),
            pl.BlockSpec((1, d), lambda i, j: (0, 0)),
            pl.BlockSpec((d, tn), lambda i, j: (0, j)),
        ],
        out_specs=pl.BlockSpec((tm, tn), lambda i, j: (i, j)),
        out_shape=jax.ShapeDtypeStruct((n, npad), F32),
        scratch_shapes=[pltpu.VMEM((tm, d), BF16)],
        compiler_params=pltpu.CompilerParams(
            dimension_semantics=("parallel", "arbitrary"), vmem_limit_bytes=VMEM_LIMIT),
    )(x2d, nw.reshape(1, d), w_bf16)


def _rwkv_kernel(*refs, chunk, has_vres):
    if has_vres:
        (p_ref, vf_ref, mu_ref, w0_ref, wbw_ref, a0_ref, wba_ref, wbg_ref, v0_ref, wbv_ref,
         kk_ref, ka_ref, rk_ref, lnw_ref, lnb_ref, gmat_ref, tri_ref,
         y_ref, buf_ref, s_ref) = refs
    else:
        (p_ref, mu_ref, w0_ref, wbw_ref, a0_ref, wba_ref, wbg_ref,
         kk_ref, ka_ref, rk_ref, lnw_ref, lnb_ref, gmat_ref, tri_ref,
         y_ref, vout_ref, buf_ref, s_ref) = refs
    c = chunk
    n_pairs = W_RWKV // LANES
    p_lora, p_ones, p_gsum = PREC["lora"], PREC["ones"], PREC["gsum"]
    p_quad, p_inv, p_apply = PREC["quad"], PREC["inv"], PREC["apply"]
    wpieces = lambda ref: [ref[i] for i in range(ref.shape[0])]

    @pl.when(pl.program_id(1) == 0)
    def _():
        buf_ref[0:8, :] = jnp.zeros((8, RW_BLOCK), F32)
        s_ref[...] = jnp.zeros_like(s_ref)

    x = p_ref[0]
    buf_ref[8:8 + c, :] = x
    prev = buf_ref[7:7 + c, :]
    buf_ref[0:8, :] = x[c - 8:c, :]
    pm = x + (prev - x) * mu_ref[...]

    r = pm[:, 0:W_RWKV]
    k = pm[:, W_RWKV:2 * W_RWKV]
    v = pm[:, 2 * W_RWKV:3 * W_RWKV]
    lw = pm[:, 3 * W_RWKV:3 * W_RWKV + LANES]
    xg = pm[:, 3 * W_RWKV + LANES:3 * W_RWKV + 2 * LANES]
    w_log = -_softplus(-(w0_ref[...] + _mm(jnp.tanh(lw), wpieces(wbw_ref), p_lora))) - 0.5
    logd = -jnp.exp(w_log)
    a = _sigmoid(a0_ref[...] + _mm(lw, wpieces(wba_ref), p_lora))
    g = _mm(_sigmoid(xg), wpieces(wbg_ref), p_lora)
    if has_vres:
        xv = pm[:, COL_VRES:COL_VRES + LANES]
        v = v + (vf_ref[0] - v) * _sigmoid(v0_ref[...] + _mm(xv, wpieces(wbv_ref), p_lora))
    else:
        vout_ref[0] = v

    gmat = gmat_ref[...]
    tri = tri_ref[...]
    lane = lax.broadcasted_iota(jnp.int32, (1, LANES), 1)
    m0 = (lane < HEAD_DIM_RWKV).astype(F32)
    m1 = 1.0 - m0
    n2 = 2 * c
    row = lax.broadcasted_iota(jnp.int32, (n2, n2), 0) % c
    col = lax.broadcasted_iota(jnp.int32, (n2, n2), 1) % c
    strict = row > col
    incl = row >= col

    def stack(z):
        return jnp.concatenate([z * m0, z * m1], axis=0)

    kk_all = k * kk_ref[...]
    k2_all = k * (1.0 + (a - 1.0) * ka_ref[...])
    for pi in range(n_pairs):
        sl = slice(pi * LANES, (pi + 1) * LANES)
        rp, vp, ap, k2 = r[:, sl], v[:, sl], a[:, sl], k2_all[:, sl]
        kk = kk_all[:, sl]
        kk = kk * lax.rsqrt(_mm(kk * kk, gmat, p_gsum) + NORM_EPS)
        ld = logd[:, sl]
        gc = _mm(tri, ld, p_ones)
        g_end = gc[c - 1:c, :]
        e_inv = jnp.exp(-gc)
        e_end = jnp.exp(g_end - gc)
        b = kk * ap
        at = stack(-kk * jnp.exp(gc - ld))
        rt = stack(rp * jnp.exp(gc))
        bt = stack(b * e_inv)
        kt = stack(k2 * e_inv)
        vst = stack(vp)
        quad = _mm(jnp.concatenate([at, rt], axis=0), jnp.concatenate([bt, kt], axis=0), p_quad, "nt")
        l_ab = jnp.where(strict, quad[:n2, :n2], 0.0)
        m_ak = jnp.where(strict, quad[:n2, n2:], 0.0)
        a_rb = jnp.where(incl, quad[n2:, :n2], 0.0)
        a_rk = jnp.where(incl, quad[n2:, n2:], 0.0)
        t_inv = _unit_lower_inverse(l_ab, c // SOLVE_BLOCK, p_inv)
        s = s_ref[pi]
        u = _mm(t_inv, _mm(at, s, p_apply, "nt") + _mm(m_ak, vst, p_apply), p_apply)
        y_st = _mm(rt, s, p_apply, "nt") + _mm(a_rb, u, p_apply) + _mm(a_rk, vst, p_apply)
        y = y_st[:c] + y_st[c:]
        s_ref[pi] = (s * jnp.exp(g_end) + _mm(u, stack(b * e_end), p_apply, "tn")
                     + _mm(vst, stack(k2 * e_end), p_apply, "tn"))
        inv_n = 1.0 / HEAD_DIM_RWKV
        mean = _mm(y, gmat, p_gsum) * inv_n
        dlt = y - mean
        var = _mm(dlt * dlt, gmat, p_gsum) * inv_n
        yn = dlt * lax.rsqrt(var + RWKV_GN_EPS) * lnw_ref[:, sl] + lnb_ref[:, sl]
        bonus = _mm(rp * k2 * rk_ref[:, sl], gmat, p_gsum) * vp
        y_ref[0, :, sl] = ((yn + bonus) * g[:, sl]).astype(y_ref.dtype)


def _rwkv_mix(p3d, v_first, mu_pad, w0, wbw, a0, wba, wbg, v0, wbv, k_k, k_a, r_k, ln_w, ln_b, chunk=64):
    bsz, t, _ = p3d.shape
    has_vres = v_first is not None
    n_lora = PREC["lora"][1]
    row = lambda z: z.reshape(1, -1).astype(F32)
    lane = jnp.arange(LANES)
    gmat = ((lane[:, None] // HEAD_DIM_RWKV) == (lane[None, :] // HEAD_DIM_RWKV)).astype(BF16)
    idx = jnp.arange(chunk)
    tri = (idx[:, None] >= idx[None, :]).astype(BF16)
    zeros = jnp.zeros((LORA_DECAY, W_RWKV), F32)
    wbw_pad = _split_bf16(jnp.concatenate([wbw, zeros], axis=0), n_lora)
    wba_pad = _split_bf16(jnp.concatenate([zeros, wba], axis=0), n_lora)
    const = lambda shape: pl.BlockSpec(shape, lambda b, c: (0,) * len(shape))
    lora_spec = const((n_lora, LANES, W_RWKV))
    p_spec = pl.BlockSpec((1, chunk, RW_BLOCK), lambda b, c: (b, c, 0))
    seq_spec = pl.BlockSpec((1, chunk, W_RWKV), lambda b, c: (b, c, 0))
    args = [p3d]
    specs = [p_spec]
    if has_vres:
        args.append(v_first)
        specs.append(seq_spec)
    args += [mu_pad, row(w0), wbw_pad, row(a0), wba_pad, _split_bf16(wbg, n_lora)]
    specs += [const((1, RW_BLOCK)), const((1, W_RWKV)), lora_spec, const((1, W_RWKV)), lora_spec, lora_spec]
    if has_vres:
        wbv_pad = jnp.concatenate([wbv, jnp.zeros((LANES - LORA_VRES, W_RWKV), F32)], axis=0)
        args += [row(v0), _split_bf16(wbv_pad, n_lora)]
        specs += [const((1, W_RWKV)), lora_spec]
    args += [row(k_k), row(k_a), row(r_k), row(ln_w), row(ln_b), gmat, tri]
    specs += [const((1, W_RWKV))] * 5 + [const((LANES, LANES)), const((chunk, chunk))]
    y_shape = jax.ShapeDtypeStruct((bsz, t, W_RWKV), BF16)
    if has_vres:
        out_shape, out_specs = y_shape, seq_spec
    else:
        out_shape = (y_shape, jax.ShapeDtypeStruct((bsz, t, W_RWKV), F32))
        out_specs = (seq_spec, seq_spec)
    return pl.pallas_call(
        functools.partial(_rwkv_kernel, chunk=chunk, has_vres=has_vres),
        grid=(bsz, t // chunk),
        in_specs=specs,
        out_specs=out_specs,
        out_shape=out_shape,
        scratch_shapes=[pltpu.VMEM((chunk + 8, RW_BLOCK), F32),
                        pltpu.VMEM((W_RWKV // LANES, LANES, LANES), F32)],
        compiler_params=pltpu.CompilerParams(
            dimension_semantics=("parallel", "arbitrary"), vmem_limit_bytes=VMEM_LIMIT),
    )(*args)


def _gdn_kernel(x_ref, ab_ref, convw_ref, hp_ref, normw_ref, tri_ref, eye_ref, y_ref, buf_ref, s_ref, *, chunk):
    c = chunk
    wq = 3 * W_GDN
    p_ones, p_quad, p_inv, p_apply = PREC["ones"], PREC["quad"], PREC["inv"], PREC["apply"]

    @pl.when(pl.program_id(1) == 0)
    def _():
        buf_ref[0:8, :] = jnp.zeros((8, wq), F32)
        s_ref[...] = jnp.zeros_like(s_ref)

    x = x_ref[0]
    xc = x[:, :wq]
    buf_ref[8:8 + c, :] = xc
    conv = xc * convw_ref[3:4, :]
    for i in range(CONV_WIDTH - 1):
        conv = conv + buf_ref[5 + i:5 + i + c, :] * convw_ref[i:i + 1, :]
    buf_ref[0:8, :] = xc[c - 8:c, :]
    qkv = conv * _sigmoid(conv)

    ab = ab_ref[0]
    g_all = -jnp.exp(hp_ref[0:1, :]) * _softplus(ab + hp_ref[1:2, :])
    beta_all = _sigmoid(ab)
    gc_all = _mm(tri_ref[...], g_all, p_ones)
    gc_t = _mm(eye_ref[...], gc_all, p_ones, "nt")

    row = lax.broadcasted_iota(jnp.int32, (c, c), 0)
    col = lax.broadcasted_iota(jnp.int32, (c, c), 1)
    causal = row >= col
    strict = row > col
    for h in range(N_HEADS_GDN):
        hs = slice(h * LANES, (h + 1) * LANES)
        q = qkv[:, h * LANES:(h + 1) * LANES]
        k = qkv[:, W_GDN + h * LANES:W_GDN + (h + 1) * LANES]
        v = qkv[:, 2 * W_GDN + h * LANES:2 * W_GDN + (h + 1) * LANES]
        z = x[:, wq + h * LANES:wq + (h + 1) * LANES]
        q = q * lax.rsqrt(jnp.sum(q * q, axis=-1, keepdims=True) + NORM_EPS) * (HEAD_DIM_GDN ** -0.5)
        k = k * lax.rsqrt(jnp.sum(k * k, axis=-1, keepdims=True) + NORM_EPS)
        gc = gc_all[:, h:h + 1]
        gc_row = gc_t[h:h + 1, :]
        beta = beta_all[:, N_HEADS_GDN + h:N_HEADS_GDN + h + 1]
        decay = jnp.exp(jnp.where(causal, gc - gc_row, -jnp.inf))
        kb = k * beta
        both = _mm(jnp.concatenate([kb, q], axis=0), k, p_quad, "nt")
        kkt, qkt = both[:c], both[c:]
        t_inv = _unit_lower_inverse(-jnp.where(strict, kkt * decay, 0.0), c // SOLVE_BLOCK, p_inv)
        e_gc = jnp.exp(gc)
        uw = _mm(t_inv, jnp.concatenate([v * beta, kb * e_gc], axis=1), p_apply)
        u, w = uw[:, :LANES], uw[:, LANES:]
        attn = qkt * decay
        g_end = gc[c - 1:c, :]
        s = s_ref[h]
        ws_qs = _mm(jnp.concatenate([w, q * e_gc], axis=0), s, p_apply)
        v_new = u - ws_qs[:c]
        o = ws_qs[c:] + _mm(attn, v_new, p_apply)
        s_ref[h] = s * jnp.exp(g_end) + _mm(k * jnp.exp(g_end - gc), v_new, p_apply, "tn")
        on = o * lax.rsqrt(jnp.mean(o * o, axis=-1, keepdims=True) + NORM_EPS) * normw_ref[...]
        y_ref[0, :, hs] = (on * (z * _sigmoid(z))).astype(y_ref.dtype)


def _gdn_mix(p3d, conv_w, a_log, dt_bias, norm_w, chunk=64):
    bsz, t, _ = p3d.shape
    pad = jnp.zeros((LANES - N_HEADS_GDN,), F32)
    hp = jnp.stack([jnp.concatenate([a_log.astype(F32), pad]), jnp.concatenate([dt_bias.astype(F32), pad])])
    idx = jnp.arange(chunk)
    tri = (idx[:, None] >= idx[None, :]).astype(BF16)
    eye = jnp.eye(LANES, dtype=BF16)
    const = lambda shape: pl.BlockSpec(shape, lambda b, c: (0,) * len(shape))
    return pl.pallas_call(
        functools.partial(_gdn_kernel, chunk=chunk),
        grid=(bsz, t // chunk),
        in_specs=[
            pl.BlockSpec((1, chunk, 4 * W_GDN), lambda b, c: (b, c, COL_GDN // (4 * W_GDN))),
            pl.BlockSpec((1, chunk, LANES), lambda b, c: (b, c, COL_GDN_AB // LANES)),
            const((CONV_WIDTH, 3 * W_GDN)), const((2, LANES)), const((1, HEAD_DIM_GDN)),
            const((chunk, chunk)), const((LANES, LANES)),
        ],
        out_specs=pl.BlockSpec((1, chunk, W_GDN), lambda b, c: (b, c, 0)),
        out_shape=jax.ShapeDtypeStruct((bsz, t, W_GDN), BF16),
        scratch_shapes=[pltpu.VMEM((chunk + 8, 3 * W_GDN), F32),
                        pltpu.VMEM((N_HEADS_GDN, HEAD_DIM_GDN, HEAD_DIM_GDN), F32)],
        compiler_params=pltpu.CompilerParams(
            dimension_semantics=("parallel", "arbitrary"), vmem_limit_bytes=VMEM_LIMIT),
    )(p3d, p3d, conv_w.astype(F32), hp, norm_w.reshape(1, -1).astype(F32), tri, eye)


def _diff_prep_kernel(q_ref, k_ref, v_ref, qw_ref, kw_ref, gmat_ref, qo_ref, ko_ref, vo_ref):
    gmat = gmat_ref[...]
    inv_d = 1.0 / HEAD_DIM_DIFF
    for hb in range(W_DIFF // LANES):
        sl = slice(hb * LANES, (hb + 1) * LANES)
        for src, wref, dst in ((q_ref, qw_ref, qo_ref), (k_ref, kw_ref, ko_ref)):
            x = src[0, :, sl]
            ms = _mm(x * x, gmat, PREC["gsum"]) * inv_d
            dst[0, :, sl] = (x * lax.rsqrt(ms + NORM_EPS) * wref[...]).astype(BF16)
    vo_ref[...] = v_ref[...].astype(BF16)


def _diff_prep(p3d, q_norm_w, k_norm_w, tm=512):
    bsz, t, _ = p3d.shape
    tm = min(tm, t)
    lane = jnp.arange(LANES)
    gmat = ((lane[:, None] // HEAD_DIM_DIFF) == (lane[None, :] // HEAD_DIM_DIFF)).astype(BF16)
    qw = (jnp.tile(q_norm_w.astype(F32), 2) * (HEAD_DIM_DIFF ** -0.5)).reshape(1, LANES)
    kw = jnp.tile(k_norm_w.astype(F32), 2).reshape(1, LANES)
    base = COL_DIFF // W_DIFF
    sec = lambda j: pl.BlockSpec((1, tm, W_DIFF), lambda b, i: (b, i, base + j))
    const = lambda shape: pl.BlockSpec(shape, lambda b, i: (0,) * len(shape))
    out_spec = pl.BlockSpec((1, tm, W_DIFF), lambda b, i: (b, i, 0))
    out = jax.ShapeDtypeStruct((bsz, t, W_DIFF), BF16)
    return pl.pallas_call(
        _diff_prep_kernel,
        grid=(bsz, t // tm),
        in_specs=[sec(0), sec(1), sec(2), const((1, LANES)), const((1, LANES)), const((LANES, LANES))],
        out_specs=(out_spec, out_spec, out_spec),
        out_shape=(out, out, out),
        compiler_params=pltpu.CompilerParams(
            dimension_semantics=("parallel", "parallel"), vmem_limit_bytes=VMEM_LIMIT),
    )(p3d, p3d, p3d, qw, kw, gmat)


def _diff_attn_kernel(q_ref, k_ref, v_ref, lam_ref, subw_ref, o_ref, qs_ref, m_ref, l_ref, acc_ref,
                      *, tq, tk, lambda_init):
    qi = pl.program_id(2)
    q = q_ref[0]
    lane = lax.broadcasted_iota(jnp.int32, (1, LANES), 1)
    zero = jnp.zeros_like(q)
    qs_ref[0:tq, :] = jnp.where(lane < HEAD_DIM_DIFF, q, zero)
    qs_ref[tq:2 * tq, :] = jnp.where(lane >= HEAD_DIM_DIFF, q, zero)
    m_ref[...] = jnp.full_like(m_ref, -jnp.inf)
    l_ref[...] = jnp.zeros_like(l_ref)
    acc_ref[...] = jnp.zeros_like(acc_ref)

    def tile(j, masked):
        start = pl.multiple_of(j * tk, tk)
        k = k_ref[0, pl.ds(start, tk), :]
        v = v_ref[0, pl.ds(start, tk), :]
        s = lax.dot_general(qs_ref[...], k, _DIMS["nt"], preferred_element_type=F32)
        if masked:
            row = qi * tq + lax.broadcasted_iota(jnp.int32, (2 * tq, tk), 0) % tq
            col = j * tk + lax.broadcasted_iota(jnp.int32, (2 * tq, tk), 1)
            s = jnp.where(col <= row, s, -jnp.inf)
        m_prev = m_ref[...]
        m_new = jnp.maximum(m_prev, jnp.max(s, axis=-1, keepdims=True))
        alpha = jnp.exp(m_prev - m_new)
        p = jnp.exp(s - m_new)
        l_ref[...] = alpha * l_ref[...] + jnp.sum(p, axis=-1, keepdims=True)
        acc_ref[...] = alpha * acc_ref[...] + _dot(p.astype(BF16), v)
        m_ref[...] = m_new

    n_full = (qi * tq) // tk

    def body(j, carry):
        tile(j, False)
        return carry

    lax.fori_loop(0, n_full, body, 0)
    tile(n_full, True)

    lv = lam_ref[...]
    lam = (jnp.exp(jnp.sum(lv[0:1] * lv[1:2], axis=-1, keepdims=True))
           - jnp.exp(jnp.sum(lv[2:3] * lv[3:4], axis=-1, keepdims=True)) + lambda_init)
    o = acc_ref[0:tq, :] / l_ref[0:tq, :] - lam * (acc_ref[tq:2 * tq, :] / l_ref[tq:2 * tq, :])
    on = o * lax.rsqrt(jnp.mean(o * o, axis=-1, keepdims=True) + NORM_EPS) * subw_ref[...]
    o_ref[0] = (on * (1.0 - lambda_init)).astype(o_ref.dtype)


def _diff_attn(qn, kn, vb, lam_vecs, subln_w, lambda_init, tq=256, tk=512):
    bsz, t, _ = qn.shape
    tk = min(tk, t)
    tq = min(tq, tk)
    q_spec = pl.BlockSpec((1, tq, LANES), lambda b, h, i: (b, i, h))
    kv_spec = pl.BlockSpec((1, t, LANES), lambda b, h, i: (b, 0, h))
    const = lambda shape: pl.BlockSpec(shape, lambda b, h, i: (0,) * len(shape))
    return pl.pallas_call(
        functools.partial(_diff_attn_kernel, tq=tq, tk=tk, lambda_init=lambda_init),
        grid=(bsz, N_HEADS_DIFF, t // tq),
        in_specs=[q_spec, kv_spec, kv_spec, const((4, HEAD_DIM_DIFF)), const((1, LANES))],
        out_specs=pl.BlockSpec((1, tq, LANES), lambda b, h, i: (b, i, h)),
        out_shape=jax.ShapeDtypeStruct((bsz, t, W_DIFF), BF16),
        scratch_shapes=[pltpu.VMEM((2 * tq, LANES), BF16), pltpu.VMEM((2 * tq, 1), F32),
                        pltpu.VMEM((2 * tq, 1), F32), pltpu.VMEM((2 * tq, LANES), F32)],
        compiler_params=pltpu.CompilerParams(
            dimension_semantics=("parallel", "parallel", "arbitrary"),
            vmem_limit_bytes=VMEM_LIMIT),
    )(qn, kn, vb, lam_vecs, subln_w.reshape(1, LANES).astype(F32))


def _out_proj_kernel(x_ref, yr_ref, yg_ref, yd_ref, w1_ref, w2_ref, w3_ref, o_ref):
    o_ref[...] = (x_ref[...] + _dot(yr_ref[...], w1_ref[...]) + _dot(yg_ref[...], w2_ref[...])
                  + _dot(yd_ref[...], w3_ref[...]))


def _out_proj(x2d, y_rw, y_gdn, y_diff, w_out_bf16, tm=512):
    n, d = x2d.shape
    rowblk = lambda w: pl.BlockSpec((tm, w), lambda i: (i, 0))
    return pl.pallas_call(
        _out_proj_kernel,
        grid=(n // tm,),
        in_specs=[rowblk(d), rowblk(W_RWKV), rowblk(W_GDN), rowblk(W_DIFF),
                  pl.BlockSpec((W_RWKV, d), lambda i: (0, 0)),
                  pl.BlockSpec((W_GDN, d), lambda i: (1, 0)),
                  pl.BlockSpec((W_DIFF, d), lambda i: (1, 0))],
        out_specs=rowblk(d),
        out_shape=jax.ShapeDtypeStruct((n, d), F32),
        compiler_params=pltpu.CompilerParams(
            dimension_semantics=("parallel",), vmem_limit_bytes=VMEM_LIMIT),
    )(x2d, y_rw, y_gdn, y_diff, w_out_bf16, w_out_bf16, w_out_bf16)


def _ffn_kernel(x_ref, nw_ref, wg_ref, wu_ref, wo_ref, o_ref, h_ref, acc_ref):
    f = pl.program_id(1)

    @pl.when(f == 0)
    def _():
        x = x_ref[...]
        ms = jnp.mean(x * x, axis=-1, keepdims=True)
        h_ref[...] = (x * lax.rsqrt(ms + NORM_EPS) * nw_ref[...]).astype(BF16)
        acc_ref[...] = x

    h = h_ref[...]
    gate = _dot(h, wg_ref[...])
    up = _dot(h, wu_ref[...])
    act = (gate * _sigmoid(gate) * up).astype(BF16)
    acc_ref[...] += _dot(act, wo_ref[...])

    @pl.when(f == pl.num_programs(1) - 1)
    def _():
        o_ref[...] = acc_ref[...]


def _ffn(x2d, nw, w_in_bf16, w_out_bf16, tm=512, tf=512):
    n, d = x2d.shape
    hidden = w_out_bf16.shape[0]
    nf = hidden // tf
    return pl.pallas_call(
        _ffn_kernel,
        grid=(n // tm, nf),
        in_specs=[
            pl.BlockSpec((tm, d), lambda i, f: (i, 0)),
            pl.BlockSpec((1, d), lambda i, f: (0, 0)),
            pl.BlockSpec((d, tf), lambda i, f: (0, f)),
            pl.BlockSpec((d, tf), lambda i, f: (0, nf + f)),
            pl.BlockSpec((tf, d), lambda i, f: (f, 0)),
        ],
        out_specs=pl.BlockSpec((tm, d), lambda i, f: (i, 0)),
        out_shape=jax.ShapeDtypeStruct((n, d), F32),
        scratch_shapes=[pltpu.VMEM((tm, d), BF16), pltpu.VMEM((tm, d), F32)],
        compiler_params=pltpu.CompilerParams(
            dimension_semantics=("parallel", "arbitrary"), vmem_limit_bytes=VMEM_LIMIT),
    )(x2d, nw.reshape(1, d), w_in_bf16, w_in_bf16, w_out_bf16)


def _pack_in_proj(w_in_l, w_vres_l):
    d = w_in_l.shape[0]
    z = lambda n: jnp.zeros((d, n), w_in_l.dtype)
    gdn0 = RWKV_COLS
    vres = z(LANES) if w_vres_l is None else jnp.concatenate([w_vres_l, z(LANES - LORA_VRES)], axis=1)
    cols = [w_in_l[:, :RWKV_COLS], vres,
            w_in_l[:, gdn0 + 4 * W_GDN:gdn0 + GDN_COLS], z(LANES - 2 * N_HEADS_GDN),
            w_in_l[:, gdn0:gdn0 + 4 * W_GDN],
            w_in_l[:, gdn0 + GDN_COLS:]]
    return jnp.concatenate(cols, axis=1).astype(BF16)


def kernel(x, attn_norm_w, w_in, w_vres_a, mu_rwkv, mu_vres, rwkv_w0, rwkv_w_lora_b, rwkv_a0, rwkv_a_lora_b, rwkv_g_lora_b, rwkv_v0, rwkv_v_lora_b, rwkv_k_k, rwkv_k_a, rwkv_r_k, rwkv_ln_w, rwkv_ln_b, gdn_conv_w, gdn_A_log, gdn_dt_bias, gdn_norm_w, diff_q_norm_w, diff_k_norm_w, diff_lambda_q1, diff_lambda_k1, diff_lambda_q2, diff_lambda_k2, diff_subln_w, w_out, ffn_norm_w, w_ffn_in, w_ffn_out):
    bsz, t, d = x.shape
    depth = w_in.shape[0]
    x2d = x.reshape(bsz * t, d)
    v_first = None
    for l in range(depth):
        w_proj = _pack_in_proj(w_in[l], None if l == 0 else w_vres_a[l - 1])
        mu_tail = jnp.zeros((RW_BLOCK - RWKV_COLS,), F32)
        if l > 0:
            mu_tail = mu_tail.at[:LORA_VRES].set(mu_vres[l - 1])
        mu_pad = jnp.concatenate([mu_rwkv[l], mu_tail]).reshape(1, RW_BLOCK)
        p3d = _norm_matmul(x2d, attn_norm_w[l], w_proj).reshape(bsz, t, N_PAD)
        rw_args = (rwkv_w0[l], rwkv_w_lora_b[l], rwkv_a0[l], rwkv_a_lora_b[l], rwkv_g_lora_b[l])
        rw_tail = (rwkv_k_k[l], rwkv_k_a[l], rwkv_r_k[l], rwkv_ln_w[l], rwkv_ln_b[l])
        if l == 0:
            y_rw, v_first = _rwkv_mix(p3d, None, mu_pad, *rw_args, None, None, *rw_tail)
        else:
            y_rw = _rwkv_mix(p3d, v_first, mu_pad, *rw_args, rwkv_v0[l - 1], rwkv_v_lora_b[l - 1], *rw_tail)
        y_gdn = _gdn_mix(p3d, gdn_conv_w[l], gdn_A_log[l], gdn_dt_bias[l], gdn_norm_w[l])
        lambda_init = 0.8 - 0.6 * math.exp(-0.3 * l)
        qn, kn, vb = _diff_prep(p3d, diff_q_norm_w[l], diff_k_norm_w[l])
        lam_vecs = jnp.stack([diff_lambda_q1[l], diff_lambda_k1[l], diff_lambda_q2[l], diff_lambda_k2[l]]).astype(F32)
        y_diff = _diff_attn(qn, kn, vb, lam_vecs, diff_subln_w[l], lambda_init)
        n = bsz * t
        x2d = _out_proj(x2d, y_rw.reshape(n, W_RWKV), y_gdn.reshape(n, W_GDN), y_diff.reshape(n, W_DIFF),
                        w_out[l].astype(BF16))
        x2d = _ffn(x2d, ffn_norm_w[l], w_ffn_in[l].astype(BF16), w_ffn_out[l].astype(BF16))
    return x2d.reshape(bsz, t, d)
```

```python
import functools
import math

import jax
import jax.numpy as jnp
from jax import lax
from jax.experimental import pallas as pl
from jax.experimental.pallas import tpu as pltpu

F32 = jnp.float32
BF16 = jnp.bfloat16

D_MODEL = 2048
W_RWKV = D_MODEL // 4
HEAD_DIM_RWKV = 64
LORA_DECAY = 64
LORA_ICLR = 64
LORA_VRES = 32
LORA_GATE = 128
RWKV_GN_EPS = 64e-5
W_GDN = D_MODEL // 4
HEAD_DIM_GDN = 128
N_HEADS_GDN = W_GDN // HEAD_DIM_GDN
CONV_WIDTH = 4
W_DIFF = D_MODEL // 2
HEAD_DIM_DIFF = 64
N_HEADS_DIFF = W_DIFF // (2 * HEAD_DIM_DIFF)
RWKV_COLS = 3 * W_RWKV + LORA_DECAY + LORA_ICLR + LORA_GATE
GDN_COLS = 4 * W_GDN + 2 * N_HEADS_GDN
DIFF_COLS = 3 * W_DIFF
N_IN = RWKV_COLS + GDN_COLS + DIFF_COLS
FFN_HIDDEN = -(-8 * D_MODEL // (3 * 256)) * 256
NORM_EPS = 1e-6

LANES = 128
SOLVE_BLOCK = 16

COL_VRES = RWKV_COLS
COL_GDN_AB = COL_VRES + LANES
COL_GDN = COL_GDN_AB + LANES
COL_DIFF = COL_GDN + 4 * W_GDN
N_PAD = COL_DIFF + DIFF_COLS
RW_BLOCK = COL_GDN

VMEM_LIMIT = 48 * 1024 * 1024

PREC = {
    "lora": (1, 1),
    "ones": (1, 3),
    "gsum": (1, 1),
    "inv": (1, 1),
}


def _pieces(x, n):
    if isinstance(x, (list, tuple)):
        return list(x)[:n]
    if x.dtype == BF16:
        return [x]
    out, rem = [], x
    for i in range(n):
        piece = rem.astype(BF16)
        out.append(piece)
        if i + 1 < n:
            rem = rem - piece.astype(F32)
    return out


_DIMS = {"nn": (((1,), (0,)), ((), ())), "nt": (((1,), (1,)), ((), ())), "tn": (((0,), (0,)), ((), ()))}
_DIMS_B = {"nn": (((2,), (1,)), ((0,), (0,))), "nt": (((2,), (2,)), ((0,), (0,))), "tn": (((1,), (1,)), ((0,), (0,)))}


def _mm(a, b, prec=(1, 1), form="nn"):
    pa, pb = _pieces(a, prec[0]), _pieces(b, prec[1])
    dims = (_DIMS_B if pa[0].ndim == 3 else _DIMS)[form]
    depth = max(len(pa), len(pb))
    terms = sorted(((i, j) for i in range(len(pa)) for j in range(len(pb)) if i + j < depth),
                   key=lambda ij: -(ij[0] + ij[1]))
    acc = None
    for i, j in terms:
        d = lax.dot_general(pa[i], pb[j], dims, preferred_element_type=F32)
        acc = d if acc is None else acc + d
    return acc


def _dot(a, b):
    return jnp.dot(a, b, preferred_element_type=F32)


def _sigmoid(x):
    return 1.0 / (1.0 + jnp.exp(-x))


def _softplus(x):
    return jnp.maximum(x, 0.0) + jnp.log(1.0 + jnp.exp(-jnp.abs(x)))


def _split_bf16(w, n):
    out, rem = [], w.astype(F32)
    for _ in range(n):
        piece = rem.astype(BF16)
        out.append(piece)
        rem = rem - piece.astype(F32)
    return jnp.stack(out)


def _block_tri(rows, chunk):
    idx = jnp.arange(rows)
    same = (idx[:, None] // chunk) == (idx[None, :] // chunk)
    return (same & (idx[:, None] >= idx[None, :])).astype(BF16)


def _unit_lower_inverse(lo_tri, n_sub, prec):
    n = lo_tri.shape[-1]
    lead = (1,) * (lo_tri.ndim - 2)
    row = lax.broadcasted_iota(jnp.int32, lead + (n, n), lo_tri.ndim - 2)
    col = lax.broadcasted_iota(jnp.int32, lead + (n, n), lo_tri.ndim - 1)
    eye = (row == col).astype(F32)
    same = (row // SOLVE_BLOCK) == (col // SOLVE_BLOCK)
    l_diag = jnp.where(same, lo_tri, 0.0)
    l_off = jnp.where(same, 0.0, lo_tri)
    t_diag = eye + l_diag
    x = l_diag
    width = 2
    while width < SOLVE_BLOCK:
        x = _mm(x, x, prec)
        t_diag = t_diag + _mm(t_diag, x, prec)
        width *= 2
    nmat = _mm(t_diag, l_off, prec)
    t = t_diag + _mm(nmat, t_diag, prec)
    x = nmat
    width = 2
    while width < n_sub:
        x = _mm(x, x, prec)
        t = t + _mm(x, t, prec)
        width *= 2
    return t


def _norm_matmul_kernel(x_ref, nw_ref, w_ref, o_ref, h_ref):
    @pl.when(pl.program_id(1) == 0)
    def _():
        x = x_ref[...]
        ms = jnp.mean(x * x, axis=-1, keepdims=True)
        h_ref[...] = (x * lax.rsqrt(ms + NORM_EPS) * nw_ref[...]).astype(BF16)

    o_ref[...] = _dot(h_ref[...], w_ref[...])


def _norm_matmul(x2d, nw, w_bf16, tm=512, tn=512):
    n, d = x2d.shape
    npad = w_bf16.shape[1]
    return pl.pallas_call(
        _norm_matmul_kernel,
        grid=(n // tm, npad // tn),
        in_specs=[
            pl.BlockSpec((tm, d), lambda i, j: (i, 0)),
            pl.BlockSpec((1, d), lambda i, j: (0, 0)),
            pl.BlockSpec((d, tn), lambda i, j: (0, j)),
        ],
        out_specs=pl.BlockSpec((tm, tn), lambda i, j: (i, j)),
        out_shape=jax.ShapeDtypeStruct((n, npad), F32),
        scratch_shapes=[pltpu.VMEM((tm, d), BF16)],
        compiler_params=pltpu.CompilerParams(
            dimension_semantics=("parallel", "arbitrary"), vmem_limit_bytes=VMEM_LIMIT),
    )(x2d, nw.reshape(1, d), w_bf16)


def _rwkv_kernel(*refs, rows, chunk, has_vres):
    if has_vres:
        (p_ref, vf_ref, mu_ref, w0_ref, wbw_ref, a0_ref, wba_ref, wbg_ref, v0_ref, wbv_ref,
         kk_ref, ka_ref, rk_ref, lnw_ref, lnb_ref, gmat_ref, tri_ref,
         y_ref, buf_ref, s_ref, ybuf_ref) = refs
    else:
        (p_ref, mu_ref, w0_ref, wbw_ref, a0_ref, wba_ref, wbg_ref,
         kk_ref, ka_ref, rk_ref, lnw_ref, lnb_ref, gmat_ref, tri_ref,
         y_ref, vout_ref, buf_ref, s_ref, ybuf_ref) = refs
    c = chunk
    ns = rows // c
    n_pairs = W_RWKV // LANES
    p_lora, p_ones, p_gsum, p_inv = PREC["lora"], PREC["ones"], PREC["gsum"], PREC["inv"]
    wpieces = lambda ref: [ref[i] for i in range(ref.shape[0])]
    pair = lambda pi: slice(pi * LANES, (pi + 1) * LANES)
    sub = lambda j: slice(j * c, (j + 1) * c)

    @pl.when(pl.program_id(1) == 0)
    def _():
        buf_ref[0:8, :] = jnp.zeros((8, RW_BLOCK), F32)
        s_ref[...] = jnp.zeros_like(s_ref)

    x = p_ref[0]
    buf_ref[8:8 + rows, :] = x
    prev = buf_ref[7:7 + rows, :]
    buf_ref[0:8, :] = x[rows - 8:rows, :]
    pm = x + (prev - x) * mu_ref[...]

    r = pm[:, 0:W_RWKV]
    k = pm[:, W_RWKV:2 * W_RWKV]
    v = pm[:, 2 * W_RWKV:3 * W_RWKV]
    lw = pm[:, 3 * W_RWKV:3 * W_RWKV + LANES]
    xg = pm[:, 3 * W_RWKV + LANES:3 * W_RWKV + 2 * LANES]
    w_log = -_softplus(-(w0_ref[...] + _mm(jnp.tanh(lw), wpieces(wbw_ref), p_lora))) - 0.5
    logd = -jnp.exp(w_log)
    a = _sigmoid(a0_ref[...] + _mm(lw, wpieces(wba_ref), p_lora))
    g = _mm(_sigmoid(xg), wpieces(wbg_ref), p_lora)
    if has_vres:
        xv = pm[:, COL_VRES:COL_VRES + LANES]
        v = v + (vf_ref[0] - v) * _sigmoid(v0_ref[...] + _mm(xv, wpieces(wbv_ref), p_lora))
    else:
        vout_ref[0] = v

    gmat = gmat_ref[...]

    def gsum(z):
        return jnp.concatenate([_mm(z[:, pair(pi)], gmat, p_gsum) for pi in range(n_pairs)], axis=1)

    kk = k * kk_ref[...]
    kk = kk * lax.rsqrt(gsum(kk * kk) + NORM_EPS)
    k2 = k * (1.0 + (a - 1.0) * ka_ref[...])
    b = kk * a
    gc = _mm(tri_ref[...], logd, p_ones)
    g_end = [gc[j * c + c - 1:j * c + c, :] for j in range(ns)]
    e_end = jnp.exp(jnp.concatenate([g_end[j] - gc[sub(j), :] for j in range(ns)], axis=0))
    e_inv = jnp.exp(-gc)

    lane = lax.broadcasted_iota(jnp.int32, (1, W_RWKV), 1)
    h0 = ((lane // HEAD_DIM_RWKV) % 2 == 0).astype(F32)
    h1 = 1.0 - h0

    def stack_all(z):
        z0 = (z * h0).astype(BF16)
        z1 = (z * h1).astype(BF16)
        return jnp.stack([jnp.concatenate([z0[sub(j), pair(pi)], z1[sub(j), pair(pi)]], axis=0)
                          for j in range(ns) for pi in range(n_pairs)])

    at = stack_all(-kk * jnp.exp(gc - logd))
    rt = stack_all(r * jnp.exp(gc))
    bt = stack_all(b * e_inv)
    kt = stack_all(k2 * e_inv)
    vst = stack_all(v)
    bh = stack_all(b * e_end)
    kh = stack_all(k2 * e_end)

    n2 = 2 * c
    quad = _mm(jnp.concatenate([at, rt], axis=1), jnp.concatenate([bt, kt], axis=1), form="nt")
    row = lax.broadcasted_iota(jnp.int32, (1, n2, n2), 1) % c
    col = lax.broadcasted_iota(jnp.int32, (1, n2, n2), 2) % c
    strict = row > col
    incl = row >= col
    l_ab = jnp.where(strict, quad[:, :n2, :n2], 0.0)
    m_ak = jnp.where(strict, quad[:, :n2, n2:], 0.0).astype(BF16)
    a_rb = jnp.where(incl, quad[:, n2:, :n2], 0.0).astype(BF16)
    a_rk = jnp.where(incl, quad[:, n2:, n2:], 0.0).astype(BF16)
    t_inv = _unit_lower_inverse(l_ab, c // SOLVE_BLOCK, p_inv).astype(BF16)
    w_st = _mm(t_inv, at).astype(BF16)
    u0 = _mm(t_inv, _mm(m_ak, vst).astype(BF16))
    y0 = _mm(a_rk, vst)
    kv = _mm(vst, kh, form="tn")

    s = s_ref[...]
    for j in range(ns):
        gs = slice(j * n_pairs, (j + 1) * n_pairs)
        s_b = s.astype(BF16)
        u = u0[gs] + _mm(w_st[gs], s_b, form="nt")
        u_b = u.astype(BF16)
        y_st = _mm(rt[gs], s_b, form="nt") + _mm(a_rb[gs], u_b) + y0[gs]
        d_end = jnp.exp(jnp.stack([g_end[j][:, pair(pi)] for pi in range(n_pairs)]))
        s = s * d_end + _mm(u_b, bh[gs], form="tn") + kv[gs]
        y = y_st[:, :c] + y_st[:, c:]
        for pi in range(n_pairs):
            ybuf_ref[sub(j), pair(pi)] = y[pi]
    s_ref[...] = s

    y = ybuf_ref[...]
    inv_n = 1.0 / HEAD_DIM_RWKV
    mean = gsum(y) * inv_n
    dlt = y - mean
    var = gsum(dlt * dlt) * inv_n
    yn = dlt * lax.rsqrt(var + RWKV_GN_EPS) * lnw_ref[...] + lnb_ref[...]
    bonus = gsum(r * k2 * rk_ref[...]) * v
    y_ref[0] = ((yn + bonus) * g).astype(y_ref.dtype)


def _rwkv_mix(p3d, v_first, mu_pad, w0, wbw, a0, wba, wbg, v0, wbv, k_k, k_a, r_k, ln_w, ln_b, rows=256, chunk=64):
    bsz, t, _ = p3d.shape
    rows = min(rows, t)
    has_vres = v_first is not None
    n_lora = PREC["lora"][1]
    row = lambda z: z.reshape(1, -1).astype(F32)
    lane = jnp.arange(LANES)
    gmat = ((lane[:, None] // HEAD_DIM_RWKV) == (lane[None, :] // HEAD_DIM_RWKV)).astype(BF16)
    zeros = jnp.zeros((LORA_DECAY, W_RWKV), F32)
    wbw_pad = _split_bf16(jnp.concatenate([wbw, zeros], axis=0), n_lora)
    wba_pad = _split_bf16(jnp.concatenate([zeros, wba], axis=0), n_lora)
    const = lambda shape: pl.BlockSpec(shape, lambda b, c: (0,) * len(shape))
    lora_spec = const((n_lora, LANES, W_RWKV))
    p_spec = pl.BlockSpec((1, rows, RW_BLOCK), lambda b, c: (b, c, 0))
    seq_spec = pl.BlockSpec((1, rows, W_RWKV), lambda b, c: (b, c, 0))
    args = [p3d]
    specs = [p_spec]
    if has_vres:
        args.append(v_first)
        specs.append(seq_spec)
    args += [mu_pad, row(w0), wbw_pad, row(a0), wba_pad, _split_bf16(wbg, n_lora)]
    specs += [const((1, RW_BLOCK)), const((1, W_RWKV)), lora_spec, const((1, W_RWKV)), lora_spec, lora_spec]
    if has_vres:
        wbv_pad = jnp.concatenate([wbv, jnp.zeros((LANES - LORA_VRES, W_RWKV), F32)], axis=0)
        args += [row(v0), _split_bf16(wbv_pad, n_lora)]
        specs += [const((1, W_RWKV)), lora_spec]
    args += [row(k_k), row(k_a), row(r_k), row(ln_w), row(ln_b), gmat, _block_tri(rows, chunk)]
    specs += [const((1, W_RWKV))] * 5 + [const((LANES, LANES)), const((rows, rows))]
    y_shape = jax.ShapeDtypeStruct((bsz, t, W_RWKV), BF16)
    if has_vres:
        out_shape, out_specs = y_shape, seq_spec
    else:
        out_shape = (y_shape, jax.ShapeDtypeStruct((bsz, t, W_RWKV), F32))
        out_specs = (seq_spec, seq_spec)
    return pl.pallas_call(
        functools.partial(_rwkv_kernel, rows=rows, chunk=chunk, has_vres=has_vres),
        grid=(bsz, t // rows),
        in_specs=specs,
        out_specs=out_specs,
        out_shape=out_shape,
        scratch_shapes=[pltpu.VMEM((rows + 8, RW_BLOCK), F32),
                        pltpu.VMEM((W_RWKV // LANES, LANES, LANES), F32),
                        pltpu.VMEM((rows, W_RWKV), F32)],
        compiler_params=pltpu.CompilerParams(
            dimension_semantics=("parallel", "arbitrary"), vmem_limit_bytes=VMEM_LIMIT),
    )(*args)


def _gdn_kernel(x_ref, ab_ref, convw_ref, hp_ref, normw_ref, tri_ref, eye_ref, ones_ref,
                y_ref, buf_ref, s_ref, obuf_ref, *, rows, chunk):
    c = chunk
    ns = rows // c
    nh = N_HEADS_GDN
    wq = 3 * W_GDN
    p_ones, p_inv = PREC["ones"], PREC["inv"]
    head = lambda h: slice(h * LANES, (h + 1) * LANES)
    sub = lambda j: slice(j * c, (j + 1) * c)

    @pl.when(pl.program_id(1) == 0)
    def _():
        buf_ref[0:8, :] = jnp.zeros((8, wq), F32)
        s_ref[...] = jnp.zeros_like(s_ref)

    x = x_ref[0]
    xc = x[:, :wq]
    buf_ref[8:8 + rows, :] = xc
    conv = xc * convw_ref[3:4, :]
    for i in range(CONV_WIDTH - 1):
        conv = conv + buf_ref[5 + i:5 + i + rows, :] * convw_ref[i:i + 1, :]
    buf_ref[0:8, :] = xc[rows - 8:rows, :]
    qkv = conv * _sigmoid(conv)

    ab = ab_ref[0]
    g_all = -jnp.exp(hp_ref[0:1, :]) * _softplus(ab + hp_ref[1:2, :])
    beta_all = _sigmoid(ab)
    gc_all = _mm(tri_ref[...], g_all, p_ones)
    gc_t = _mm(eye_ref[...], gc_all, p_ones, "nt")
    ones = ones_ref[...]

    def l2n(z):
        return z * lax.rsqrt(_mm(z * z, ones, (2, 1)) + NORM_EPS)

    chains = [(j, h) for j in range(ns) for h in range(nh)]
    k_l, kb_l, q_l, vk_l, qe_l, ke_l, gcol_l, grow_l, gend_l = [], [], [], [], [], [], [], [], []
    for h in range(nh):
        q = l2n(qkv[:, head(h)]) * (HEAD_DIM_GDN ** -0.5)
        k = l2n(qkv[:, W_GDN + h * LANES:W_GDN + (h + 1) * LANES])
        v = qkv[:, 2 * W_GDN + h * LANES:2 * W_GDN + (h + 1) * LANES]
        gcol = gc_all[:, h:h + 1]
        beta = beta_all[:, nh + h:nh + h + 1]
        g_end = [gcol[j * c + c - 1:j * c + c, :] for j in range(ns)]
        e_gc = jnp.exp(gcol)
        e_end = jnp.exp(jnp.concatenate([g_end[j] - gcol[sub(j), :] for j in range(ns)], axis=0))
        kb = k * beta
        k_l.append(k.astype(BF16))
        kb_l.append(kb.astype(BF16))
        q_l.append(q.astype(BF16))
        vk_l.append(jnp.concatenate([v * beta, kb * e_gc], axis=1).astype(BF16))
        qe_l.append((q * e_gc).astype(BF16))
        ke_l.append((k * e_end).astype(BF16))
        gcol_l.append(gcol)
        grow_l.append(gc_t[h:h + 1, :])
        gend_l.append(g_end)
    gather = lambda lst: jnp.stack([lst[h][sub(j), :] for j, h in chains])
    k_b, kb_b, q_b, vk_b, qe_b, ke_b = (gather(l) for l in (k_l, kb_l, q_l, vk_l, qe_l, ke_l))
    gcol_b = gather(gcol_l)
    grow_b = jnp.stack([grow_l[h][:, sub(j)] for j, h in chains])

    row = lax.broadcasted_iota(jnp.int32, (1, c, c), 1)
    col = lax.broadcasted_iota(jnp.int32, (1, c, c), 2)
    decay = jnp.exp(jnp.where(row >= col, gcol_b - grow_b, -jnp.inf))
    both = _mm(jnp.concatenate([kb_b, q_b], axis=1), k_b, form="nt")
    kkt, qkt = both[:, :c], both[:, c:]
    t_inv = _unit_lower_inverse(-jnp.where(row > col, kkt * decay, 0.0), c // SOLVE_BLOCK, p_inv).astype(BF16)
    uw = _mm(t_inv, vk_b)
    u = uw[:, :, :LANES]
    wq_b = jnp.concatenate([uw[:, :, LANES:].astype(BF16), qe_b], axis=1)
    attn = (qkt * decay).astype(BF16)

    s = s_ref[...]
    for j in range(ns):
        gs = slice(j * nh, (j + 1) * nh)
        ws_qs = _mm(wq_b[gs], s.astype(BF16))
        v_new = (u[gs] - ws_qs[:, :c]).astype(BF16)
        o = ws_qs[:, c:] + _mm(attn[gs], v_new)
        d_end = jnp.exp(jnp.stack([gend_l[h][j] for h in range(nh)]))
        s = s * d_end + _mm(ke_b[gs], v_new, form="tn")
        for h in range(nh):
            obuf_ref[sub(j), head(h)] = o[h]
    s_ref[...] = s

    o = obuf_ref[...]
    ms = jnp.concatenate([_mm(o[:, head(h)] * o[:, head(h)], ones, (2, 1)) for h in range(nh)], axis=1)
    on = o * lax.rsqrt(ms * (1.0 / HEAD_DIM_GDN) + NORM_EPS) * normw_ref[...]
    z = x[:, wq:]
    y_ref[0] = (on * (z * _sigmoid(z))).astype(y_ref.dtype)


def _gdn_mix(p3d, conv_w, a_log, dt_bias, norm_w, rows=256, chunk=128):
    bsz, t, _ = p3d.shape
    rows = min(rows, t)
    pad = jnp.zeros((LANES - N_HEADS_GDN,), F32)
    hp = jnp.stack([jnp.concatenate([a_log.astype(F32), pad]), jnp.concatenate([dt_bias.astype(F32), pad])])
    eye = jnp.eye(LANES, dtype=BF16)
    ones = jnp.ones((LANES, LANES), BF16)
    normw = jnp.tile(norm_w.astype(F32), N_HEADS_GDN).reshape(1, W_GDN)
    const = lambda shape: pl.BlockSpec(shape, lambda b, c: (0,) * len(shape))
    return pl.pallas_call(
        functools.partial(_gdn_kernel, rows=rows, chunk=chunk),
        grid=(bsz, t // rows),
        in_specs=[
            pl.BlockSpec((1, rows, 4 * W_GDN), lambda b, c: (b, c, COL_GDN // (4 * W_GDN))),
            pl.BlockSpec((1, rows, LANES), lambda b, c: (b, c, COL_GDN_AB // LANES)),
            const((CONV_WIDTH, 3 * W_GDN)), const((2, LANES)), const((1, W_GDN)),
            const((rows, rows)), const((LANES, LANES)), const((LANES, LANES)),
        ],
        out_specs=pl.BlockSpec((1, rows, W_GDN), lambda b, c: (b, c, 0)),
        out_shape=jax.ShapeDtypeStruct((bsz, t, W_GDN), BF16),
        scratch_shapes=[pltpu.VMEM((rows + 8, 3 * W_GDN), F32),
                        pltpu.VMEM((N_HEADS_GDN, HEAD_DIM_GDN, HEAD_DIM_GDN), F32),
                        pltpu.VMEM((rows, W_GDN), F32)],
        compiler_params=pltpu.CompilerParams(
            dimension_semantics=("parallel", "arbitrary"), vmem_limit_bytes=VMEM_LIMIT),
    )(p3d, p3d, conv_w.astype(F32), hp, normw, _block_tri(rows, chunk), eye, ones)


def _diff_prep_kernel(q_ref, k_ref, v_ref, qw_ref, kw_ref, gmat_ref, qo_ref, ko_ref, vo_ref):
    gmat = gmat_ref[...]
    inv_d = 1.0 / HEAD_DIM_DIFF
    for hb in range(W_DIFF // LANES):
        sl = slice(hb * LANES, (hb + 1) * LANES)
        for src, wref, dst in ((q_ref, qw_ref, qo_ref), (k_ref, kw_ref, ko_ref)):
            x = src[0, :, sl]
            ms = _mm(x * x, gmat, PREC["gsum"]) * inv_d
            dst[0, :, sl] = (x * lax.rsqrt(ms + NORM_EPS) * wref[...]).astype(BF16)
    vo_ref[...] = v_ref[...].astype(BF16)


def _diff_prep(p3d, q_norm_w, k_norm_w, tm=512):
    bsz, t, _ = p3d.shape
    tm = min(tm, t)
    lane = jnp.arange(LANES)
    gmat = ((lane[:, None] // HEAD_DIM_DIFF) == (lane[None, :] // HEAD_DIM_DIFF)).astype(BF16)
    qw = (jnp.tile(q_norm_w.astype(F32), 2) * (HEAD_DIM_DIFF ** -0.5)).reshape(1, LANES)
    kw = jnp.tile(k_norm_w.astype(F32), 2).reshape(1, LANES)
    base = COL_DIFF // W_DIFF
    sec = lambda j: pl.BlockSpec((1, tm, W_DIFF), lambda b, i: (b, i, base + j))
    const = lambda shape: pl.BlockSpec(shape, lambda b, i: (0,) * len(shape))
    out_spec = pl.BlockSpec((1, tm, W_DIFF), lambda b, i: (b, i, 0))
    out = jax.ShapeDtypeStruct((bsz, t, W_DIFF), BF16)
    return pl.pallas_call(
        _diff_prep_kernel,
        grid=(bsz, t // tm),
        in_specs=[sec(0), sec(1), sec(2), const((1, LANES)), const((1, LANES)), const((LANES, LANES))],
        out_specs=(out_spec, out_spec, out_spec),
        out_shape=(out, out, out),
        compiler_params=pltpu.CompilerParams(
            dimension_semantics=("parallel", "parallel"), vmem_limit_bytes=VMEM_LIMIT),
    )(p3d, p3d, p3d, qw, kw, gmat)


def _diff_attn_kernel(q_ref, k_ref, v_ref, lam_ref, subw_ref, o_ref, qs_ref, m_ref, l_ref, acc_ref,
                      *, tq, tk, lambda_init):
    qi = pl.program_id(2)
    q = q_ref[0]
    lane = lax.broadcasted_iota(jnp.int32, (1, LANES), 1)
    zero = jnp.zeros_like(q)
    qs_ref[0:tq, :] = jnp.where(lane < HEAD_DIM_DIFF, q, zero)
    qs_ref[tq:2 * tq, :] = jnp.where(lane >= HEAD_DIM_DIFF, q, zero)
    m_ref[...] = jnp.full_like(m_ref, -jnp.inf)
    l_ref[...] = jnp.zeros_like(l_ref)
    acc_ref[...] = jnp.zeros_like(acc_ref)

    def tile(j, masked):
        start = pl.multiple_of(j * tk, tk)
        k = k_ref[0, pl.ds(start, tk), :]
        v = v_ref[0, pl.ds(start, tk), :]
        s = lax.dot_general(qs_ref[...], k, _DIMS["nt"], preferred_element_type=F32)
        if masked:
            row = qi * tq + lax.broadcasted_iota(jnp.int32, (2 * tq, tk), 0) % tq
            col = j * tk + lax.broadcasted_iota(jnp.int32, (2 * tq, tk), 1)
            s = jnp.where(col <= row, s, -jnp.inf)
        m_prev = m_ref[...]
        m_new = jnp.maximum(m_prev, jnp.max(s, axis=-1, keepdims=True))
        alpha = jnp.exp(m_prev - m_new)
        p = jnp.exp(s - m_new)
        l_ref[...] = alpha * l_ref[...] + jnp.sum(p, axis=-1, keepdims=True)
        acc_ref[...] = alpha * acc_ref[...] + _dot(p.astype(BF16), v)
        m_ref[...] = m_new

    n_full = (qi * tq) // tk

    def body(j, carry):
        tile(j, False)
        return carry

    lax.fori_loop(0, n_full, body, 0)
    tile(n_full, True)

    lv = lam_ref[...]
    lam = (jnp.exp(jnp.sum(lv[0:1] * lv[1:2], axis=-1, keepdims=True))
           - jnp.exp(jnp.sum(lv[2:3] * lv[3:4], axis=-1, keepdims=True)) + lambda_init)
    o = acc_ref[0:tq, :] / l_ref[0:tq, :] - lam * (acc_ref[tq:2 * tq, :] / l_ref[tq:2 * tq, :])
    on = o * lax.rsqrt(jnp.mean(o * o, axis=-1, keepdims=True) + NORM_EPS) * subw_ref[...]
    o_ref[0] = (on * (1.0 - lambda_init)).astype(o_ref.dtype)


def _diff_attn(qn, kn, vb, lam_vecs, subln_w, lambda_init, tq=256, tk=512):
    bsz, t, _ = qn.shape
    tk = min(tk, t)
    tq = min(tq, tk)
    q_spec = pl.BlockSpec((1, tq, LANES), lambda b, h, i: (b, i, h))
    kv_spec = pl.BlockSpec((1, t, LANES), lambda b, h, i: (b, 0, h))
    const = lambda shape: pl.BlockSpec(shape, lambda b, h, i: (0,) * len(shape))
    return pl.pallas_call(
        functools.partial(_diff_attn_kernel, tq=tq, tk=tk, lambda_init=lambda_init),
        grid=(bsz, N_HEADS_DIFF, t // tq),
        in_specs=[q_spec, kv_spec, kv_spec, const((4, HEAD_DIM_DIFF)), const((1, LANES))],
        out_specs=pl.BlockSpec((1, tq, LANES), lambda b, h, i: (b, i, h)),
        out_shape=jax.ShapeDtypeStruct((bsz, t, W_DIFF), BF16),
        scratch_shapes=[pltpu.VMEM((2 * tq, LANES), BF16), pltpu.VMEM((2 * tq, 1), F32),
                        pltpu.VMEM((2 * tq, 1), F32), pltpu.VMEM((2 * tq, LANES), F32)],
        compiler_params=pltpu.CompilerParams(
            dimension_semantics=("parallel", "parallel", "arbitrary"),
            vmem_limit_bytes=VMEM_LIMIT),
    )(qn, kn, vb, lam_vecs, subln_w.reshape(1, LANES).astype(F32))


def _out_proj_kernel(x_ref, yr_ref, yg_ref, yd_ref, w1_ref, w2_ref, w3_ref, o_ref):
    o_ref[...] = (x_ref[...] + _dot(yr_ref[...], w1_ref[...]) + _dot(yg_ref[...], w2_ref[...])
                  + _dot(yd_ref[...], w3_ref[...]))


def _out_proj(x2d, y_rw, y_gdn, y_diff, w_out_bf16, tm=512):
    n, d = x2d.shape
    rowblk = lambda w: pl.BlockSpec((tm, w), lambda i: (i, 0))
    return pl.pallas_call(
        _out_proj_kernel,
        grid=(n // tm,),
        in_specs=[rowblk(d), rowblk(W_RWKV), rowblk(W_GDN), rowblk(W_DIFF),
                  pl.BlockSpec((W_RWKV, d), lambda i: (0, 0)),
                  pl.BlockSpec((W_GDN, d), lambda i: (1, 0)),
                  pl.BlockSpec((W_DIFF, d), lambda i: (1, 0))],
        out_specs=rowblk(d),
        out_shape=jax.ShapeDtypeStruct((n, d), F32),
        compiler_params=pltpu.CompilerParams(
            dimension_semantics=("parallel",), vmem_limit_bytes=VMEM_LIMIT),
    )(x2d, y_rw, y_gdn, y_diff, w_out_bf16, w_out_bf16, w_out_bf16)


def _ffn_kernel(x_ref, nw_ref, wg_ref, wu_ref, wo_ref, o_ref, h_ref, acc_ref):
    f = pl.program_id(1)

    @pl.when(f == 0)
    def _():
        x = x_ref[...]
        ms = jnp.mean(x * x, axis=-1, keepdims=True)
        h_ref[...] = (x * lax.rsqrt(ms + NORM_EPS) * nw_ref[...]).astype(BF16)
        acc_ref[...] = x

    h = h_ref[...]
    gate = _dot(h, wg_ref[...])
    up = _dot(h, wu_ref[...])
    act = (gate * _sigmoid(gate) * up).astype(BF16)
    acc_ref[...] += _dot(act, wo_ref[...])

    @pl.when(f == pl.num_programs(1) - 1)
    def _():
        o_ref[...] = acc_ref[...]


def _ffn(x2d, nw, w_in_bf16, w_out_bf16, tm=512, tf=512):
    n, d = x2d.shape
    hidden = w_out_bf16.shape[0]
    nf = hidden // tf
    return pl.pallas_call(
        _ffn_kernel,
        grid=(n // tm, nf),
        in_specs=[
            pl.BlockSpec((tm, d), lambda i, f: (i, 0)),
            pl.BlockSpec((1, d), lambda i, f: (0, 0)),
            pl.BlockSpec((d, tf), lambda i, f: (0, f)),
            pl.BlockSpec((d, tf), lambda i, f: (0, nf + f)),
            pl.BlockSpec((tf, d), lambda i, f: (f, 0)),
        ],
        out_specs=pl.BlockSpec((tm, d), lambda i, f: (i, 0)),
        out_shape=jax.ShapeDtypeStruct((n, d), F32),
        scratch_shapes=[pltpu.VMEM((tm, d), BF16), pltpu.VMEM((tm, d), F32)],
        compiler_params=pltpu.CompilerParams(
            dimension_semantics=("parallel", "arbitrary"), vmem_limit_bytes=VMEM_LIMIT),
    )(x2d, nw.reshape(1, d), w_in_bf16, w_in_bf16, w_out_bf16)


def _pack_in_proj(w_in_l, w_vres_l):
    d = w_in_l.shape[0]
    z = lambda n: jnp.zeros((d, n), w_in_l.dtype)
    gdn0 = RWKV_COLS
    vres = z(LANES) if w_vres_l is None else jnp.concatenate([w_vres_l, z(LANES - LORA_VRES)], axis=1)
    cols = [w_in_l[:, :RWKV_COLS], vres,
            w_in_l[:, gdn0 + 4 * W_GDN:gdn0 + GDN_COLS], z(LANES - 2 * N_HEADS_GDN),
            w_in_l[:, gdn0:gdn0 + 4 * W_GDN],
            w_in_l[:, gdn0 + GDN_COLS:]]
    return jnp.concatenate(cols, axis=1).astype(BF16)


def kernel(x, attn_norm_w, w_in, w_vres_a, mu_rwkv, mu_vres, rwkv_w0, rwkv_w_lora_b, rwkv_a0, rwkv_a_lora_b, rwkv_g_lora_b, rwkv_v0, rwkv_v_lora_b, rwkv_k_k, rwkv_k_a, rwkv_r_k, rwkv_ln_w, rwkv_ln_b, gdn_conv_w, gdn_A_log, gdn_dt_bias, gdn_norm_w, diff_q_norm_w, diff_k_norm_w, diff_lambda_q1, diff_lambda_k1, diff_lambda_q2, diff_lambda_k2, diff_subln_w, w_out, ffn_norm_w, w_ffn_in, w_ffn_out):
    bsz, t, d = x.shape
    depth = w_in.shape[0]
    x2d = x.reshape(bsz * t, d)
    v_first = None
    for l in range(depth):
        w_proj = _pack_in_proj(w_in[l], None if l == 0 else w_vres_a[l - 1])
        mu_tail = jnp.zeros((RW_BLOCK - RWKV_COLS,), F32)
        if l > 0:
            mu_tail = mu_tail.at[:LORA_VRES].set(mu_vres[l - 1])
        mu_pad = jnp.concatenate([mu_rwkv[l], mu_tail]).reshape(1, RW_BLOCK)
        p3d = _norm_matmul(x2d, attn_norm_w[l], w_proj).reshape(bsz, t, N_PAD)
        rw_args = (rwkv_w0[l], rwkv_w_lora_b[l], rwkv_a0[l], rwkv_a_lora_b[l], rwkv_g_lora_b[l])
        rw_tail = (rwkv_k_k[l], rwkv_k_a[l], rwkv_r_k[l], rwkv_ln_w[l], rwkv_ln_b[l])
        if l == 0:
            y_rw, v_first = _rwkv_mix(p3d, None, mu_pad, *rw_args, None, None, *rw_tail)
        else:
            y_rw = _rwkv_mix(p3d, v_first, mu_pad, *rw_args, rwkv_v0[l - 1], rwkv_v_lora_b[l - 1], *rw_tail)
        y_gdn = _gdn_mix(p3d, gdn_conv_w[l], gdn_A_log[l], gdn_dt_bias[l], gdn_norm_w[l])
        lambda_init = 0.8 - 0.6 * math.exp(-0.3 * l)
        qn, kn, vb = _diff_prep(p3d, diff_q_norm_w[l], diff_k_norm_w[l])
        lam_vecs = jnp.stack([diff_lambda_q1[l], diff_lambda_k1[l], diff_lambda_q2[l], diff_lambda_k2[l]]).astype(F32)
        y_diff = _diff_attn(qn, kn, vb, lam_vecs, diff_subln_w[l], lambda_init)
        n = bsz * t
        x2d = _out_proj(x2d, y_rw.reshape(n, W_RWKV), y_gdn.reshape(n, W_GDN), y_diff.reshape(n, W_DIFF),
                        w_out[l].astype(BF16))
        x2d = _ffn(x2d, ffn_norm_w[l], w_ffn_in[l].astype(BF16), w_ffn_out[l].astype(BF16))
    return x2d.reshape(bsz, t, d)
```

```python
import functools
import math

import jax
import jax.numpy as jnp
from jax import lax
from jax.experimental import pallas as pl
from jax.experimental.pallas import tpu as pltpu

F32 = jnp.float32
BF16 = jnp.bfloat16

D_MODEL = 2048
W_RWKV = D_MODEL // 4
HEAD_DIM_RWKV = 64
LORA_DECAY = 64
LORA_ICLR = 64
LORA_VRES = 32
LORA_GATE = 128
RWKV_GN_EPS = 64e-5
W_GDN = D_MODEL // 4
HEAD_DIM_GDN = 128
N_HEADS_GDN = W_GDN // HEAD_DIM_GDN
CONV_WIDTH = 4
W_DIFF = D_MODEL // 2
HEAD_DIM_DIFF = 64
N_HEADS_DIFF = W_DIFF // (2 * HEAD_DIM_DIFF)
RWKV_COLS = 3 * W_RWKV + LORA_DECAY + LORA_ICLR + LORA_GATE
GDN_COLS = 4 * W_GDN + 2 * N_HEADS_GDN
DIFF_COLS = 3 * W_DIFF
N_IN = RWKV_COLS + GDN_COLS + DIFF_COLS
FFN_HIDDEN = -(-8 * D_MODEL // (3 * 256)) * 256
NORM_EPS = 1e-6

LANES = 128
SOLVE_BLOCK = 16

COL_VRES = RWKV_COLS
COL_GDN_AB = COL_VRES + LANES
COL_GDN = COL_GDN_AB + LANES
COL_DIFF = COL_GDN + 4 * W_GDN
N_PAD = COL_DIFF + DIFF_COLS
RW_BLOCK = COL_GDN

VMEM_LIMIT = 48 * 1024 * 1024

PREC = {
    "lora": (1, 1),
    "ones": (1, 3),
    "gsum": (1, 1),
    "inv": (1, 1),
}


def _pieces(x, n):
    if isinstance(x, (list, tuple)):
        return list(x)[:n]
    if x.dtype == BF16:
        return [x]
    out, rem = [], x
    for i in range(n):
        piece = rem.astype(BF16)
        out.append(piece)
        if i + 1 < n:
            rem = rem - piece.astype(F32)
    return out


_DIMS = {"nn": (((1,), (0,)), ((), ())), "nt": (((1,), (1,)), ((), ())), "tn": (((0,), (0,)), ((), ()))}
_DIMS_B = {"nn": (((2,), (1,)), ((0,), (0,))), "nt": (((2,), (2,)), ((0,), (0,))), "tn": (((1,), (1,)), ((0,), (0,)))}


def _mm(a, b, prec=(1, 1), form="nn"):
    pa, pb = _pieces(a, prec[0]), _pieces(b, prec[1])
    dims = (_DIMS_B if pa[0].ndim == 3 else _DIMS)[form]
    depth = max(len(pa), len(pb))
    terms = sorted(((i, j) for i in range(len(pa)) for j in range(len(pb)) if i + j < depth),
                   key=lambda ij: -(ij[0] + ij[1]))
    acc = None
    for i, j in terms:
        d = lax.dot_general(pa[i], pb[j], dims, preferred_element_type=F32)
        acc = d if acc is None else acc + d
    return acc


def _dot(a, b):
    return jnp.dot(a, b, preferred_element_type=F32)


def _sigmoid(x):
    return 1.0 / (1.0 + jnp.exp(-x))


def _softplus(x):
    return jnp.maximum(x, 0.0) + jnp.log(1.0 + jnp.exp(-jnp.abs(x)))


def _split_bf16(w, n):
    out, rem = [], w.astype(F32)
    for _ in range(n):
        piece = rem.astype(BF16)
        out.append(piece)
        rem = rem - piece.astype(F32)
    return jnp.stack(out)


def _block_tri(rows, chunk):
    idx = jnp.arange(rows)
    same = (idx[:, None] // chunk) == (idx[None, :] // chunk)
    return (same & (idx[:, None] >= idx[None, :])).astype(BF16)


def _unit_lower_inverse(lo_tri, n_sub, prec):
    n = lo_tri.shape[-1]
    lead = (1,) * (lo_tri.ndim - 2)
    row = lax.broadcasted_iota(jnp.int32, lead + (n, n), lo_tri.ndim - 2)
    col = lax.broadcasted_iota(jnp.int32, lead + (n, n), lo_tri.ndim - 1)
    eye = (row == col).astype(F32)
    same = (row // SOLVE_BLOCK) == (col // SOLVE_BLOCK)
    l_diag = jnp.where(same, lo_tri, 0.0)
    l_off = jnp.where(same, 0.0, lo_tri)
    t_diag = eye + l_diag
    x = l_diag
    width = 2
    while width < SOLVE_BLOCK:
        x = _mm(x, x, prec)
        t_diag = t_diag + _mm(t_diag, x, prec)
        width *= 2
    nmat = _mm(t_diag, l_off, prec)
    t = t_diag + _mm(nmat, t_diag, prec)
    x = nmat
    width = 2
    while width < n_sub:
        x = _mm(x, x, prec)
        t = t + _mm(x, t, prec)
        width *= 2
    return t


def _norm_matmul_kernel(x_ref, nw_ref, w_ref, o_ref, h_ref):
    @pl.when(pl.program_id(1) == 0)
    def _():
        x = x_ref[...]
        ms = jnp.mean(x * x, axis=-1, keepdims=True)
        h_ref[...] = (x * lax.rsqrt(ms + NORM_EPS) * nw_ref[...]).astype(BF16)

    o_ref[...] = _dot(h_ref[...], w_ref[...])


def _norm_matmul(x2d, nw, w_bf16, tm=1024, tn=512):
    n, d = x2d.shape
    tm = min(tm, n)
    npad = w_bf16.shape[1]
    return pl.pallas_call(
        _norm_matmul_kernel,
        grid=(n // tm, npad // tn),
        in_specs=[
            pl.BlockSpec((tm, d), lambda i, j: (i, 0)),
            pl.BlockSpec((1, d), lambda i, j: (0, 0)),
            pl.BlockSpec((d, tn), lambda i, j: (0, j)),
        ],
        out_specs=pl.BlockSpec((tm, tn), lambda i, j: (i, j)),
        out_shape=jax.ShapeDtypeStruct((n, npad), F32),
        scratch_shapes=[pltpu.VMEM((tm, d), BF16)],
        compiler_params=pltpu.CompilerParams(
            dimension_semantics=("parallel", "arbitrary"), vmem_limit_bytes=VMEM_LIMIT),
    )(x2d, nw.reshape(1, d), w_bf16)


def _rwkv_kernel(*refs, rows, chunk, has_vres):
    if has_vres:
        (p_ref, vf_ref, mu_ref, w0_ref, wbw_ref, a0_ref, wba_ref, wbg_ref, v0_ref, wbv_ref,
         kk_ref, ka_ref, rk_ref, lnw_ref, lnb_ref, gmat_ref, tri_ref,
         y_ref, buf_ref, s_ref, ybuf_ref) = refs
    else:
        (p_ref, mu_ref, w0_ref, wbw_ref, a0_ref, wba_ref, wbg_ref,
         kk_ref, ka_ref, rk_ref, lnw_ref, lnb_ref, gmat_ref, tri_ref,
         y_ref, vout_ref, buf_ref, s_ref, ybuf_ref) = refs
    c = chunk
    ns = rows // c
    n_pairs = W_RWKV // LANES
    p_lora, p_ones, p_gsum, p_inv = PREC["lora"], PREC["ones"], PREC["gsum"], PREC["inv"]
    wpieces = lambda ref: [ref[i] for i in range(ref.shape[0])]
    pair = lambda pi: slice(pi * LANES, (pi + 1) * LANES)
    sub = lambda j: slice(j * c, (j + 1) * c)

    @pl.when(pl.program_id(1) == 0)
    def _():
        buf_ref[0:8, :] = jnp.zeros((8, RW_BLOCK), F32)
        s_ref[...] = jnp.zeros_like(s_ref)

    x = p_ref[0]
    buf_ref[8:8 + rows, :] = x
    prev = buf_ref[7:7 + rows, :]
    buf_ref[0:8, :] = x[rows - 8:rows, :]
    pm = x + (prev - x) * mu_ref[...]

    r = pm[:, 0:W_RWKV]
    k = pm[:, W_RWKV:2 * W_RWKV]
    v = pm[:, 2 * W_RWKV:3 * W_RWKV]
    lw = pm[:, 3 * W_RWKV:3 * W_RWKV + LANES]
    xg = pm[:, 3 * W_RWKV + LANES:3 * W_RWKV + 2 * LANES]
    w_log = -_softplus(-(w0_ref[...] + _mm(jnp.tanh(lw), wpieces(wbw_ref), p_lora))) - 0.5
    logd = -jnp.exp(w_log)
    a = _sigmoid(a0_ref[...] + _mm(lw, wpieces(wba_ref), p_lora))
    g = _mm(_sigmoid(xg), wpieces(wbg_ref), p_lora)
    if has_vres:
        xv = pm[:, COL_VRES:COL_VRES + LANES]
        v = v + (vf_ref[0] - v) * _sigmoid(v0_ref[...] + _mm(xv, wpieces(wbv_ref), p_lora))
    else:
        vout_ref[0] = v

    gmat = gmat_ref[...]

    def gsum(z):
        return jnp.concatenate([_mm(z[:, pair(pi)], gmat, p_gsum) for pi in range(n_pairs)], axis=1)

    kk = k * kk_ref[...]
    kk = kk * lax.rsqrt(gsum(kk * kk) + NORM_EPS)
    k2 = k * (1.0 + (a - 1.0) * ka_ref[...])
    b = kk * a
    gc = _mm(tri_ref[...], logd, p_ones)
    g_end = [gc[j * c + c - 1:j * c + c, :] for j in range(ns)]
    e_end = jnp.exp(jnp.concatenate([g_end[j] - gc[sub(j), :] for j in range(ns)], axis=0))
    e_inv = jnp.exp(-gc)

    lane = lax.broadcasted_iota(jnp.int32, (1, W_RWKV), 1)
    h0 = ((lane // HEAD_DIM_RWKV) % 2 == 0).astype(F32)
    h1 = 1.0 - h0

    def stack_all(z):
        z0 = (z * h0).astype(BF16)
        z1 = (z * h1).astype(BF16)
        return jnp.stack([jnp.concatenate([z0[sub(j), pair(pi)], z1[sub(j), pair(pi)]], axis=0)
                          for j in range(ns) for pi in range(n_pairs)])

    at = stack_all(-kk * jnp.exp(gc - logd))
    rt = stack_all(r * jnp.exp(gc))
    bt = stack_all(b * e_inv)
    kt = stack_all(k2 * e_inv)
    vst = stack_all(v)
    bh = stack_all(b * e_end)
    kh = stack_all(k2 * e_end)

    n2 = 2 * c
    quad = _mm(jnp.concatenate([at, rt], axis=1), jnp.concatenate([bt, kt], axis=1), form="nt")
    row = lax.broadcasted_iota(jnp.int32, (1, n2, n2), 1) % c
    col = lax.broadcasted_iota(jnp.int32, (1, n2, n2), 2) % c
    strict = row > col
    incl = row >= col
    l_ab = jnp.where(strict, quad[:, :n2, :n2], 0.0)
    m_ak = jnp.where(strict, quad[:, :n2, n2:], 0.0).astype(BF16)
    a_rb = jnp.where(incl, quad[:, n2:, :n2], 0.0).astype(BF16)
    a_rk = jnp.where(incl, quad[:, n2:, n2:], 0.0).astype(BF16)
    t_inv = _unit_lower_inverse(l_ab, c // SOLVE_BLOCK, p_inv).astype(BF16)
    w_st = _mm(t_inv, at).astype(BF16)
    u0 = _mm(t_inv, _mm(m_ak, vst).astype(BF16))
    y0 = _mm(a_rk, vst)
    kv = _mm(vst, kh, form="tn")

    s = s_ref[...]
    for j in range(ns):
        gs = slice(j * n_pairs, (j + 1) * n_pairs)
        s_b = s.astype(BF16)
        u = u0[gs] + _mm(w_st[gs], s_b, form="nt")
        u_b = u.astype(BF16)
        y_st = _mm(rt[gs], s_b, form="nt") + _mm(a_rb[gs], u_b) + y0[gs]
        d_end = jnp.exp(jnp.stack([g_end[j][:, pair(pi)] for pi in range(n_pairs)]))
        s = s * d_end + _mm(u_b, bh[gs], form="tn") + kv[gs]
        y = y_st[:, :c] + y_st[:, c:]
        for pi in range(n_pairs):
            ybuf_ref[sub(j), pair(pi)] = y[pi]
    s_ref[...] = s

    y = ybuf_ref[...]
    inv_n = 1.0 / HEAD_DIM_RWKV
    mean = gsum(y) * inv_n
    dlt = y - mean
    var = gsum(dlt * dlt) * inv_n
    yn = dlt * lax.rsqrt(var + RWKV_GN_EPS) * lnw_ref[...] + lnb_ref[...]
    bonus = gsum(r * k2 * rk_ref[...]) * v
    y_ref[0] = ((yn + bonus) * g).astype(y_ref.dtype)


def _rwkv_mix(p3d, v_first, mu_pad, w0, wbw, a0, wba, wbg, v0, wbv, k_k, k_a, r_k, ln_w, ln_b, rows=256, chunk=64):
    bsz, t, _ = p3d.shape
    rows = min(rows, t)
    has_vres = v_first is not None
    n_lora = PREC["lora"][1]
    row = lambda z: z.reshape(1, -1).astype(F32)
    lane = jnp.arange(LANES)
    gmat = ((lane[:, None] // HEAD_DIM_RWKV) == (lane[None, :] // HEAD_DIM_RWKV)).astype(BF16)
    zeros = jnp.zeros((LORA_DECAY, W_RWKV), F32)
    wbw_pad = _split_bf16(jnp.concatenate([wbw, zeros], axis=0), n_lora)
    wba_pad = _split_bf16(jnp.concatenate([zeros, wba], axis=0), n_lora)
    const = lambda shape: pl.BlockSpec(shape, lambda b, c: (0,) * len(shape))
    lora_spec = const((n_lora, LANES, W_RWKV))
    p_spec = pl.BlockSpec((1, rows, RW_BLOCK), lambda b, c: (b, c, 0))
    seq_spec = pl.BlockSpec((1, rows, W_RWKV), lambda b, c: (b, c, 0))
    args = [p3d]
    specs = [p_spec]
    if has_vres:
        args.append(v_first)
        specs.append(seq_spec)
    args += [mu_pad, row(w0), wbw_pad, row(a0), wba_pad, _split_bf16(wbg, n_lora)]
    specs += [const((1, RW_BLOCK)), const((1, W_RWKV)), lora_spec, const((1, W_RWKV)), lora_spec, lora_spec]
    if has_vres:
        wbv_pad = jnp.concatenate([wbv, jnp.zeros((LANES - LORA_VRES, W_RWKV), F32)], axis=0)
        args += [row(v0), _split_bf16(wbv_pad, n_lora)]
        specs += [const((1, W_RWKV)), lora_spec]
    args += [row(k_k), row(k_a), row(r_k), row(ln_w), row(ln_b), gmat, _block_tri(rows, chunk)]
    specs += [const((1, W_RWKV))] * 5 + [const((LANES, LANES)), const((rows, rows))]
    y_shape = jax.ShapeDtypeStruct((bsz, t, W_RWKV), BF16)
    if has_vres:
        out_shape, out_specs = y_shape, seq_spec
    else:
        out_shape = (y_shape, jax.ShapeDtypeStruct((bsz, t, W_RWKV), F32))
        out_specs = (seq_spec, seq_spec)
    return pl.pallas_call(
        functools.partial(_rwkv_kernel, rows=rows, chunk=chunk, has_vres=has_vres),
        grid=(bsz, t // rows),
        in_specs=specs,
        out_specs=out_specs,
        out_shape=out_shape,
        scratch_shapes=[pltpu.VMEM((rows + 8, RW_BLOCK), F32),
                        pltpu.VMEM((W_RWKV // LANES, LANES, LANES), F32),
                        pltpu.VMEM((rows, W_RWKV), F32)],
        compiler_params=pltpu.CompilerParams(
            dimension_semantics=("parallel", "arbitrary"), vmem_limit_bytes=VMEM_LIMIT),
    )(*args)


def _gdn_kernel(x_ref, ab_ref, convw_ref, hp_ref, normw_ref, tri_ref, eye_ref, ones_ref,
                y_ref, buf_ref, s_ref, obuf_ref, *, rows, chunk):
    c = chunk
    ns = rows // c
    nh = N_HEADS_GDN
    wq = 3 * W_GDN
    p_ones, p_inv = PREC["ones"], PREC["inv"]
    head = lambda h: slice(h * LANES, (h + 1) * LANES)
    sub = lambda j: slice(j * c, (j + 1) * c)

    @pl.when(pl.program_id(1) == 0)
    def _():
        buf_ref[0:8, :] = jnp.zeros((8, wq), F32)
        s_ref[...] = jnp.zeros_like(s_ref)

    x = x_ref[0]
    xc = x[:, :wq]
    buf_ref[8:8 + rows, :] = xc
    conv = xc * convw_ref[3:4, :]
    for i in range(CONV_WIDTH - 1):
        conv = conv + buf_ref[5 + i:5 + i + rows, :] * convw_ref[i:i + 1, :]
    buf_ref[0:8, :] = xc[rows - 8:rows, :]
    qkv = conv * _sigmoid(conv)

    ab = ab_ref[0]
    g_all = -jnp.exp(hp_ref[0:1, :]) * _softplus(ab + hp_ref[1:2, :])
    beta_all = _sigmoid(ab)
    gc_all = _mm(tri_ref[...], g_all, p_ones)
    gc_t = _mm(eye_ref[...], gc_all, p_ones, "nt")
    ones = ones_ref[...]

    def l2n(z):
        return z * lax.rsqrt(_mm(z * z, ones, (2, 1)) + NORM_EPS)

    chains = [(j, h) for j in range(ns) for h in range(nh)]
    k_l, kb_l, q_l, vk_l, qe_l, ke_l, gcol_l, grow_l, gend_l = [], [], [], [], [], [], [], [], []
    for h in range(nh):
        q = l2n(qkv[:, head(h)]) * (HEAD_DIM_GDN ** -0.5)
        k = l2n(qkv[:, W_GDN + h * LANES:W_GDN + (h + 1) * LANES])
        v = qkv[:, 2 * W_GDN + h * LANES:2 * W_GDN + (h + 1) * LANES]
        gcol = gc_all[:, h:h + 1]
        beta = beta_all[:, nh + h:nh + h + 1]
        g_end = [gcol[j * c + c - 1:j * c + c, :] for j in range(ns)]
        e_gc = jnp.exp(gcol)
        e_end = jnp.exp(jnp.concatenate([g_end[j] - gcol[sub(j), :] for j in range(ns)], axis=0))
        kb = k * beta
        k_l.append(k.astype(BF16))
        kb_l.append(kb.astype(BF16))
        q_l.append(q.astype(BF16))
        vk_l.append(jnp.concatenate([v * beta, kb * e_gc], axis=1).astype(BF16))
        qe_l.append((q * e_gc).astype(BF16))
        ke_l.append((k * e_end).astype(BF16))
        gcol_l.append(gcol)
        grow_l.append(gc_t[h:h + 1, :])
        gend_l.append(g_end)
    gather = lambda lst: jnp.stack([lst[h][sub(j), :] for j, h in chains])
    k_b, kb_b, q_b, vk_b, qe_b, ke_b = (gather(l) for l in (k_l, kb_l, q_l, vk_l, qe_l, ke_l))
    gcol_b = gather(gcol_l)
    grow_b = jnp.stack([grow_l[h][:, sub(j)] for j, h in chains])

    row = lax.broadcasted_iota(jnp.int32, (1, c, c), 1)
    col = lax.broadcasted_iota(jnp.int32, (1, c, c), 2)
    decay = jnp.exp(jnp.where(row >= col, gcol_b - grow_b, -jnp.inf))
    both = _mm(jnp.concatenate([kb_b, q_b], axis=1), k_b, form="nt")
    kkt, qkt = both[:, :c], both[:, c:]
    t_inv = _unit_lower_inverse(-jnp.where(row > col, kkt * decay, 0.0), c // SOLVE_BLOCK, p_inv).astype(BF16)
    uw = _mm(t_inv, vk_b)
    u = uw[:, :, :LANES]
    wq_b = jnp.concatenate([uw[:, :, LANES:].astype(BF16), qe_b], axis=1)
    attn = (qkt * decay).astype(BF16)

    s = s_ref[...]
    for j in range(ns):
        gs = slice(j * nh, (j + 1) * nh)
        ws_qs = _mm(wq_b[gs], s.astype(BF16))
        v_new = (u[gs] - ws_qs[:, :c]).astype(BF16)
        o = ws_qs[:, c:] + _mm(attn[gs], v_new)
        d_end = jnp.exp(jnp.stack([gend_l[h][j] for h in range(nh)]))
        s = s * d_end + _mm(ke_b[gs], v_new, form="tn")
        for h in range(nh):
            obuf_ref[sub(j), head(h)] = o[h]
    s_ref[...] = s

    o = obuf_ref[...]
    ms = jnp.concatenate([_mm(o[:, head(h)] * o[:, head(h)], ones, (2, 1)) for h in range(nh)], axis=1)
    on = o * lax.rsqrt(ms * (1.0 / HEAD_DIM_GDN) + NORM_EPS) * normw_ref[...]
    z = x[:, wq:]
    y_ref[0] = (on * (z * _sigmoid(z))).astype(y_ref.dtype)


def _gdn_mix(p3d, conv_w, a_log, dt_bias, norm_w, rows=256, chunk=128):
    bsz, t, _ = p3d.shape
    rows = min(rows, t)
    pad = jnp.zeros((LANES - N_HEADS_GDN,), F32)
    hp = jnp.stack([jnp.concatenate([a_log.astype(F32), pad]), jnp.concatenate([dt_bias.astype(F32), pad])])
    eye = jnp.eye(LANES, dtype=BF16)
    ones = jnp.ones((LANES, LANES), BF16)
    normw = jnp.tile(norm_w.astype(F32), N_HEADS_GDN).reshape(1, W_GDN)
    const = lambda shape: pl.BlockSpec(shape, lambda b, c: (0,) * len(shape))
    return pl.pallas_call(
        functools.partial(_gdn_kernel, rows=rows, chunk=chunk),
        grid=(bsz, t // rows),
        in_specs=[
            pl.BlockSpec((1, rows, 4 * W_GDN), lambda b, c: (b, c, COL_GDN // (4 * W_GDN))),
            pl.BlockSpec((1, rows, LANES), lambda b, c: (b, c, COL_GDN_AB // LANES)),
            const((CONV_WIDTH, 3 * W_GDN)), const((2, LANES)), const((1, W_GDN)),
            const((rows, rows)), const((LANES, LANES)), const((LANES, LANES)),
        ],
        out_specs=pl.BlockSpec((1, rows, W_GDN), lambda b, c: (b, c, 0)),
        out_shape=jax.ShapeDtypeStruct((bsz, t, W_GDN), BF16),
        scratch_shapes=[pltpu.VMEM((rows + 8, 3 * W_GDN), F32),
                        pltpu.VMEM((N_HEADS_GDN, HEAD_DIM_GDN, HEAD_DIM_GDN), F32),
                        pltpu.VMEM((rows, W_GDN), F32)],
        compiler_params=pltpu.CompilerParams(
            dimension_semantics=("parallel", "arbitrary"), vmem_limit_bytes=VMEM_LIMIT),
    )(p3d, p3d, conv_w.astype(F32), hp, normw, _block_tri(rows, chunk), eye, ones)


def _diff_prep_kernel(q_ref, k_ref, v_ref, qw_ref, kw_ref, gmat_ref, eye_ref, qt_ref, ko_ref, vt_ref, *, tq, tk):
    gmat = gmat_ref[...]
    eye = eye_ref[...]
    inv_d = 1.0 / HEAD_DIM_DIFF
    tm = q_ref.shape[1]

    def normed(src, wref, sl):
        x = src[0, :, sl]
        ms = _mm(x * x, gmat, PREC["gsum"]) * inv_d
        return (x * lax.rsqrt(ms + NORM_EPS) * wref[...]).astype(BF16)

    for hb in range(N_HEADS_DIFF):
        sl = slice(hb * LANES, (hb + 1) * LANES)
        ko_ref[0, :, sl] = normed(k_ref, kw_ref, sl)
        q_t = _mm(eye, normed(q_ref, qw_ref, sl), form="nt").astype(BF16)
        v_t = _mm(eye, v_ref[0, :, sl].astype(BF16), form="nt").astype(BF16)
        for i in range(tm // tq):
            qt_ref[0, hb, i] = q_t[:, i * tq:(i + 1) * tq]
        for i in range(tm // tk):
            vt_ref[0, hb, i] = v_t[:, i * tk:(i + 1) * tk]


def _diff_prep(p3d, q_norm_w, k_norm_w, tq, tk):
    bsz, t, _ = p3d.shape
    tm = tk
    lane = jnp.arange(LANES)
    gmat = ((lane[:, None] // HEAD_DIM_DIFF) == (lane[None, :] // HEAD_DIM_DIFF)).astype(BF16)
    eye = jnp.eye(LANES, dtype=BF16)
    qw = (jnp.tile(q_norm_w.astype(F32), 2) * (HEAD_DIM_DIFF ** -0.5)).reshape(1, LANES)
    kw = jnp.tile(k_norm_w.astype(F32), 2).reshape(1, LANES)
    base = COL_DIFF // W_DIFF
    sec = lambda j: pl.BlockSpec((1, tm, W_DIFF), lambda b, i: (b, i, base + j))
    const = lambda shape: pl.BlockSpec(shape, lambda b, i: (0,) * len(shape))
    nh = N_HEADS_DIFF
    return pl.pallas_call(
        functools.partial(_diff_prep_kernel, tq=tq, tk=tk),
        grid=(bsz, t // tm),
        in_specs=[sec(0), sec(1), sec(2), const((1, LANES)), const((1, LANES)),
                  const((LANES, LANES)), const((LANES, LANES))],
        out_specs=(pl.BlockSpec((1, nh, tm // tq, LANES, tq), lambda b, i: (b, 0, i, 0, 0)),
                   pl.BlockSpec((1, tm, W_DIFF), lambda b, i: (b, i, 0)),
                   pl.BlockSpec((1, nh, tm // tk, LANES, tk), lambda b, i: (b, 0, i, 0, 0))),
        out_shape=(jax.ShapeDtypeStruct((bsz, nh, t // tq, LANES, tq), BF16),
                   jax.ShapeDtypeStruct((bsz, t, W_DIFF), BF16),
                   jax.ShapeDtypeStruct((bsz, nh, t // tk, LANES, tk), BF16)),
        compiler_params=pltpu.CompilerParams(
            dimension_semantics=("parallel", "parallel"), vmem_limit_bytes=VMEM_LIMIT),
    )(p3d, p3d, p3d, qw, kw, gmat, eye)


def _diff_attn_kernel(qt_ref, k_ref, vt_ref, lam_ref, subw_ref, eye_ref, o_ref, *, tq, tk, lambda_init):
    t = k_ref.shape[1]
    feat = lax.broadcasted_iota(jnp.int32, (LANES, 1), 0)
    lv = lam_ref[...]
    lam = (jnp.exp(jnp.sum(lv[0:1] * lv[1:2], axis=-1, keepdims=True))
           - jnp.exp(jnp.sum(lv[2:3] * lv[3:4], axis=-1, keepdims=True)) + lambda_init)
    eye = eye_ref[...]

    tiles = []
    for i in range(t // tq):
        q_end = (i + 1) * tq
        for k0 in range(0, q_end, tk):
            klen = min(tk, q_end - k0)
            tiles.append((i, k0, klen, k0 + klen > i * tq))

    q_maps = {}

    def scores(i, k0, klen):
        if i not in q_maps:
            q_t = qt_ref[0, 0, i]
            zero = jnp.zeros_like(q_t)
            q_maps.clear()
            q_maps[i] = (jnp.where(feat < HEAD_DIM_DIFF, q_t, zero), jnp.where(feat >= HEAD_DIM_DIFF, q_t, zero))
        k = k_ref[0, k0:k0 + klen, :]
        return [_dot(k, qm) for qm in q_maps[i]]

    state = None
    pending = scores(*tiles[0][:3])
    for n, (i, k0, klen, diag) in enumerate(tiles):
        cur = pending
        if n + 1 < len(tiles):
            pending = scores(*tiles[n + 1][:3])
        v_t = vt_ref[0, 0, k0 // tk][:, (k0 % tk):(k0 % tk) + klen]
        new_state = []
        for mp, s in enumerate(cur):
            if diag:
                key = lax.broadcasted_iota(jnp.int32, (klen, tq), 0)
                qry = lax.broadcasted_iota(jnp.int32, (klen, tq), 1)
                s = jnp.where(key - qry <= i * tq - k0, s, -jnp.inf)
            if k0 == 0:
                m_new = jnp.max(s, axis=0, keepdims=True)
                p = jnp.exp(s - m_new)
                l_new = jnp.sum(p, axis=0, keepdims=True)
                acc_new = _dot(v_t, p.astype(BF16))
            else:
                m_prev, l_prev, acc_prev = state[mp]
                m_new = jnp.maximum(m_prev, jnp.max(s, axis=0, keepdims=True))
                alpha = jnp.exp(m_prev - m_new)
                p = jnp.exp(s - m_new)
                l_new = alpha * l_prev + jnp.sum(p, axis=0, keepdims=True)
                acc_new = alpha * acc_prev + _dot(v_t, p.astype(BF16))
            new_state.append((m_new, l_new, acc_new))
        state = new_state
        if k0 + klen == (i + 1) * tq:
            (_, l0, acc0), (_, l1, acc1) = state
            o_t = acc0 * (1.0 / l0) - lam * (acc1 * (1.0 / l1))
            ms = jnp.sum(o_t * o_t, axis=0, keepdims=True) * (1.0 / LANES)
            on_t = (o_t * lax.rsqrt(ms + NORM_EPS) * subw_ref[...] * (1.0 - lambda_init)).astype(BF16)
            o_ref[0, i * tq:(i + 1) * tq, :] = _mm(on_t, eye, form="tn").astype(o_ref.dtype)


def _diff_attn(p3d, q_norm_w, k_norm_w, lam_vecs, subln_w, lambda_init, tq=256, tk=512):
    bsz, t, _ = p3d.shape
    tk = min(tk, t)
    tq = min(tq, tk)
    q_t, kn, v_t = _diff_prep(p3d, q_norm_w, k_norm_w, tq, tk)
    const = lambda shape: pl.BlockSpec(shape, lambda b, h: (0,) * len(shape))
    return pl.pallas_call(
        functools.partial(_diff_attn_kernel, tq=tq, tk=tk, lambda_init=lambda_init),
        grid=(bsz, N_HEADS_DIFF),
        in_specs=[pl.BlockSpec((1, 1, t // tq, LANES, tq), lambda b, h: (b, h, 0, 0, 0)),
                  pl.BlockSpec((1, t, LANES), lambda b, h: (b, 0, h)),
                  pl.BlockSpec((1, 1, t // tk, LANES, tk), lambda b, h: (b, h, 0, 0, 0)),
                  const((4, HEAD_DIM_DIFF)), const((LANES, 1)), const((LANES, LANES))],
        out_specs=pl.BlockSpec((1, t, LANES), lambda b, h: (b, 0, h)),
        out_shape=jax.ShapeDtypeStruct((bsz, t, W_DIFF), BF16),
        compiler_params=pltpu.CompilerParams(
            dimension_semantics=("parallel", "parallel"), vmem_limit_bytes=VMEM_LIMIT),
    )(q_t, kn, v_t, lam_vecs, subln_w.reshape(LANES, 1).astype(F32), jnp.eye(LANES, dtype=BF16))


def _out_proj_kernel(x_ref, yr_ref, yg_ref, yd_ref, w1_ref, w2_ref, w3_ref, o_ref):
    o_ref[...] = (x_ref[...] + _dot(yr_ref[...], w1_ref[...]) + _dot(yg_ref[...], w2_ref[...])
                  + _dot(yd_ref[...], w3_ref[...]))


def _out_proj(x2d, y_rw, y_gdn, y_diff, w_out_bf16, tm=512):
    n, d = x2d.shape
    rowblk = lambda w: pl.BlockSpec((tm, w), lambda i: (i, 0))
    return pl.pallas_call(
        _out_proj_kernel,
        grid=(n // tm,),
        in_specs=[rowblk(d), rowblk(W_RWKV), rowblk(W_GDN), rowblk(W_DIFF),
                  pl.BlockSpec((W_RWKV, d), lambda i: (0, 0)),
                  pl.BlockSpec((W_GDN, d), lambda i: (1, 0)),
                  pl.BlockSpec((W_DIFF, d), lambda i: (1, 0))],
        out_specs=rowblk(d),
        out_shape=jax.ShapeDtypeStruct((n, d), F32),
        compiler_params=pltpu.CompilerParams(
            dimension_semantics=("parallel",), vmem_limit_bytes=VMEM_LIMIT),
    )(x2d, y_rw, y_gdn, y_diff, w_out_bf16, w_out_bf16, w_out_bf16)


def _ffn_kernel(x_ref, nw_ref, wg_ref, wu_ref, wo_ref, o_ref, h_ref, acc_ref):
    f = pl.program_id(1)

    @pl.when(f == 0)
    def _():
        x = x_ref[...]
        ms = jnp.mean(x * x, axis=-1, keepdims=True)
        h_ref[...] = (x * lax.rsqrt(ms + NORM_EPS) * nw_ref[...]).astype(BF16)
        acc_ref[...] = x

    h = h_ref[...]
    gate = _dot(h, wg_ref[...])
    up = _dot(h, wu_ref[...])
    act = (gate * _sigmoid(gate) * up).astype(BF16)
    acc_ref[...] += _dot(act, wo_ref[...])

    @pl.when(f == pl.num_programs(1) - 1)
    def _():
        o_ref[...] = acc_ref[...]


def _ffn(x2d, nw, w_in_bf16, w_out_bf16, tm=512, tf=512):
    n, d = x2d.shape
    hidden = w_out_bf16.shape[0]
    nf = hidden // tf
    return pl.pallas_call(
        _ffn_kernel,
        grid=(n // tm, nf),
        in_specs=[
            pl.BlockSpec((tm, d), lambda i, f: (i, 0)),
            pl.BlockSpec((1, d), lambda i, f: (0, 0)),
            pl.BlockSpec((d, tf), lambda i, f: (0, f)),
            pl.BlockSpec((d, tf), lambda i, f: (0, nf + f)),
            pl.BlockSpec((tf, d), lambda i, f: (f, 0)),
        ],
        out_specs=pl.BlockSpec((tm, d), lambda i, f: (i, 0)),
        out_shape=jax.ShapeDtypeStruct((n, d), F32),
        scratch_shapes=[pltpu.VMEM((tm, d), BF16), pltpu.VMEM((tm, d), F32)],
        compiler_params=pltpu.CompilerParams(
            dimension_semantics=("parallel", "arbitrary"), vmem_limit_bytes=VMEM_LIMIT),
    )(x2d, nw.reshape(1, d), w_in_bf16, w_in_bf16, w_out_bf16)


def _pack_in_proj(w_in_l, w_vres_l):
    d = w_in_l.shape[0]
    z = lambda n: jnp.zeros((d, n), w_in_l.dtype)
    gdn0 = RWKV_COLS
    vres = z(LANES) if w_vres_l is None else jnp.concatenate([w_vres_l, z(LANES - LORA_VRES)], axis=1)
    cols = [w_in_l[:, :RWKV_COLS], vres,
            w_in_l[:, gdn0 + 4 * W_GDN:gdn0 + GDN_COLS], z(LANES - 2 * N_HEADS_GDN),
            w_in_l[:, gdn0:gdn0 + 4 * W_GDN],
            w_in_l[:, gdn0 + GDN_COLS:]]
    return jnp.concatenate(cols, axis=1).astype(BF16)


def kernel(x, attn_norm_w, w_in, w_vres_a, mu_rwkv, mu_vres, rwkv_w0, rwkv_w_lora_b, rwkv_a0, rwkv_a_lora_b, rwkv_g_lora_b, rwkv_v0, rwkv_v_lora_b, rwkv_k_k, rwkv_k_a, rwkv_r_k, rwkv_ln_w, rwkv_ln_b, gdn_conv_w, gdn_A_log, gdn_dt_bias, gdn_norm_w, diff_q_norm_w, diff_k_norm_w, diff_lambda_q1, diff_lambda_k1, diff_lambda_q2, diff_lambda_k2, diff_subln_w, w_out, ffn_norm_w, w_ffn_in, w_ffn_out):
    bsz, t, d = x.shape
    depth = w_in.shape[0]
    x2d = x.reshape(bsz * t, d)
    v_first = None
    for l in range(depth):
        w_proj = _pack_in_proj(w_in[l], None if l == 0 else w_vres_a[l - 1])
        mu_tail = jnp.zeros((RW_BLOCK - RWKV_COLS,), F32)
        if l > 0:
            mu_tail = mu_tail.at[:LORA_VRES].set(mu_vres[l - 1])
        mu_pad = jnp.concatenate([mu_rwkv[l], mu_tail]).reshape(1, RW_BLOCK)
        p3d = _norm_matmul(x2d, attn_norm_w[l], w_proj).reshape(bsz, t, N_PAD)
        rw_args = (rwkv_w0[l], rwkv_w_lora_b[l], rwkv_a0[l], rwkv_a_lora_b[l], rwkv_g_lora_b[l])
        rw_tail = (rwkv_k_k[l], rwkv_k_a[l], rwkv_r_k[l], rwkv_ln_w[l], rwkv_ln_b[l])
        if l == 0:
            y_rw, v_first = _rwkv_mix(p3d, None, mu_pad, *rw_args, None, None, *rw_tail)
        else:
            y_rw = _rwkv_mix(p3d, v_first, mu_pad, *rw_args, rwkv_v0[l - 1], rwkv_v_lora_b[l - 1], *rw_tail)
        y_gdn = _gdn_mix(p3d, gdn_conv_w[l], gdn_A_log[l], gdn_dt_bias[l], gdn_norm_w[l])
        lambda_init = 0.8 - 0.6 * math.exp(-0.3 * l)
        lam_vecs = jnp.stack([diff_lambda_q1[l], diff_lambda_k1[l], diff_lambda_q2[l], diff_lambda_k2[l]]).astype(F32)
        y_diff = _diff_attn(p3d, diff_q_norm_w[l], diff_k_norm_w[l], lam_vecs, diff_subln_w[l], lambda_init)
        n = bsz * t
        x2d = _out_proj(x2d, y_rw.reshape(n, W_RWKV), y_gdn.reshape(n, W_GDN), y_diff.reshape(n, W_DIFF),
                        w_out[l].astype(BF16))
        x2d = _ffn(x2d, ffn_norm_w[l], w_ffn_in[l].astype(BF16), w_ffn_out[l].astype(BF16))
    return x2d.reshape(bsz, t, d)
```

```python
import functools
import math

import jax
import jax.numpy as jnp
from jax import lax
from jax.experimental import pallas as pl
from jax.experimental.pallas import tpu as pltpu

F32 = jnp.float32
BF16 = jnp.bfloat16

D_MODEL = 2048
W_RWKV = D_MODEL // 4
HEAD_DIM_RWKV = 64
LORA_DECAY = 64
LORA_ICLR = 64
LORA_VRES = 32
LORA_GATE = 128
RWKV_GN_EPS = 64e-5
W_GDN = D_MODEL // 4
HEAD_DIM_GDN = 128
N_HEADS_GDN = W_GDN // HEAD_DIM_GDN
CONV_WIDTH = 4
W_DIFF = D_MODEL // 2
HEAD_DIM_DIFF = 64
N_HEADS_DIFF = W_DIFF // (2 * HEAD_DIM_DIFF)
RWKV_COLS = 3 * W_RWKV + LORA_DECAY + LORA_ICLR + LORA_GATE
GDN_COLS = 4 * W_GDN + 2 * N_HEADS_GDN
DIFF_COLS = 3 * W_DIFF
N_IN = RWKV_COLS + GDN_COLS + DIFF_COLS
FFN_HIDDEN = -(-8 * D_MODEL // (3 * 256)) * 256
NORM_EPS = 1e-6

LANES = 128
SOLVE_BLOCK = 16

COL_VRES = RWKV_COLS
COL_GDN_AB = COL_VRES + LANES
COL_GDN = COL_GDN_AB + LANES
COL_DIFF = COL_GDN + 4 * W_GDN
N_PAD = COL_DIFF + DIFF_COLS
RW_BLOCK = COL_GDN

VMEM_LIMIT = 56 * 1024 * 1024

PREC = {
    "lora": (1, 1),
    "ones": (1, 3),
    "gsum": (1, 1),
    "inv": (1, 1),
}


def _pieces(x, n):
    if isinstance(x, (list, tuple)):
        return list(x)[:n]
    if x.dtype == BF16:
        return [x]
    out, rem = [], x
    for i in range(n):
        piece = rem.astype(BF16)
        out.append(piece)
        if i + 1 < n:
            rem = rem - piece.astype(F32)
    return out


_DIMS = {"nn": (((1,), (0,)), ((), ())), "nt": (((1,), (1,)), ((), ())), "tn": (((0,), (0,)), ((), ()))}
_DIMS_B = {"nn": (((2,), (1,)), ((0,), (0,))), "nt": (((2,), (2,)), ((0,), (0,))), "tn": (((1,), (1,)), ((0,), (0,)))}


def _mm(a, b, prec=(1, 1), form="nn"):
    pa, pb = _pieces(a, prec[0]), _pieces(b, prec[1])
    dims = (_DIMS_B if pa[0].ndim == 3 else _DIMS)[form]
    depth = max(len(pa), len(pb))
    terms = sorted(((i, j) for i in range(len(pa)) for j in range(len(pb)) if i + j < depth),
                   key=lambda ij: -(ij[0] + ij[1]))
    acc = None
    for i, j in terms:
        d = lax.dot_general(pa[i], pb[j], dims, preferred_element_type=F32)
        acc = d if acc is None else acc + d
    return acc


def _dot(a, b):
    return jnp.dot(a, b, preferred_element_type=F32)


def _sigmoid(x):
    return 1.0 / (1.0 + jnp.exp(-x))


def _softplus(x):
    return jnp.maximum(x, 0.0) + jnp.log(1.0 + jnp.exp(-jnp.abs(x)))


def _split_bf16(w, n):
    out, rem = [], w.astype(F32)
    for _ in range(n):
        piece = rem.astype(BF16)
        out.append(piece)
        rem = rem - piece.astype(F32)
    return jnp.stack(out)


def _block_tri(rows, chunk):
    idx = jnp.arange(rows)
    same = (idx[:, None] // chunk) == (idx[None, :] // chunk)
    return (same & (idx[:, None] >= idx[None, :])).astype(BF16)


def _unit_lower_inverse(lo_tri, n_sub, prec):
    n = lo_tri.shape[-1]
    lead = (1,) * (lo_tri.ndim - 2)
    row = lax.broadcasted_iota(jnp.int32, lead + (n, n), lo_tri.ndim - 2)
    col = lax.broadcasted_iota(jnp.int32, lead + (n, n), lo_tri.ndim - 1)
    eye = (row == col).astype(F32)
    same = (row // SOLVE_BLOCK) == (col // SOLVE_BLOCK)
    l_diag = jnp.where(same, lo_tri, 0.0)
    l_off = jnp.where(same, 0.0, lo_tri)

    def neumann(a, s, order):
        width = 1
        while width < order:
            if 2 * width < order:
                both = _mm(a, jnp.concatenate([s, a], axis=-1), prec)
                s = s + both[..., :n]
                a = both[..., n:]
            else:
                s = s + _mm(a, s, prec)
            width *= 2
        return s

    t_diag = neumann(_mm(l_diag, l_diag, prec), eye + l_diag, SOLVE_BLOCK // 2)
    return neumann(_mm(t_diag, l_off, prec), t_diag, n_sub)


def _norm_matmul_kernel(x_ref, nw_ref, w_ref, o_ref, h_ref):
    @pl.when(pl.program_id(1) == 0)
    def _():
        x = x_ref[...]
        ms = jnp.mean(x * x, axis=-1, keepdims=True)
        h_ref[...] = (x * lax.rsqrt(ms + NORM_EPS) * nw_ref[...]).astype(BF16)

    o_ref[...] = _dot(h_ref[...], w_ref[...])


def _norm_matmul(x2d, nw, w_bf16, tm=1024, tn=512):
    n, d = x2d.shape
    tm = min(tm, n)
    npad = w_bf16.shape[1]
    return pl.pallas_call(
        _norm_matmul_kernel,
        grid=(n // tm, npad // tn),
        in_specs=[
            pl.BlockSpec((tm, d), lambda i, j: (i, 0)),
            pl.BlockSpec((1, d), lambda i, j: (0, 0)),
            pl.BlockSpec((d, tn), lambda i, j: (0, j)),
        ],
        out_specs=pl.BlockSpec((tm, tn), lambda i, j: (i, j)),
        out_shape=jax.ShapeDtypeStruct((n, npad), F32),
        scratch_shapes=[pltpu.VMEM((tm, d), BF16)],
        compiler_params=pltpu.CompilerParams(
            dimension_semantics=("parallel", "arbitrary"), vmem_limit_bytes=VMEM_LIMIT),
    )(x2d, nw.reshape(1, d), w_bf16)


def _rwkv_kernel(*refs, rows, chunk, has_vres):
    if has_vres:
        (p_ref, vf_ref, mu_ref, w0_ref, wbw_ref, a0_ref, wba_ref, wbg_ref, v0_ref, wbv_ref,
         kk_ref, ka_ref, rk_ref, lnw_ref, lnb_ref, gmat_ref, tri_ref,
         y_ref, buf_ref, s_ref, ybuf_ref) = refs
    else:
        (p_ref, mu_ref, w0_ref, wbw_ref, a0_ref, wba_ref, wbg_ref,
         kk_ref, ka_ref, rk_ref, lnw_ref, lnb_ref, gmat_ref, tri_ref,
         y_ref, vout_ref, buf_ref, s_ref, ybuf_ref) = refs
    c = chunk
    ns = rows // c
    n_pairs = W_RWKV // LANES
    p_lora, p_ones, p_gsum, p_inv = PREC["lora"], PREC["ones"], PREC["gsum"], PREC["inv"]
    wpieces = lambda ref: [ref[i] for i in range(ref.shape[0])]
    pair = lambda pi: slice(pi * LANES, (pi + 1) * LANES)
    sub = lambda j: slice(j * c, (j + 1) * c)

    @pl.when(pl.program_id(1) == 0)
    def _():
        buf_ref[0:8, :] = jnp.zeros((8, RW_BLOCK), F32)
        s_ref[...] = jnp.zeros_like(s_ref)

    x = p_ref[0]
    buf_ref[8:8 + rows, :] = x
    prev = buf_ref[7:7 + rows, :]
    buf_ref[0:8, :] = x[rows - 8:rows, :]
    pm = x + (prev - x) * mu_ref[...]

    r = pm[:, 0:W_RWKV]
    k = pm[:, W_RWKV:2 * W_RWKV]
    v = pm[:, 2 * W_RWKV:3 * W_RWKV]
    lw = pm[:, 3 * W_RWKV:3 * W_RWKV + LANES]
    xg = pm[:, 3 * W_RWKV + LANES:3 * W_RWKV + 2 * LANES]
    w_log = -_softplus(-(w0_ref[...] + _mm(jnp.tanh(lw), wpieces(wbw_ref), p_lora))) - 0.5
    logd = -jnp.exp(w_log)
    a = _sigmoid(a0_ref[...] + _mm(lw, wpieces(wba_ref), p_lora))
    g = _mm(_sigmoid(xg), wpieces(wbg_ref), p_lora)
    if has_vres:
        xv = pm[:, COL_VRES:COL_VRES + LANES]
        v = v + (vf_ref[0] - v) * _sigmoid(v0_ref[...] + _mm(xv, wpieces(wbv_ref), p_lora))
    else:
        vout_ref[0] = v

    gmat = gmat_ref[...]

    def gsum(z):
        return jnp.concatenate([_mm(z[:, pair(pi)], gmat, p_gsum) for pi in range(n_pairs)], axis=1)

    kk = k * kk_ref[...]
    kk = kk * lax.rsqrt(gsum(kk * kk) + NORM_EPS)
    k2 = k * (1.0 + (a - 1.0) * ka_ref[...])
    b = kk * a
    gc = _mm(tri_ref[...], logd, p_ones)
    g_end = [gc[j * c + c - 1:j * c + c, :] for j in range(ns)]
    e_end = jnp.exp(jnp.concatenate([g_end[j] - gc[sub(j), :] for j in range(ns)], axis=0))
    e_inv = jnp.exp(-gc)

    lane = lax.broadcasted_iota(jnp.int32, (1, W_RWKV), 1)
    h0 = ((lane // HEAD_DIM_RWKV) % 2 == 0).astype(F32)
    h1 = 1.0 - h0

    def stack_all(z):
        z0 = (z * h0).astype(BF16)
        z1 = (z * h1).astype(BF16)
        return jnp.stack([jnp.concatenate([z0[sub(j), pair(pi)], z1[sub(j), pair(pi)]], axis=0)
                          for j in range(ns) for pi in range(n_pairs)])

    at = stack_all(-kk * jnp.exp(gc - logd))
    rt = stack_all(r * jnp.exp(gc))
    bt = stack_all(b * e_inv)
    kt = stack_all(k2 * e_inv)
    vst = stack_all(v)
    bh = stack_all(b * e_end)
    kh = stack_all(k2 * e_end)

    n2 = 2 * c
    quad = _mm(jnp.concatenate([at, rt], axis=1), jnp.concatenate([bt, kt], axis=1), form="nt")
    row = lax.broadcasted_iota(jnp.int32, (1, n2, n2), 1) % c
    col = lax.broadcasted_iota(jnp.int32, (1, n2, n2), 2) % c
    strict = row > col
    incl = row >= col
    l_ab = jnp.where(strict, quad[:, :n2, :n2], 0.0)
    a_rb = jnp.where(incl, quad[:, n2:, :n2], 0.0).astype(BF16)
    vmask = jnp.concatenate([jnp.broadcast_to(strict, (1, n2, n2)), jnp.broadcast_to(incl, (1, n2, n2))], axis=1)
    own = _mm(jnp.where(vmask, quad[:, :, n2:], 0.0).astype(BF16), vst)
    y0 = own[:, n2:]
    t_inv = _unit_lower_inverse(l_ab, c // SOLVE_BLOCK, p_inv).astype(BF16)
    t_both = _mm(t_inv, jnp.concatenate([at, own[:, :n2].astype(BF16)], axis=-1))
    u0 = t_both[:, :, LANES:]
    wr = jnp.concatenate([t_both[:, :, :LANES].astype(BF16), rt], axis=1)
    kv = _mm(vst, kh, form="tn")

    s = s_ref[...]
    for j in range(ns):
        gs = slice(j * n_pairs, (j + 1) * n_pairs)
        from_s = _mm(wr[gs], s.astype(BF16), form="nt")
        u = u0[gs] + from_s[:, :n2]
        u_b = u.astype(BF16)
        y_st = from_s[:, n2:] + _mm(a_rb[gs], u_b) + y0[gs]
        d_end = jnp.exp(jnp.stack([g_end[j][:, pair(pi)] for pi in range(n_pairs)]))
        s = s * d_end + _mm(u_b, bh[gs], form="tn") + kv[gs]
        y = y_st[:, :c] + y_st[:, c:]
        for pi in range(n_pairs):
            ybuf_ref[sub(j), pair(pi)] = y[pi]
    s_ref[...] = s

    y = ybuf_ref[...]
    inv_n = 1.0 / HEAD_DIM_RWKV
    mean = gsum(y) * inv_n
    dlt = y - mean
    var = gsum(dlt * dlt) * inv_n
    yn = dlt * lax.rsqrt(var + RWKV_GN_EPS) * lnw_ref[...] + lnb_ref[...]
    bonus = gsum(r * k2 * rk_ref[...]) * v
    y_ref[0] = ((yn + bonus) * g).astype(y_ref.dtype)


def _rwkv_mix(p3d, v_first, mu_pad, w0, wbw, a0, wba, wbg, v0, wbv, k_k, k_a, r_k, ln_w, ln_b, rows=256, chunk=64):
    bsz, t, _ = p3d.shape
    rows = min(rows, t)
    has_vres = v_first is not None
    n_lora = PREC["lora"][1]
    row = lambda z: z.reshape(1, -1).astype(F32)
    lane = jnp.arange(LANES)
    gmat = ((lane[:, None] // HEAD_DIM_RWKV) == (lane[None, :] // HEAD_DIM_RWKV)).astype(BF16)
    zeros = jnp.zeros((LORA_DECAY, W_RWKV), F32)
    wbw_pad = _split_bf16(jnp.concatenate([wbw, zeros], axis=0), n_lora)
    wba_pad = _split_bf16(jnp.concatenate([zeros, wba], axis=0), n_lora)
    const = lambda shape: pl.BlockSpec(shape, lambda b, c: (0,) * len(shape))
    lora_spec = const((n_lora, LANES, W_RWKV))
    p_spec = pl.BlockSpec((1, rows, RW_BLOCK), lambda b, c: (b, c, 0))
    seq_spec = pl.BlockSpec((1, rows, W_RWKV), lambda b, c: (b, c, 0))
    args = [p3d]
    specs = [p_spec]
    if has_vres:
        args.append(v_first)
        specs.append(seq_spec)
    args += [mu_pad, row(w0), wbw_pad, row(a0), wba_pad, _split_bf16(wbg, n_lora)]
    specs += [const((1, RW_BLOCK)), const((1, W_RWKV)), lora_spec, const((1, W_RWKV)), lora_spec, lora_spec]
    if has_vres:
        wbv_pad = jnp.concatenate([wbv, jnp.zeros((LANES - LORA_VRES, W_RWKV), F32)], axis=0)
        args += [row(v0), _split_bf16(wbv_pad, n_lora)]
        specs += [const((1, W_RWKV)), lora_spec]
    args += [row(k_k), row(k_a), row(r_k), row(ln_w), row(ln_b), gmat, _block_tri(rows, chunk)]
    specs += [const((1, W_RWKV))] * 5 + [const((LANES, LANES)), const((rows, rows))]
    y_shape = jax.ShapeDtypeStruct((bsz, t, W_RWKV), BF16)
    if has_vres:
        out_shape, out_specs = y_shape, seq_spec
    else:
        out_shape = (y_shape, jax.ShapeDtypeStruct((bsz, t, W_RWKV), F32))
        out_specs = (seq_spec, seq_spec)
    return pl.pallas_call(
        functools.partial(_rwkv_kernel, rows=rows, chunk=chunk, has_vres=has_vres),
        grid=(bsz, t // rows),
        in_specs=specs,
        out_specs=out_specs,
        out_shape=out_shape,
        scratch_shapes=[pltpu.VMEM((rows + 8, RW_BLOCK), F32),
                        pltpu.VMEM((W_RWKV // LANES, LANES, LANES), F32),
                        pltpu.VMEM((rows, W_RWKV), F32)],
        compiler_params=pltpu.CompilerParams(
            dimension_semantics=("parallel", "arbitrary"), vmem_limit_bytes=VMEM_LIMIT),
    )(*args)


def _gdn_kernel(x_ref, ab_ref, convw_ref, hp_ref, normw_ref, tri_ref, eye_ref, ones_ref,
                y_ref, buf_ref, s_ref, obuf_ref, *, rows, chunk):
    c = chunk
    ns = rows // c
    nh = N_HEADS_GDN
    wq = 3 * W_GDN
    p_ones, p_inv = PREC["ones"], PREC["inv"]
    head = lambda h: slice(h * LANES, (h + 1) * LANES)
    sub = lambda j: slice(j * c, (j + 1) * c)

    @pl.when(pl.program_id(1) == 0)
    def _():
        buf_ref[0:8, :] = jnp.zeros((8, wq), F32)
        s_ref[...] = jnp.zeros_like(s_ref)

    x = x_ref[0]
    xc = x[:, :wq]
    buf_ref[8:8 + rows, :] = xc
    conv = xc * convw_ref[3:4, :]
    for i in range(CONV_WIDTH - 1):
        conv = conv + buf_ref[5 + i:5 + i + rows, :] * convw_ref[i:i + 1, :]
    buf_ref[0:8, :] = xc[rows - 8:rows, :]
    qkv = conv * _sigmoid(conv)

    ab = ab_ref[0]
    g_all = -jnp.exp(hp_ref[0:1, :]) * _softplus(ab + hp_ref[1:2, :])
    beta_all = _sigmoid(ab)
    gc_all = _mm(tri_ref[...], g_all, p_ones)
    gc_t = _mm(eye_ref[...], gc_all, p_ones, "nt")
    ones = ones_ref[...]

    def l2n(z):
        return z * lax.rsqrt(_mm(z * z, ones, (2, 1)) + NORM_EPS)

    chains = [(j, h) for j in range(ns) for h in range(nh)]
    k_l, kb_l, q_l, vk_l, qe_l, ke_l, gcol_l, grow_l, gend_l = [], [], [], [], [], [], [], [], []
    for h in range(nh):
        q = l2n(qkv[:, head(h)]) * (HEAD_DIM_GDN ** -0.5)
        k = l2n(qkv[:, W_GDN + h * LANES:W_GDN + (h + 1) * LANES])
        v = qkv[:, 2 * W_GDN + h * LANES:2 * W_GDN + (h + 1) * LANES]
        gcol = gc_all[:, h:h + 1]
        beta = beta_all[:, nh + h:nh + h + 1]
        g_end = [gcol[j * c + c - 1:j * c + c, :] for j in range(ns)]
        e_gc = jnp.exp(gcol)
        e_end = jnp.exp(jnp.concatenate([g_end[j] - gcol[sub(j), :] for j in range(ns)], axis=0))
        kb = k * beta
        k_l.append(k.astype(BF16))
        kb_l.append(kb.astype(BF16))
        q_l.append(q.astype(BF16))
        vk_l.append(jnp.concatenate([v * beta, kb * e_gc], axis=1).astype(BF16))
        qe_l.append((q * e_gc).astype(BF16))
        ke_l.append((k * e_end).astype(BF16))
        gcol_l.append(gcol)
        grow_l.append(gc_t[h:h + 1, :])
        gend_l.append(g_end)
    gather = lambda lst: jnp.stack([lst[h][sub(j), :] for j, h in chains])
    k_b, kb_b, q_b, vk_b, qe_b, ke_b = (gather(l) for l in (k_l, kb_l, q_l, vk_l, qe_l, ke_l))
    gcol_b = gather(gcol_l)
    grow_b = jnp.stack([grow_l[h][:, sub(j)] for j, h in chains])

    row = lax.broadcasted_iota(jnp.int32, (1, c, c), 1)
    col = lax.broadcasted_iota(jnp.int32, (1, c, c), 2)
    decay = jnp.exp(jnp.where(row >= col, gcol_b - grow_b, -jnp.inf))
    both = _mm(jnp.concatenate([kb_b, q_b], axis=1), k_b, form="nt")
    kkt, qkt = both[:, :c], both[:, c:]
    t_inv = _unit_lower_inverse(-jnp.where(row > col, kkt * decay, 0.0), c // SOLVE_BLOCK, p_inv).astype(BF16)
    uw = _mm(t_inv, vk_b)
    u = uw[:, :, :LANES]
    wq_b = jnp.concatenate([uw[:, :, LANES:].astype(BF16), qe_b], axis=1)
    attn = (qkt * decay).astype(BF16)

    s = s_ref[...]
    for j in range(ns):
        gs = slice(j * nh, (j + 1) * nh)
        ws_qs = _mm(wq_b[gs], s.astype(BF16))
        v_new = (u[gs] - ws_qs[:, :c]).astype(BF16)
        o = ws_qs[:, c:] + _mm(attn[gs], v_new)
        d_end = jnp.exp(jnp.stack([gend_l[h][j] for h in range(nh)]))
        s = s * d_end + _mm(ke_b[gs], v_new, form="tn")
        for h in range(nh):
            obuf_ref[sub(j), head(h)] = o[h]
    s_ref[...] = s

    o = obuf_ref[...]
    ms = jnp.concatenate([_mm(o[:, head(h)] * o[:, head(h)], ones, (2, 1)) for h in range(nh)], axis=1)
    on = o * lax.rsqrt(ms * (1.0 / HEAD_DIM_GDN) + NORM_EPS) * normw_ref[...]
    z = x[:, wq:]
    y_ref[0] = (on * (z * _sigmoid(z))).astype(y_ref.dtype)


def _gdn_mix(p3d, conv_w, a_log, dt_bias, norm_w, rows=256, chunk=128):
    bsz, t, _ = p3d.shape
    rows = min(rows, t)
    pad = jnp.zeros((LANES - N_HEADS_GDN,), F32)
    hp = jnp.stack([jnp.concatenate([a_log.astype(F32), pad]), jnp.concatenate([dt_bias.astype(F32), pad])])
    eye = jnp.eye(LANES, dtype=BF16)
    ones = jnp.ones((LANES, LANES), BF16)
    normw = jnp.tile(norm_w.astype(F32), N_HEADS_GDN).reshape(1, W_GDN)
    const = lambda shape: pl.BlockSpec(shape, lambda b, c: (0,) * len(shape))
    return pl.pallas_call(
        functools.partial(_gdn_kernel, rows=rows, chunk=chunk),
        grid=(bsz, t // rows),
        in_specs=[
            pl.BlockSpec((1, rows, 4 * W_GDN), lambda b, c: (b, c, COL_GDN // (4 * W_GDN))),
            pl.BlockSpec((1, rows, LANES), lambda b, c: (b, c, COL_GDN_AB // LANES)),
            const((CONV_WIDTH, 3 * W_GDN)), const((2, LANES)), const((1, W_GDN)),
            const((rows, rows)), const((LANES, LANES)), const((LANES, LANES)),
        ],
        out_specs=pl.BlockSpec((1, rows, W_GDN), lambda b, c: (b, c, 0)),
        out_shape=jax.ShapeDtypeStruct((bsz, t, W_GDN), BF16),
        scratch_shapes=[pltpu.VMEM((rows + 8, 3 * W_GDN), F32),
                        pltpu.VMEM((N_HEADS_GDN, HEAD_DIM_GDN, HEAD_DIM_GDN), F32),
                        pltpu.VMEM((rows, W_GDN), F32)],
        compiler_params=pltpu.CompilerParams(
            dimension_semantics=("parallel", "arbitrary"), vmem_limit_bytes=VMEM_LIMIT),
    )(p3d, p3d, conv_w.astype(F32), hp, normw, _block_tri(rows, chunk), eye, ones)


def _diff_prep_kernel(q_ref, k_ref, v_ref, qw_ref, kw_ref, gmat_ref, eye_ref, qt_ref, ko_ref, vt_ref, *, tq, tk):
    gmat = gmat_ref[...]
    eye = eye_ref[...]
    inv_d = 1.0 / HEAD_DIM_DIFF
    tm = q_ref.shape[1]

    def normed(src, wref, sl):
        x = src[0, :, sl]
        ms = _mm(x * x, gmat, PREC["gsum"]) * inv_d
        return (x * lax.rsqrt(ms + NORM_EPS) * wref[...]).astype(BF16)

    for hb in range(N_HEADS_DIFF):
        sl = slice(hb * LANES, (hb + 1) * LANES)
        ko_ref[0, :, sl] = normed(k_ref, kw_ref, sl)
        q_t = _mm(eye, normed(q_ref, qw_ref, sl), form="nt").astype(BF16)
        v_t = _mm(eye, v_ref[0, :, sl].astype(BF16), form="nt").astype(BF16)
        for i in range(tm // tq):
            qt_ref[0, hb, i] = q_t[:, i * tq:(i + 1) * tq]
        for i in range(tm // tk):
            vt_ref[0, hb, i] = v_t[:, i * tk:(i + 1) * tk]


def _diff_prep(p3d, q_norm_w, k_norm_w, tq, tk):
    bsz, t, _ = p3d.shape
    tm = tk
    lane = jnp.arange(LANES)
    gmat = ((lane[:, None] // HEAD_DIM_DIFF) == (lane[None, :] // HEAD_DIM_DIFF)).astype(BF16)
    eye = jnp.eye(LANES, dtype=BF16)
    qw = (jnp.tile(q_norm_w.astype(F32), 2) * (HEAD_DIM_DIFF ** -0.5 * math.log2(math.e))).reshape(1, LANES)
    kw = jnp.tile(k_norm_w.astype(F32), 2).reshape(1, LANES)
    base = COL_DIFF // W_DIFF
    sec = lambda j: pl.BlockSpec((1, tm, W_DIFF), lambda b, i: (b, i, base + j))
    const = lambda shape: pl.BlockSpec(shape, lambda b, i: (0,) * len(shape))
    nh = N_HEADS_DIFF
    return pl.pallas_call(
        functools.partial(_diff_prep_kernel, tq=tq, tk=tk),
        grid=(bsz, t // tm),
        in_specs=[sec(0), sec(1), sec(2), const((1, LANES)), const((1, LANES)),
                  const((LANES, LANES)), const((LANES, LANES))],
        out_specs=(pl.BlockSpec((1, nh, tm // tq, LANES, tq), lambda b, i: (b, 0, i, 0, 0)),
                   pl.BlockSpec((1, tm, W_DIFF), lambda b, i: (b, i, 0)),
                   pl.BlockSpec((1, nh, tm // tk, LANES, tk), lambda b, i: (b, 0, i, 0, 0))),
        out_shape=(jax.ShapeDtypeStruct((bsz, nh, t // tq, LANES, tq), BF16),
                   jax.ShapeDtypeStruct((bsz, t, W_DIFF), BF16),
                   jax.ShapeDtypeStruct((bsz, nh, t // tk, LANES, tk), BF16)),
        compiler_params=pltpu.CompilerParams(
            dimension_semantics=("parallel", "parallel"), vmem_limit_bytes=VMEM_LIMIT),
    )(p3d, p3d, p3d, qw, kw, gmat, eye)


def _diff_attn_kernel(qt_ref, k_ref, vt_ref, lam_ref, subw_ref, eye_ref, o_ref, *, tq, tk, lambda_init):
    t = k_ref.shape[1]
    feat = lax.broadcasted_iota(jnp.int32, (LANES, 1), 0)
    lv = lam_ref[...]
    lam = (jnp.exp(jnp.sum(lv[0:1] * lv[1:2], axis=-1, keepdims=True))
           - jnp.exp(jnp.sum(lv[2:3] * lv[3:4], axis=-1, keepdims=True)) + lambda_init)
    eye = eye_ref[...]

    tiles = []
    for i in range(t // tq):
        q_end = (i + 1) * tq
        for k0 in range(0, q_end, tk):
            klen = min(tk, q_end - k0)
            tiles.append((i, k0, klen, k0 + klen > i * tq))

    q_maps = {}

    def scores(i, k0, klen):
        if i not in q_maps:
            q_t = qt_ref[0, 0, i]
            zero = jnp.zeros_like(q_t)
            q_maps.clear()
            q_maps[i] = (jnp.where(feat < HEAD_DIM_DIFF, q_t, zero), jnp.where(feat >= HEAD_DIM_DIFF, q_t, zero))
        k = k_ref[0, k0:k0 + klen, :]
        return [_dot(k, qm) for qm in q_maps[i]]

    state = None
    pending = scores(*tiles[0][:3])
    for n, (i, k0, klen, diag) in enumerate(tiles):
        cur = pending
        if n + 1 < len(tiles):
            pending = scores(*tiles[n + 1][:3])
        v_t = vt_ref[0, 0, k0 // tk][:, (k0 % tk):(k0 % tk) + klen]
        new_state = []
        for mp, s in enumerate(cur):
            if diag:
                key = lax.broadcasted_iota(jnp.int32, (klen, tq), 0)
                qry = lax.broadcasted_iota(jnp.int32, (klen, tq), 1)
                s = jnp.where(key - qry <= i * tq - k0, s, -jnp.inf)
            if k0 == 0:
                m_new = jnp.max(s, axis=0, keepdims=True)
                p = jnp.exp2(s - m_new)
                l_new = jnp.sum(p, axis=0, keepdims=True)
                acc_new = _dot(v_t, p.astype(BF16))
            else:
                m_prev, l_prev, acc_prev = state[mp]
                m_new = jnp.maximum(m_prev, jnp.max(s, axis=0, keepdims=True))
                alpha = jnp.exp2(m_prev - m_new)
                p = jnp.exp2(s - m_new)
                l_new = alpha * l_prev + jnp.sum(p, axis=0, keepdims=True)
                acc_new = alpha * acc_prev + _dot(v_t, p.astype(BF16))
            new_state.append((m_new, l_new, acc_new))
        state = new_state
        if k0 + klen == (i + 1) * tq:
            (_, l0, acc0), (_, l1, acc1) = state
            o_t = acc0 * (1.0 / l0) - lam * (acc1 * (1.0 / l1))
            ms = jnp.sum(o_t * o_t, axis=0, keepdims=True) * (1.0 / LANES)
            on_t = (o_t * lax.rsqrt(ms + NORM_EPS) * subw_ref[...] * (1.0 - lambda_init)).astype(BF16)
            o_ref[0, i * tq:(i + 1) * tq, :] = _mm(on_t, eye, form="tn").astype(o_ref.dtype)


def _diff_attn(p3d, q_norm_w, k_norm_w, lam_vecs, subln_w, lambda_init, tq=256, tk=512):
    bsz, t, _ = p3d.shape
    tk = min(tk, t)
    tq = min(tq, tk)
    q_t, kn, v_t = _diff_prep(p3d, q_norm_w, k_norm_w, tq, tk)
    const = lambda shape: pl.BlockSpec(shape, lambda b, h: (0,) * len(shape))
    return pl.pallas_call(
        functools.partial(_diff_attn_kernel, tq=tq, tk=tk, lambda_init=lambda_init),
        grid=(bsz, N_HEADS_DIFF),
        in_specs=[pl.BlockSpec((1, 1, t // tq, LANES, tq), lambda b, h: (b, h, 0, 0, 0)),
                  pl.BlockSpec((1, t, LANES), lambda b, h: (b, 0, h)),
                  pl.BlockSpec((1, 1, t // tk, LANES, tk), lambda b, h: (b, h, 0, 0, 0)),
                  const((4, HEAD_DIM_DIFF)), const((LANES, 1)), const((LANES, LANES))],
        out_specs=pl.BlockSpec((1, t, LANES), lambda b, h: (b, 0, h)),
        out_shape=jax.ShapeDtypeStruct((bsz, t, W_DIFF), BF16),
        compiler_params=pltpu.CompilerParams(
            dimension_semantics=("parallel", "parallel"), vmem_limit_bytes=VMEM_LIMIT),
    )(q_t, kn, v_t, lam_vecs, subln_w.reshape(LANES, 1).astype(F32), jnp.eye(LANES, dtype=BF16))


def _mix_ffn_kernel(x_ref, yr_ref, yg_ref, yd_ref, w1_ref, w2_ref, w3_ref, nw_ref, wg_ref, wu_ref, wo_ref,
                    o_ref, h_ref):
    @pl.when(pl.program_id(1) == 0)
    def _():
        x1 = (x_ref[...] + _dot(yr_ref[...], w1_ref[...]) + _dot(yg_ref[...], w2_ref[...])
              + _dot(yd_ref[...], w3_ref[...]))
        ms = jnp.mean(x1 * x1, axis=-1, keepdims=True)
        h_ref[...] = (x1 * lax.rsqrt(ms + NORM_EPS) * nw_ref[...]).astype(BF16)
        o_ref[...] = x1

    h = h_ref[...]
    gate = _dot(h, wg_ref[...])
    up = _dot(h, wu_ref[...])
    act = (gate * _sigmoid(gate) * up).astype(BF16)
    o_ref[...] += _dot(act, wo_ref[...])


def _mix_ffn(x2d, y_rw, y_gdn, y_diff, w_mix_bf16, nw, w_in_bf16, w_out_bf16, tm=512, tf=512):
    n, d = x2d.shape
    hidden = w_out_bf16.shape[0]
    nf = hidden // tf
    rowblk = lambda w: pl.BlockSpec((tm, w), lambda i, f: (i, 0))
    resident = lambda rows, blk: pl.BlockSpec((rows, d), lambda i, f: (blk, 0), pipeline_mode=pl.Buffered(1))
    return pl.pallas_call(
        _mix_ffn_kernel,
        grid=(n // tm, nf),
        in_specs=[
            rowblk(d), rowblk(W_RWKV), rowblk(W_GDN), rowblk(W_DIFF),
            resident(W_RWKV, 0), resident(W_GDN, 1), resident(W_DIFF, 1),
            pl.BlockSpec((1, d), lambda i, f: (0, 0)),
            pl.BlockSpec((d, tf), lambda i, f: (0, f)),
            pl.BlockSpec((d, tf), lambda i, f: (0, nf + f)),
            pl.BlockSpec((tf, d), lambda i, f: (f, 0)),
        ],
        out_specs=pl.BlockSpec((tm, d), lambda i, f: (i, 0)),
        out_shape=jax.ShapeDtypeStruct((n, d), F32),
        scratch_shapes=[pltpu.VMEM((tm, d), BF16)],
        compiler_params=pltpu.CompilerParams(
            dimension_semantics=("parallel", "arbitrary"), vmem_limit_bytes=VMEM_LIMIT),
    )(x2d, y_rw, y_gdn, y_diff, w_mix_bf16, w_mix_bf16, w_mix_bf16, nw.reshape(1, d),
      w_in_bf16, w_in_bf16, w_out_bf16)


def _pack_in_proj(w_in_l, w_vres_l):
    d = w_in_l.shape[0]
    w = w_in_l.astype(BF16)
    z = lambda n: jnp.zeros((d, n), BF16)
    gdn0 = RWKV_COLS
    vres = z(LANES) if w_vres_l is None else jnp.concatenate([w_vres_l.astype(BF16), z(LANES - LORA_VRES)], axis=1)
    cols = [w[:, :RWKV_COLS], vres,
            w[:, gdn0 + 4 * W_GDN:gdn0 + GDN_COLS], z(LANES - 2 * N_HEADS_GDN),
            w[:, gdn0:gdn0 + 4 * W_GDN],
            w[:, gdn0 + GDN_COLS:]]
    return jnp.concatenate(cols, axis=1)


def kernel(x, attn_norm_w, w_in, w_vres_a, mu_rwkv, mu_vres, rwkv_w0, rwkv_w_lora_b, rwkv_a0, rwkv_a_lora_b, rwkv_g_lora_b, rwkv_v0, rwkv_v_lora_b, rwkv_k_k, rwkv_k_a, rwkv_r_k, rwkv_ln_w, rwkv_ln_b, gdn_conv_w, gdn_A_log, gdn_dt_bias, gdn_norm_w, diff_q_norm_w, diff_k_norm_w, diff_lambda_q1, diff_lambda_k1, diff_lambda_q2, diff_lambda_k2, diff_subln_w, w_out, ffn_norm_w, w_ffn_in, w_ffn_out):
    bsz, t, d = x.shape
    depth = w_in.shape[0]
    x2d = x.reshape(bsz * t, d)
    v_first = None
    for l in range(depth):
        w_proj = _pack_in_proj(w_in[l], None if l == 0 else w_vres_a[l - 1])
        mu_tail = jnp.zeros((RW_BLOCK - RWKV_COLS,), F32)
        if l > 0:
            mu_tail = mu_tail.at[:LORA_VRES].set(mu_vres[l - 1])
        mu_pad = jnp.concatenate([mu_rwkv[l], mu_tail]).reshape(1, RW_BLOCK)
        p3d = _norm_matmul(x2d, attn_norm_w[l], w_proj).reshape(bsz, t, N_PAD)
        rw_args = (rwkv_w0[l], rwkv_w_lora_b[l], rwkv_a0[l], rwkv_a_lora_b[l], rwkv_g_lora_b[l])
        rw_tail = (rwkv_k_k[l], rwkv_k_a[l], rwkv_r_k[l], rwkv_ln_w[l], rwkv_ln_b[l])
        if l == 0:
            y_rw, v_first = _rwkv_mix(p3d, None, mu_pad, *rw_args, None, None, *rw_tail)
        else:
            y_rw = _rwkv_mix(p3d, v_first, mu_pad, *rw_args, rwkv_v0[l - 1], rwkv_v_lora_b[l - 1], *rw_tail)
        y_gdn = _gdn_mix(p3d, gdn_conv_w[l], gdn_A_log[l], gdn_dt_bias[l], gdn_norm_w[l])
        lambda_init = 0.8 - 0.6 * math.exp(-0.3 * l)
        lam_vecs = jnp.stack([diff_lambda_q1[l], diff_lambda_k1[l], diff_lambda_q2[l], diff_lambda_k2[l]]).astype(F32)
        y_diff = _diff_attn(p3d, diff_q_norm_w[l], diff_k_norm_w[l], lam_vecs, diff_subln_w[l], lambda_init)
        n = bsz * t
        x2d = _mix_ffn(x2d, y_rw.reshape(n, W_RWKV), y_gdn.reshape(n, W_GDN), y_diff.reshape(n, W_DIFF),
                       w_out[l].astype(BF16), ffn_norm_w[l], w_ffn_in[l].astype(BF16), w_ffn_out[l].astype(BF16))
    return x2d.reshape(bsz, t, d)
```

```python
import functools
import math

import jax
import jax.numpy as jnp
from jax import lax
from jax.experimental import pallas as pl
from jax.experimental.pallas import tpu as pltpu

F32 = jnp.float32
BF16 = jnp.bfloat16

D_MODEL = 2048
W_RWKV = D_MODEL // 4
HEAD_DIM_RWKV = 64
LORA_DECAY = 64
LORA_ICLR = 64
LORA_VRES = 32
LORA_GATE = 128
RWKV_GN_EPS = 64e-5
W_GDN = D_MODEL // 4
HEAD_DIM_GDN = 128
N_HEADS_GDN = W_GDN // HEAD_DIM_GDN
CONV_WIDTH = 4
W_DIFF = D_MODEL // 2
HEAD_DIM_DIFF = 64
N_HEADS_DIFF = W_DIFF // (2 * HEAD_DIM_DIFF)
RWKV_COLS = 3 * W_RWKV + LORA_DECAY + LORA_ICLR + LORA_GATE
GDN_COLS = 4 * W_GDN + 2 * N_HEADS_GDN
DIFF_COLS = 3 * W_DIFF
N_IN = RWKV_COLS + GDN_COLS + DIFF_COLS
FFN_HIDDEN = -(-8 * D_MODEL // (3 * 256)) * 256
NORM_EPS = 1e-6

LANES = 128
SOLVE_BLOCK = 16

COL_VRES = RWKV_COLS
COL_GDN_AB = COL_VRES + LANES
COL_GDN = COL_GDN_AB + LANES
COL_DIFF = COL_GDN + 4 * W_GDN
N_PAD = COL_DIFF + DIFF_COLS
RW_BLOCK = COL_GDN

VMEM_LIMIT = 56 * 1024 * 1024

PREC = {
    "lora": (1, 1),
    "ones": (1, 3),
    "gsum": (1, 1),
    "inv": (1, 1),
}


def _pieces(x, n):
    if isinstance(x, (list, tuple)):
        return list(x)[:n]
    if x.dtype == BF16:
        return [x]
    out, rem = [], x
    for i in range(n):
        piece = rem.astype(BF16)
        out.append(piece)
        if i + 1 < n:
            rem = rem - piece.astype(F32)
    return out


_DIMS = {"nn": (((1,), (0,)), ((), ())), "nt": (((1,), (1,)), ((), ())), "tn": (((0,), (0,)), ((), ()))}
_DIMS_B = {"nn": (((2,), (1,)), ((0,), (0,))), "nt": (((2,), (2,)), ((0,), (0,))), "tn": (((1,), (1,)), ((0,), (0,)))}


def _mm(a, b, prec=(1, 1), form="nn"):
    pa, pb = _pieces(a, prec[0]), _pieces(b, prec[1])
    dims = (_DIMS_B if pa[0].ndim == 3 else _DIMS)[form]
    depth = max(len(pa), len(pb))
    terms = sorted(((i, j) for i in range(len(pa)) for j in range(len(pb)) if i + j < depth),
                   key=lambda ij: -(ij[0] + ij[1]))
    acc = None
    for i, j in terms:
        d = lax.dot_general(pa[i], pb[j], dims, preferred_element_type=F32)
        acc = d if acc is None else acc + d
    return acc


def _dot(a, b):
    return jnp.dot(a, b, preferred_element_type=F32)


def _sigmoid(x):
    return 1.0 / (1.0 + jnp.exp(-x))


def _softplus(x):
    return jnp.maximum(x, 0.0) + jnp.log(1.0 + jnp.exp(-jnp.abs(x)))


def _split_bf16(w, n):
    out, rem = [], w.astype(F32)
    for _ in range(n):
        piece = rem.astype(BF16)
        out.append(piece)
        rem = rem - piece.astype(F32)
    return jnp.stack(out)


def _block_tri(rows, chunk):
    idx = jnp.arange(rows)
    same = (idx[:, None] // chunk) == (idx[None, :] // chunk)
    return (same & (idx[:, None] >= idx[None, :])).astype(BF16)


def _unit_lower_inverse(lo_tri, n_sub, prec):
    n = lo_tri.shape[-1]
    lead = (1,) * (lo_tri.ndim - 2)
    row = lax.broadcasted_iota(jnp.int32, lead + (n, n), lo_tri.ndim - 2)
    col = lax.broadcasted_iota(jnp.int32, lead + (n, n), lo_tri.ndim - 1)
    eye = (row == col).astype(F32)
    same = (row // SOLVE_BLOCK) == (col // SOLVE_BLOCK)
    l_diag = jnp.where(same, lo_tri, 0.0)
    l_off = jnp.where(same, 0.0, lo_tri)

    def neumann(a, s, order):
        width = 1
        while width < order:
            if 2 * width < order:
                both = _mm(a, jnp.concatenate([s, a], axis=-1), prec)
                s = s + both[..., :n]
                a = both[..., n:]
            else:
                s = s + _mm(a, s, prec)
            width *= 2
        return s

    t_diag = neumann(_mm(l_diag, l_diag, prec), eye + l_diag, SOLVE_BLOCK // 2)
    return neumann(_mm(t_diag, l_off, prec), t_diag, n_sub)


def _norm_matmul_kernel(x_ref, nw_ref, wa_ref, wg_ref, wd_ref, o_ref, h_ref, *, na, ng):
    j = pl.program_id(1)

    @pl.when(j == 0)
    def _():
        x = x_ref[...]
        ms = jnp.mean(x * x, axis=-1, keepdims=True)
        h_ref[...] = (x * lax.rsqrt(ms + NORM_EPS) * nw_ref[...]).astype(BF16)

    @pl.when(j < na)
    def _():
        o_ref[...] = _dot(h_ref[...], wa_ref[...])

    @pl.when((j >= na) & (j < na + ng))
    def _():
        o_ref[...] = _dot(h_ref[...], wg_ref[...])

    @pl.when(j >= na + ng)
    def _():
        o_ref[...] = _dot(h_ref[...], wd_ref[...])


def _norm_matmul(x2d, nw, w_sections, tm=1024, tn=1024):
    n, d = x2d.shape
    tm = min(tm, n)
    wa, wg, wd = w_sections
    na, ng, nd = wa.shape[1] // tn, wg.shape[1] // tn, wd.shape[1] // tn
    return pl.pallas_call(
        functools.partial(_norm_matmul_kernel, na=na, ng=ng),
        grid=(n // tm, na + ng + nd),
        in_specs=[
            pl.BlockSpec((tm, d), lambda i, j: (i, 0)),
            pl.BlockSpec((1, d), lambda i, j: (0, 0)),
            pl.BlockSpec((d, tn), lambda i, j: (0, jnp.minimum(j, na - 1))),
            pl.BlockSpec((d, tn), lambda i, j: (0, jnp.clip(j - na, 0, ng - 1))),
            pl.BlockSpec((d, tn), lambda i, j: (0, jnp.maximum(j - na - ng, 0))),
        ],
        out_specs=pl.BlockSpec((tm, tn), lambda i, j: (i, j)),
        out_shape=jax.ShapeDtypeStruct((n, (na + ng + nd) * tn), F32),
        scratch_shapes=[pltpu.VMEM((tm, d), BF16)],
        compiler_params=pltpu.CompilerParams(
            dimension_semantics=("parallel", "arbitrary"), vmem_limit_bytes=VMEM_LIMIT),
    )(x2d, nw.reshape(1, d), wa, wg, wd)


def _rwkv_kernel(*refs, rows, chunk, has_vres):
    if has_vres:
        (p_ref, vf_ref, mu_ref, w0_ref, wbw_ref, a0_ref, wba_ref, wbg_ref, v0_ref, wbv_ref,
         kk_ref, ka_ref, rk_ref, lnw_ref, lnb_ref, gmat_ref, tri_ref,
         y_ref, buf_ref, s_ref, ybuf_ref) = refs
    else:
        (p_ref, mu_ref, w0_ref, wbw_ref, a0_ref, wba_ref, wbg_ref,
         kk_ref, ka_ref, rk_ref, lnw_ref, lnb_ref, gmat_ref, tri_ref,
         y_ref, vout_ref, buf_ref, s_ref, ybuf_ref) = refs
    c = chunk
    ns = rows // c
    n_pairs = W_RWKV // LANES
    p_lora, p_ones, p_gsum, p_inv = PREC["lora"], PREC["ones"], PREC["gsum"], PREC["inv"]
    wpieces = lambda ref: [ref[i] for i in range(ref.shape[0])]
    pair = lambda pi: slice(pi * LANES, (pi + 1) * LANES)
    sub = lambda j: slice(j * c, (j + 1) * c)

    @pl.when(pl.program_id(1) == 0)
    def _():
        buf_ref[0:8, :] = jnp.zeros((8, RW_BLOCK), F32)
        s_ref[...] = jnp.zeros_like(s_ref)

    x = p_ref[0]
    buf_ref[8:8 + rows, :] = x
    prev = buf_ref[7:7 + rows, :]
    buf_ref[0:8, :] = x[rows - 8:rows, :]
    pm = x + (prev - x) * mu_ref[...]

    r = pm[:, 0:W_RWKV]
    k = pm[:, W_RWKV:2 * W_RWKV]
    v = pm[:, 2 * W_RWKV:3 * W_RWKV]
    lw = pm[:, 3 * W_RWKV:3 * W_RWKV + LANES]
    xg = pm[:, 3 * W_RWKV + LANES:3 * W_RWKV + 2 * LANES]
    w_log = -_softplus(-(w0_ref[...] + _mm(jnp.tanh(lw), wpieces(wbw_ref), p_lora))) - 0.5
    logd = -jnp.exp(w_log)
    a = _sigmoid(a0_ref[...] + _mm(lw, wpieces(wba_ref), p_lora))
    g = _mm(_sigmoid(xg), wpieces(wbg_ref), p_lora)
    if has_vres:
        xv = pm[:, COL_VRES:COL_VRES + LANES]
        v = v + (vf_ref[0] - v) * _sigmoid(v0_ref[...] + _mm(xv, wpieces(wbv_ref), p_lora))
    else:
        vout_ref[0] = v

    gmat = gmat_ref[...]

    def gsum(z):
        return jnp.concatenate([_mm(z[:, pair(pi)], gmat, p_gsum) for pi in range(n_pairs)], axis=1)

    kk = k * kk_ref[...]
    kk = kk * lax.rsqrt(gsum(kk * kk) + NORM_EPS)
    k2 = k * (1.0 + (a - 1.0) * ka_ref[...])
    b = kk * a
    gc = _mm(tri_ref[...], logd, p_ones)
    g_end = [gc[j * c + c - 1:j * c + c, :] for j in range(ns)]
    e_end = jnp.exp(jnp.concatenate([g_end[j] - gc[sub(j), :] for j in range(ns)], axis=0))
    e_inv = jnp.exp(-gc)

    lane = lax.broadcasted_iota(jnp.int32, (1, W_RWKV), 1)
    h0 = ((lane // HEAD_DIM_RWKV) % 2 == 0).astype(F32)
    h1 = 1.0 - h0

    def stack_all(z):
        z0 = (z * h0).astype(BF16)
        z1 = (z * h1).astype(BF16)
        return jnp.stack([jnp.concatenate([z0[sub(j), pair(pi)], z1[sub(j), pair(pi)]], axis=0)
                          for j in range(ns) for pi in range(n_pairs)])

    at = stack_all(-kk * jnp.exp(gc - logd))
    rt = stack_all(r * jnp.exp(gc))
    bt = stack_all(b * e_inv)
    kt = stack_all(k2 * e_inv)
    vst = stack_all(v)
    bh = stack_all(b * e_end)
    kh = stack_all(k2 * e_end)

    n2 = 2 * c
    quad = _mm(jnp.concatenate([at, rt], axis=1), jnp.concatenate([bt, kt], axis=1), form="nt")
    row = lax.broadcasted_iota(jnp.int32, (1, n2, n2), 1) % c
    col = lax.broadcasted_iota(jnp.int32, (1, n2, n2), 2) % c
    strict = row > col
    incl = row >= col
    l_ab = jnp.where(strict, quad[:, :n2, :n2], 0.0)
    a_rb = jnp.where(incl, quad[:, n2:, :n2], 0.0).astype(BF16)
    vmask = jnp.concatenate([jnp.broadcast_to(strict, (1, n2, n2)), jnp.broadcast_to(incl, (1, n2, n2))], axis=1)
    own = _mm(jnp.where(vmask, quad[:, :, n2:], 0.0).astype(BF16), vst)
    y0 = own[:, n2:]
    t_inv = _unit_lower_inverse(l_ab, c // SOLVE_BLOCK, p_inv).astype(BF16)
    t_both = _mm(t_inv, jnp.concatenate([at, own[:, :n2].astype(BF16)], axis=-1))
    u0 = t_both[:, :, LANES:]
    wr = jnp.concatenate([t_both[:, :, :LANES].astype(BF16), rt], axis=1)
    kv = _mm(vst, kh, form="tn")

    s = s_ref[...]
    for j in range(ns):
        gs = slice(j * n_pairs, (j + 1) * n_pairs)
        from_s = _mm(wr[gs], s.astype(BF16), form="nt")
        u = u0[gs] + from_s[:, :n2]
        u_b = u.astype(BF16)
        y_st = from_s[:, n2:] + _mm(a_rb[gs], u_b) + y0[gs]
        d_end = jnp.exp(jnp.stack([g_end[j][:, pair(pi)] for pi in range(n_pairs)]))
        s = s * d_end + _mm(u_b, bh[gs], form="tn") + kv[gs]
        y = y_st[:, :c] + y_st[:, c:]
        for pi in range(n_pairs):
            ybuf_ref[sub(j), pair(pi)] = y[pi]
    s_ref[...] = s

    y = ybuf_ref[...]
    inv_n = 1.0 / HEAD_DIM_RWKV
    mean = gsum(y) * inv_n
    dlt = y - mean
    var = gsum(dlt * dlt) * inv_n
    yn = dlt * lax.rsqrt(var + RWKV_GN_EPS) * lnw_ref[...] + lnb_ref[...]
    bonus = gsum(r * k2 * rk_ref[...]) * v
    y_ref[0] = ((yn + bonus) * g).astype(y_ref.dtype)


def _rwkv_mix(p3d, v_first, mu_pad, w0, wbw, a0, wba, wbg, v0, wbv, k_k, k_a, r_k, ln_w, ln_b, rows=256, chunk=64):
    bsz, t, _ = p3d.shape
    rows = min(rows, t)
    has_vres = v_first is not None
    n_lora = PREC["lora"][1]
    row = lambda z: z.reshape(1, -1).astype(F32)
    lane = jnp.arange(LANES)
    gmat = ((lane[:, None] // HEAD_DIM_RWKV) == (lane[None, :] // HEAD_DIM_RWKV)).astype(BF16)
    zeros = jnp.zeros((LORA_DECAY, W_RWKV), F32)
    wbw_pad = _split_bf16(jnp.concatenate([wbw, zeros], axis=0), n_lora)
    wba_pad = _split_bf16(jnp.concatenate([zeros, wba], axis=0), n_lora)
    const = lambda shape: pl.BlockSpec(shape, lambda b, c: (0,) * len(shape))
    lora_spec = const((n_lora, LANES, W_RWKV))
    p_spec = pl.BlockSpec((1, rows, RW_BLOCK), lambda b, c: (b, c, 0))
    seq_spec = pl.BlockSpec((1, rows, W_RWKV), lambda b, c: (b, c, 0))
    args = [p3d]
    specs = [p_spec]
    if has_vres:
        args.append(v_first)
        specs.append(seq_spec)
    args += [mu_pad, row(w0), wbw_pad, row(a0), wba_pad, _split_bf16(wbg, n_lora)]
    specs += [const((1, RW_BLOCK)), const((1, W_RWKV)), lora_spec, const((1, W_RWKV)), lora_spec, lora_spec]
    if has_vres:
        wbv_pad = jnp.concatenate([wbv, jnp.zeros((LANES - LORA_VRES, W_RWKV), F32)], axis=0)
        args += [row(v0), _split_bf16(wbv_pad, n_lora)]
        specs += [const((1, W_RWKV)), lora_spec]
    args += [row(k_k), row(k_a), row(r_k), row(ln_w), row(ln_b), gmat, _block_tri(rows, chunk)]
    specs += [const((1, W_RWKV))] * 5 + [const((LANES, LANES)), const((rows, rows))]
    y_shape = jax.ShapeDtypeStruct((bsz, t, W_RWKV), BF16)
    if has_vres:
        out_shape, out_specs = y_shape, seq_spec
    else:
        out_shape = (y_shape, jax.ShapeDtypeStruct((bsz, t, W_RWKV), F32))
        out_specs = (seq_spec, seq_spec)
    return pl.pallas_call(
        functools.partial(_rwkv_kernel, rows=rows, chunk=chunk, has_vres=has_vres),
        grid=(bsz, t // rows),
        in_specs=specs,
        out_specs=out_specs,
        out_shape=out_shape,
        scratch_shapes=[pltpu.VMEM((rows + 8, RW_BLOCK), F32),
                        pltpu.VMEM((W_RWKV // LANES, LANES, LANES), F32),
                        pltpu.VMEM((rows, W_RWKV), F32)],
        compiler_params=pltpu.CompilerParams(
            dimension_semantics=("parallel", "arbitrary"), vmem_limit_bytes=VMEM_LIMIT),
    )(*args)


def _gdn_kernel(x_ref, ab_ref, convw_ref, hp_ref, normw_ref, tri_ref, eye_ref, ones_ref,
                y_ref, buf_ref, s_ref, obuf_ref, *, rows, chunk):
    c = chunk
    ns = rows // c
    nh = N_HEADS_GDN
    wq = 3 * W_GDN
    p_ones, p_inv = PREC["ones"], PREC["inv"]
    head = lambda h: slice(h * LANES, (h + 1) * LANES)
    sub = lambda j: slice(j * c, (j + 1) * c)

    @pl.when(pl.program_id(1) == 0)
    def _():
        buf_ref[0:8, :] = jnp.zeros((8, wq), F32)
        s_ref[...] = jnp.zeros_like(s_ref)

    x = x_ref[0]
    xc = x[:, :wq]
    buf_ref[8:8 + rows, :] = xc
    conv = xc * convw_ref[3:4, :]
    for i in range(CONV_WIDTH - 1):
        conv = conv + buf_ref[5 + i:5 + i + rows, :] * convw_ref[i:i + 1, :]
    buf_ref[0:8, :] = xc[rows - 8:rows, :]
    qkv = conv * _sigmoid(conv)

    ab = ab_ref[0]
    g_all = -jnp.exp(hp_ref[0:1, :]) * _softplus(ab + hp_ref[1:2, :])
    beta_all = _sigmoid(ab)
    gc_all = _mm(tri_ref[...], g_all, p_ones)
    gc_t = _mm(eye_ref[...], gc_all, p_ones, "nt")
    ones = ones_ref[...]

    def l2n(z):
        return z * lax.rsqrt(_mm(z * z, ones, (2, 1)) + NORM_EPS)

    chains = [(j, h) for j in range(ns) for h in range(nh)]
    k_l, kb_l, q_l, vk_l, qe_l, ke_l, gcol_l, grow_l, gend_l = [], [], [], [], [], [], [], [], []
    for h in range(nh):
        q = l2n(qkv[:, head(h)]) * (HEAD_DIM_GDN ** -0.5)
        k = l2n(qkv[:, W_GDN + h * LANES:W_GDN + (h + 1) * LANES])
        v = qkv[:, 2 * W_GDN + h * LANES:2 * W_GDN + (h + 1) * LANES]
        gcol = gc_all[:, h:h + 1]
        beta = beta_all[:, nh + h:nh + h + 1]
        g_end = [gcol[j * c + c - 1:j * c + c, :] for j in range(ns)]
        e_gc = jnp.exp(gcol)
        e_end = jnp.exp(jnp.concatenate([g_end[j] - gcol[sub(j), :] for j in range(ns)], axis=0))
        kb = k * beta
        k_l.append(k.astype(BF16))
        kb_l.append(kb.astype(BF16))
        q_l.append(q.astype(BF16))
        vk_l.append(jnp.concatenate([v * beta, kb * e_gc], axis=1).astype(BF16))
        qe_l.append((q * e_gc).astype(BF16))
        ke_l.append((k * e_end).astype(BF16))
        gcol_l.append(gcol)
        grow_l.append(gc_t[h:h + 1, :])
        gend_l.append(g_end)
    gather = lambda lst: jnp.stack([lst[h][sub(j), :] for j, h in chains])
    k_b, kb_b, q_b, vk_b, qe_b, ke_b = (gather(l) for l in (k_l, kb_l, q_l, vk_l, qe_l, ke_l))
    gcol_b = gather(gcol_l)
    grow_b = jnp.stack([grow_l[h][:, sub(j)] for j, h in chains])

    row = lax.broadcasted_iota(jnp.int32, (1, c, c), 1)
    col = lax.broadcasted_iota(jnp.int32, (1, c, c), 2)
    decay = jnp.exp(jnp.where(row >= col, gcol_b - grow_b, -jnp.inf))
    both = _mm(jnp.concatenate([kb_b, q_b], axis=1), k_b, form="nt")
    kkt, qkt = both[:, :c], both[:, c:]
    t_inv = _unit_lower_inverse(-jnp.where(row > col, kkt * decay, 0.0), c // SOLVE_BLOCK, p_inv).astype(BF16)
    uw = _mm(t_inv, vk_b)
    u = uw[:, :, :LANES]
    wq_b = jnp.concatenate([uw[:, :, LANES:].astype(BF16), qe_b], axis=1)
    attn = (qkt * decay).astype(BF16)

    s = s_ref[...]
    for j in range(ns):
        gs = slice(j * nh, (j + 1) * nh)
        ws_qs = _mm(wq_b[gs], s.astype(BF16))
        v_new = (u[gs] - ws_qs[:, :c]).astype(BF16)
        o = ws_qs[:, c:] + _mm(attn[gs], v_new)
        d_end = jnp.exp(jnp.stack([gend_l[h][j] for h in range(nh)]))
        s = s * d_end + _mm(ke_b[gs], v_new, form="tn")
        for h in range(nh):
            obuf_ref[sub(j), head(h)] = o[h]
    s_ref[...] = s

    o = obuf_ref[...]
    ms = jnp.concatenate([_mm(o[:, head(h)] * o[:, head(h)], ones, (2, 1)) for h in range(nh)], axis=1)
    on = o * lax.rsqrt(ms * (1.0 / HEAD_DIM_GDN) + NORM_EPS) * normw_ref[...]
    z = x[:, wq:]
    y_ref[0] = (on * (z * _sigmoid(z))).astype(y_ref.dtype)


def _gdn_mix(p3d, conv_w, a_log, dt_bias, norm_w, rows=256, chunk=128):
    bsz, t, _ = p3d.shape
    rows = min(rows, t)
    pad = jnp.zeros((LANES - N_HEADS_GDN,), F32)
    hp = jnp.stack([jnp.concatenate([a_log.astype(F32), pad]), jnp.concatenate([dt_bias.astype(F32), pad])])
    eye = jnp.eye(LANES, dtype=BF16)
    ones = jnp.ones((LANES, LANES), BF16)
    normw = jnp.tile(norm_w.astype(F32), N_HEADS_GDN).reshape(1, W_GDN)
    const = lambda shape: pl.BlockSpec(shape, lambda b, c: (0,) * len(shape))
    return pl.pallas_call(
        functools.partial(_gdn_kernel, rows=rows, chunk=chunk),
        grid=(bsz, t // rows),
        in_specs=[
            pl.BlockSpec((1, rows, 4 * W_GDN), lambda b, c: (b, c, COL_GDN // (4 * W_GDN))),
            pl.BlockSpec((1, rows, LANES), lambda b, c: (b, c, COL_GDN_AB // LANES)),
            const((CONV_WIDTH, 3 * W_GDN)), const((2, LANES)), const((1, W_GDN)),
            const((rows, rows)), const((LANES, LANES)), const((LANES, LANES)),
        ],
        out_specs=pl.BlockSpec((1, rows, W_GDN), lambda b, c: (b, c, 0)),
        out_shape=jax.ShapeDtypeStruct((bsz, t, W_GDN), BF16),
        scratch_shapes=[pltpu.VMEM((rows + 8, 3 * W_GDN), F32),
                        pltpu.VMEM((N_HEADS_GDN, HEAD_DIM_GDN, HEAD_DIM_GDN), F32),
                        pltpu.VMEM((rows, W_GDN), F32)],
        compiler_params=pltpu.CompilerParams(
            dimension_semantics=("parallel", "arbitrary"), vmem_limit_bytes=VMEM_LIMIT),
    )(p3d, p3d, conv_w.astype(F32), hp, normw, _block_tri(rows, chunk), eye, ones)


def _diff_attn_kernel(q_ref, k_ref, v_ref, qw_ref, kw_ref, gmat_ref, eye_ref, lam_ref, subw_ref, o_ref,
                      qt_ref, kn_ref, vt_ref, *, tq, tk, lambda_init):
    t = k_ref.shape[1]
    gmat = gmat_ref[...]
    eye = eye_ref[...]

    def normed(ref, wref, rows):
        x = ref[0, rows, :]
        ms = _mm(x * x, gmat, PREC["gsum"]) * (1.0 / HEAD_DIM_DIFF)
        return (x * lax.rsqrt(ms + NORM_EPS) * wref[...]).astype(BF16)

    for r0 in range(0, t, tk):
        rows = slice(r0, r0 + tk)
        kn_ref[rows, :] = normed(k_ref, kw_ref, rows)
        qt_ref[:, rows] = _mm(eye, normed(q_ref, qw_ref, rows), form="nt").astype(BF16)
        vt_ref[:, rows] = _mm(eye, v_ref[0, rows, :].astype(BF16), form="nt").astype(BF16)

    feat = lax.broadcasted_iota(jnp.int32, (LANES, 1), 0)
    lv = lam_ref[...]
    lam = (jnp.exp(jnp.sum(lv[0:1] * lv[1:2], axis=-1, keepdims=True))
           - jnp.exp(jnp.sum(lv[2:3] * lv[3:4], axis=-1, keepdims=True)) + lambda_init)

    tiles = []
    for i in range(t // tq):
        q_end = (i + 1) * tq
        for k0 in range(0, q_end, tk):
            klen = min(tk, q_end - k0)
            tiles.append((i, k0, klen, k0 + klen > i * tq))

    q_maps = {}

    def scores(i, k0, klen):
        if i not in q_maps:
            q_t = qt_ref[:, i * tq:(i + 1) * tq]
            zero = jnp.zeros_like(q_t)
            q_maps.clear()
            q_maps[i] = (jnp.where(feat < HEAD_DIM_DIFF, q_t, zero), jnp.where(feat >= HEAD_DIM_DIFF, q_t, zero))
        k = kn_ref[k0:k0 + klen, :]
        return [_dot(k, qm) for qm in q_maps[i]]

    state = None
    pending = scores(*tiles[0][:3])
    for n, (i, k0, klen, diag) in enumerate(tiles):
        cur = pending
        if n + 1 < len(tiles):
            pending = scores(*tiles[n + 1][:3])
        v_t = vt_ref[:, k0:k0 + klen]
        new_state = []
        for mp, s in enumerate(cur):
            if diag:
                key = lax.broadcasted_iota(jnp.int32, (klen, tq), 0)
                qry = lax.broadcasted_iota(jnp.int32, (klen, tq), 1)
                s = jnp.where(key - qry <= i * tq - k0, s, -jnp.inf)
            if k0 == 0:
                m_new = jnp.max(s, axis=0, keepdims=True)
                p = jnp.exp2(s - m_new)
                l_new = jnp.sum(p, axis=0, keepdims=True)
                acc_new = _dot(v_t, p.astype(BF16))
            else:
                m_prev, l_prev, acc_prev = state[mp]
                m_new = jnp.maximum(m_prev, jnp.max(s, axis=0, keepdims=True))
                alpha = jnp.exp2(m_prev - m_new)
                p = jnp.exp2(s - m_new)
                l_new = alpha * l_prev + jnp.sum(p, axis=0, keepdims=True)
                acc_new = alpha * acc_prev + _dot(v_t, p.astype(BF16))
            new_state.append((m_new, l_new, acc_new))
        state = new_state
        if k0 + klen == (i + 1) * tq:
            (_, l0, acc0), (_, l1, acc1) = state
            o_t = acc0 * (1.0 / l0) - lam * (acc1 * (1.0 / l1))
            ms = jnp.sum(o_t * o_t, axis=0, keepdims=True) * (1.0 / LANES)
            on_t = (o_t * lax.rsqrt(ms + NORM_EPS) * subw_ref[...] * (1.0 - lambda_init)).astype(BF16)
            o_ref[0, i * tq:(i + 1) * tq, :] = _mm(on_t, eye, form="tn").astype(o_ref.dtype)


def _diff_attn(p3d, q_norm_w, k_norm_w, lam_vecs, subln_w, lambda_init, tq=256, tk=512):
    bsz, t, _ = p3d.shape
    tk = min(tk, t)
    tq = min(tq, tk)
    lane = jnp.arange(LANES)
    gmat = ((lane[:, None] // HEAD_DIM_DIFF) == (lane[None, :] // HEAD_DIM_DIFF)).astype(BF16)
    qw = (jnp.tile(q_norm_w.astype(F32), 2) * (HEAD_DIM_DIFF ** -0.5 * math.log2(math.e))).reshape(1, LANES)
    kw = jnp.tile(k_norm_w.astype(F32), 2).reshape(1, LANES)
    base = COL_DIFF // LANES
    sec = lambda j: pl.BlockSpec((1, t, LANES), lambda b, h: (b, 0, base + j * N_HEADS_DIFF + h))
    const = lambda shape: pl.BlockSpec(shape, lambda b, h: (0,) * len(shape))
    return pl.pallas_call(
        functools.partial(_diff_attn_kernel, tq=tq, tk=tk, lambda_init=lambda_init),
        grid=(bsz, N_HEADS_DIFF),
        in_specs=[sec(0), sec(1), sec(2), const((1, LANES)), const((1, LANES)),
                  const((LANES, LANES)), const((LANES, LANES)), const((4, HEAD_DIM_DIFF)), const((LANES, 1))],
        out_specs=pl.BlockSpec((1, t, LANES), lambda b, h: (b, 0, h)),
        out_shape=jax.ShapeDtypeStruct((bsz, t, W_DIFF), BF16),
        scratch_shapes=[pltpu.VMEM((LANES, t), BF16), pltpu.VMEM((t, LANES), BF16), pltpu.VMEM((LANES, t), BF16)],
        compiler_params=pltpu.CompilerParams(
            dimension_semantics=("parallel", "parallel"), vmem_limit_bytes=VMEM_LIMIT),
    )(p3d, p3d, p3d, qw, kw, gmat, jnp.eye(LANES, dtype=BF16), lam_vecs, subln_w.reshape(LANES, 1).astype(F32))


def _mix_ffn_kernel(x_ref, yr_ref, yg_ref, yd_ref, w1_ref, w2_ref, w3_ref, nw_ref, wg_ref, wu_ref, wo_ref,
                    o_ref, h_ref):
    @pl.when(pl.program_id(1) == 0)
    def _():
        x1 = (x_ref[...] + _dot(yr_ref[...], w1_ref[...]) + _dot(yg_ref[...], w2_ref[...])
              + _dot(yd_ref[...], w3_ref[...]))
        ms = jnp.mean(x1 * x1, axis=-1, keepdims=True)
        h_ref[...] = (x1 * lax.rsqrt(ms + NORM_EPS) * nw_ref[...]).astype(BF16)
        o_ref[...] = x1

    h = h_ref[...]
    gate = _dot(h, wg_ref[...])
    up = _dot(h, wu_ref[...])
    act = (gate * _sigmoid(gate) * up).astype(BF16)
    o_ref[...] += _dot(act, wo_ref[...])


def _mix_ffn(x2d, y_rw, y_gdn, y_diff, w_mix_bf16, nw, w_in_bf16, w_out_bf16, tm=512, tf=512):
    n, d = x2d.shape
    hidden = w_out_bf16.shape[0]
    nf = hidden // tf
    rowblk = lambda w: pl.BlockSpec((tm, w), lambda i, f: (i, 0))
    resident = lambda rows, blk: pl.BlockSpec((rows, d), lambda i, f: (blk, 0), pipeline_mode=pl.Buffered(1))
    return pl.pallas_call(
        _mix_ffn_kernel,
        grid=(n // tm, nf),
        in_specs=[
            rowblk(d), rowblk(W_RWKV), rowblk(W_GDN), rowblk(W_DIFF),
            resident(W_RWKV, 0), resident(W_GDN, 1), resident(W_DIFF, 1),
            pl.BlockSpec((1, d), lambda i, f: (0, 0)),
            pl.BlockSpec((d, tf), lambda i, f: (0, f)),
            pl.BlockSpec((d, tf), lambda i, f: (0, nf + f)),
            pl.BlockSpec((tf, d), lambda i, f: (f, 0)),
        ],
        out_specs=pl.BlockSpec((tm, d), lambda i, f: (i, 0)),
        out_shape=jax.ShapeDtypeStruct((n, d), F32),
        scratch_shapes=[pltpu.VMEM((tm, d), BF16)],
        compiler_params=pltpu.CompilerParams(
            dimension_semantics=("parallel", "arbitrary"), vmem_limit_bytes=VMEM_LIMIT),
    )(x2d, y_rw, y_gdn, y_diff, w_mix_bf16, w_mix_bf16, w_mix_bf16, nw.reshape(1, d),
      w_in_bf16, w_in_bf16, w_out_bf16)


def _pack_in_proj(w_in_l, w_vres_l):
    d = w_in_l.shape[0]
    z = lambda n: jnp.zeros((d, n), w_in_l.dtype)
    gdn0 = RWKV_COLS
    vres = z(LANES) if w_vres_l is None else jnp.concatenate([w_vres_l, z(LANES - LORA_VRES)], axis=1)
    first = jnp.concatenate([w_in_l[:, :RWKV_COLS], vres,
                             w_in_l[:, gdn0 + 4 * W_GDN:gdn0 + GDN_COLS], z(LANES - 2 * N_HEADS_GDN)], axis=1)
    return (first.astype(BF16), w_in_l[:, gdn0:gdn0 + 4 * W_GDN].astype(BF16),
            w_in_l[:, gdn0 + GDN_COLS:].astype(BF16))


def kernel(x, attn_norm_w, w_in, w_vres_a, mu_rwkv, mu_vres, rwkv_w0, rwkv_w_lora_b, rwkv_a0, rwkv_a_lora_b, rwkv_g_lora_b, rwkv_v0, rwkv_v_lora_b, rwkv_k_k, rwkv_k_a, rwkv_r_k, rwkv_ln_w, rwkv_ln_b, gdn_conv_w, gdn_A_log, gdn_dt_bias, gdn_norm_w, diff_q_norm_w, diff_k_norm_w, diff_lambda_q1, diff_lambda_k1, diff_lambda_q2, diff_lambda_k2, diff_subln_w, w_out, ffn_norm_w, w_ffn_in, w_ffn_out):
    bsz, t, d = x.shape
    depth = w_in.shape[0]
    x2d = x.reshape(bsz * t, d)
    v_first = None
    for l in range(depth):
        w_proj = _pack_in_proj(w_in[l], None if l == 0 else w_vres_a[l - 1])
        mu_tail = jnp.zeros((RW_BLOCK - RWKV_COLS,), F32)
        if l > 0:
            mu_tail = mu_tail.at[:LORA_VRES].set(mu_vres[l - 1])
        mu_pad = jnp.concatenate([mu_rwkv[l], mu_tail]).reshape(1, RW_BLOCK)
        p3d = _norm_matmul(x2d, attn_norm_w[l], w_proj).reshape(bsz, t, N_PAD)
        rw_args = (rwkv_w0[l], rwkv_w_lora_b[l], rwkv_a0[l], rwkv_a_lora_b[l], rwkv_g_lora_b[l])
        rw_tail = (rwkv_k_k[l], rwkv_k_a[l], rwkv_r_k[l], rwkv_ln_w[l], rwkv_ln_b[l])
        if l == 0:
            y_rw, v_first = _rwkv_mix(p3d, None, mu_pad, *rw_args, None, None, *rw_tail)
        else:
            y_rw = _rwkv_mix(p3d, v_first, mu_pad, *rw_args, rwkv_v0[l - 1], rwkv_v_lora_b[l - 1], *rw_tail)
        y_gdn = _gdn_mix(p3d, gdn_conv_w[l], gdn_A_log[l], gdn_dt_bias[l], gdn_norm_w[l])
        lambda_init = 0.8 - 0.6 * math.exp(-0.3 * l)
        lam_vecs = jnp.stack([diff_lambda_q1[l], diff_lambda_k1[l], diff_lambda_q2[l], diff_lambda_k2[l]]).astype(F32)
        y_diff = _diff_attn(p3d, diff_q_norm_w[l], diff_k_norm_w[l], lam_vecs, diff_subln_w[l], lambda_init)
        n = bsz * t
        x2d = _mix_ffn(x2d, y_rw.reshape(n, W_RWKV), y_gdn.reshape(n, W_GDN), y_diff.reshape(n, W_DIFF),
                       w_out[l].astype(BF16), ffn_norm_w[l], w_ffn_in[l].astype(BF16), w_ffn_out[l].astype(BF16))
    return x2d.reshape(bsz, t, d)
```

```python
import functools
import math

import jax
import jax.numpy as jnp
from jax import lax
from jax.experimental import pallas as pl
from jax.experimental.pallas import tpu as pltpu

F32 = jnp.float32
BF16 = jnp.bfloat16

D_MODEL = 2048
W_RWKV = D_MODEL // 4
HEAD_DIM_RWKV = 64
LORA_DECAY = 64
LORA_ICLR = 64
LORA_VRES = 32
LORA_GATE = 128
RWKV_GN_EPS = 64e-5
W_GDN = D_MODEL // 4
HEAD_DIM_GDN = 128
N_HEADS_GDN = W_GDN // HEAD_DIM_GDN
CONV_WIDTH = 4
W_DIFF = D_MODEL // 2
HEAD_DIM_DIFF = 64
N_HEADS_DIFF = W_DIFF // (2 * HEAD_DIM_DIFF)
RWKV_COLS = 3 * W_RWKV + LORA_DECAY + LORA_ICLR + LORA_GATE
GDN_COLS = 4 * W_GDN + 2 * N_HEADS_GDN
DIFF_COLS = 3 * W_DIFF
N_IN = RWKV_COLS + GDN_COLS + DIFF_COLS
FFN_HIDDEN = -(-8 * D_MODEL // (3 * 256)) * 256
NORM_EPS = 1e-6

LANES = 128
SOLVE_BLOCK = 16
NORM_CHUNK = 256

COL_VRES = RWKV_COLS
COL_GDN_AB = COL_VRES + LANES
COL_GDN = COL_GDN_AB + LANES
COL_DIFF = COL_GDN + 4 * W_GDN
N_PAD = COL_DIFF + DIFF_COLS
RW_BLOCK = COL_GDN

VMEM_LIMIT = 56 * 1024 * 1024

PREC = {
    "lora": (1, 1),
    "ones": (1, 3),
    "gsum": (1, 1),
    "inv": (1, 1),
}


def _pieces(x, n):
    if isinstance(x, (list, tuple)):
        return list(x)[:n]
    if x.dtype == BF16:
        return [x]
    out, rem = [], x
    for i in range(n):
        piece = rem.astype(BF16)
        out.append(piece)
        if i + 1 < n:
            rem = rem - piece.astype(F32)
    return out


_DIMS = {"nn": (((1,), (0,)), ((), ())), "nt": (((1,), (1,)), ((), ())), "tn": (((0,), (0,)), ((), ()))}
_DIMS_B = {"nn": (((2,), (1,)), ((0,), (0,))), "nt": (((2,), (2,)), ((0,), (0,))), "tn": (((1,), (1,)), ((0,), (0,)))}


def _mm(a, b, prec=(1, 1), form="nn"):
    pa, pb = _pieces(a, prec[0]), _pieces(b, prec[1])
    dims = (_DIMS_B if pa[0].ndim == 3 else _DIMS)[form]
    depth = max(len(pa), len(pb))
    terms = sorted(((i, j) for i in range(len(pa)) for j in range(len(pb)) if i + j < depth),
                   key=lambda ij: -(ij[0] + ij[1]))
    acc = None
    for i, j in terms:
        d = lax.dot_general(pa[i], pb[j], dims, preferred_element_type=F32)
        acc = d if acc is None else acc + d
    return acc


def _dot(a, b):
    return jnp.dot(a, b, preferred_element_type=F32)


def _sigmoid(x):
    return 1.0 / (1.0 + jnp.exp(-x))


def _softplus(x):
    return jnp.maximum(x, 0.0) + jnp.log(1.0 + jnp.exp(-jnp.abs(x)))


def _split_bf16(w, n):
    out, rem = [], w.astype(F32)
    for _ in range(n):
        piece = rem.astype(BF16)
        out.append(piece)
        rem = rem - piece.astype(F32)
    return jnp.stack(out)


def _block_tri(rows, chunk):
    idx = jnp.arange(rows)
    same = (idx[:, None] // chunk) == (idx[None, :] // chunk)
    return (same & (idx[:, None] >= idx[None, :])).astype(BF16)


def _unit_lower_inverse(lo_tri, n_sub, prec):
    n = lo_tri.shape[-1]
    lead = (1,) * (lo_tri.ndim - 2)
    row = lax.broadcasted_iota(jnp.int32, lead + (n, n), lo_tri.ndim - 2)
    col = lax.broadcasted_iota(jnp.int32, lead + (n, n), lo_tri.ndim - 1)
    eye = (row == col).astype(F32)
    same = (row // SOLVE_BLOCK) == (col // SOLVE_BLOCK)
    l_diag = jnp.where(same, lo_tri, 0.0)
    l_off = jnp.where(same, 0.0, lo_tri)

    def neumann(a, s, order):
        width = 1
        while width < order:
            if 2 * width < order:
                both = _mm(a, jnp.concatenate([s, a], axis=-1), prec)
                s = s + both[..., :n]
                a = both[..., n:]
            else:
                s = s + _mm(a, s, prec)
            width *= 2
        return s

    t_diag = neumann(_mm(l_diag, l_diag, prec), eye + l_diag, SOLVE_BLOCK // 2)
    return neumann(_mm(t_diag, l_off, prec), t_diag, n_sub)


def _norm_matmul_kernel(x_ref, nw_ref, wa_ref, wg_ref, wd_ref, o_ref, h_ref, *, na, ng):
    j = pl.program_id(1)

    @pl.when(j == 0)
    def _():
        for r0 in range(0, x_ref.shape[0], NORM_CHUNK):
            rows = slice(r0, r0 + NORM_CHUNK)
            x = x_ref[rows, :]
            ms = jnp.mean(x * x, axis=-1, keepdims=True)
            h = (x * lax.rsqrt(ms + NORM_EPS) * nw_ref[...]).astype(BF16)
            h_ref[rows, :] = h
            o_ref[rows, :] = _dot(h, wa_ref[...])

    @pl.when((j > 0) & (j < na))
    def _():
        o_ref[...] = _dot(h_ref[...], wa_ref[...])

    @pl.when((j >= na) & (j < na + ng))
    def _():
        o_ref[...] = _dot(h_ref[...], wg_ref[...])

    @pl.when(j >= na + ng)
    def _():
        o_ref[...] = _dot(h_ref[...], wd_ref[...])


def _norm_matmul(x2d, nw, w_sections, tm=1024, tn=1024):
    n, d = x2d.shape
    tm = min(tm, n)
    wa, wg, wd = w_sections
    na, ng, nd = wa.shape[1] // tn, wg.shape[1] // tn, wd.shape[1] // tn
    return pl.pallas_call(
        functools.partial(_norm_matmul_kernel, na=na, ng=ng),
        grid=(n // tm, na + ng + nd),
        in_specs=[
            pl.BlockSpec((tm, d), lambda i, j: (i, 0)),
            pl.BlockSpec((1, d), lambda i, j: (0, 0)),
            pl.BlockSpec((d, tn), lambda i, j: (0, jnp.minimum(j, na - 1))),
            pl.BlockSpec((d, tn), lambda i, j: (0, jnp.clip(j - na, 0, ng - 1))),
            pl.BlockSpec((d, tn), lambda i, j: (0, jnp.maximum(j - na - ng, 0))),
        ],
        out_specs=pl.BlockSpec((tm, tn), lambda i, j: (i, j)),
        out_shape=jax.ShapeDtypeStruct((n, (na + ng + nd) * tn), F32),
        scratch_shapes=[pltpu.VMEM((tm, d), BF16)],
        compiler_params=pltpu.CompilerParams(
            dimension_semantics=("parallel", "arbitrary"), vmem_limit_bytes=VMEM_LIMIT),
    )(x2d, nw.reshape(1, d), wa, wg, wd)


def _rwkv_kernel(*refs, rows, chunk, has_vres):
    if has_vres:
        (p_ref, vf_ref, mu_ref, w0_ref, wbw_ref, a0_ref, wba_ref, wbg_ref, v0_ref, wbv_ref,
         kk_ref, ka_ref, rk_ref, lnw_ref, lnb_ref, gmat_ref, tri_ref,
         y_ref, buf_ref, s_ref, ybuf_ref) = refs
    else:
        (p_ref, mu_ref, w0_ref, wbw_ref, a0_ref, wba_ref, wbg_ref,
         kk_ref, ka_ref, rk_ref, lnw_ref, lnb_ref, gmat_ref, tri_ref,
         y_ref, vout_ref, buf_ref, s_ref, ybuf_ref) = refs
    c = chunk
    ns = rows // c
    n_pairs = W_RWKV // LANES
    p_lora, p_ones, p_gsum, p_inv = PREC["lora"], PREC["ones"], PREC["gsum"], PREC["inv"]
    wpieces = lambda ref: [ref[i] for i in range(ref.shape[0])]
    pair = lambda pi: slice(pi * LANES, (pi + 1) * LANES)
    sub = lambda j: slice(j * c, (j + 1) * c)

    @pl.when(pl.program_id(1) == 0)
    def _():
        buf_ref[0:8, :] = jnp.zeros((8, RW_BLOCK), F32)
        s_ref[...] = jnp.zeros_like(s_ref)

    x = p_ref[0]
    buf_ref[8:8 + rows, :] = x
    prev = buf_ref[7:7 + rows, :]
    buf_ref[0:8, :] = x[rows - 8:rows, :]
    pm = x + (prev - x) * mu_ref[...]

    r = pm[:, 0:W_RWKV]
    k = pm[:, W_RWKV:2 * W_RWKV]
    v = pm[:, 2 * W_RWKV:3 * W_RWKV]
    lw = pm[:, 3 * W_RWKV:3 * W_RWKV + LANES]
    xg = pm[:, 3 * W_RWKV + LANES:3 * W_RWKV + 2 * LANES]
    w_log = -_softplus(-(w0_ref[...] + _mm(jnp.tanh(lw), wpieces(wbw_ref), p_lora))) - 0.5
    logd = -jnp.exp(w_log)
    a = _sigmoid(a0_ref[...] + _mm(lw, wpieces(wba_ref), p_lora))
    g = _mm(_sigmoid(xg), wpieces(wbg_ref), p_lora)
    if has_vres:
        xv = pm[:, COL_VRES:COL_VRES + LANES]
        v = v + (vf_ref[0] - v) * _sigmoid(v0_ref[...] + _mm(xv, wpieces(wbv_ref), p_lora))
    else:
        vout_ref[0] = v

    gmat = gmat_ref[...]

    def gsum(z):
        return jnp.concatenate([_mm(z[:, pair(pi)], gmat, p_gsum) for pi in range(n_pairs)], axis=1)

    kk = k * kk_ref[...]
    kk = kk * lax.rsqrt(gsum(kk * kk) + NORM_EPS)
    k2 = k * (1.0 + (a - 1.0) * ka_ref[...])
    b = kk * a
    gc = _mm(tri_ref[...], logd, p_ones)
    g_end = [gc[j * c + c - 1:j * c + c, :] for j in range(ns)]
    e_end = jnp.exp(jnp.concatenate([g_end[j] - gc[sub(j), :] for j in range(ns)], axis=0))
    e_inv = jnp.exp(-gc)

    lane = lax.broadcasted_iota(jnp.int32, (1, W_RWKV), 1)
    h0 = ((lane // HEAD_DIM_RWKV) % 2 == 0).astype(F32)
    h1 = 1.0 - h0

    def stack_all(z):
        z0 = (z * h0).astype(BF16)
        z1 = (z * h1).astype(BF16)
        return jnp.stack([jnp.concatenate([z0[sub(j), pair(pi)], z1[sub(j), pair(pi)]], axis=0)
                          for j in range(ns) for pi in range(n_pairs)])

    at = stack_all(-kk * jnp.exp(gc - logd))
    rt = stack_all(r * jnp.exp(gc))
    bt = stack_all(b * e_inv)
    kt = stack_all(k2 * e_inv)
    vst = stack_all(v)
    bh = stack_all(b * e_end)
    kh = stack_all(k2 * e_end)

    n2 = 2 * c
    quad = _mm(jnp.concatenate([at, rt], axis=1), jnp.concatenate([bt, kt], axis=1), form="nt")
    row = lax.broadcasted_iota(jnp.int32, (1, n2, n2), 1) % c
    col = lax.broadcasted_iota(jnp.int32, (1, n2, n2), 2) % c
    strict = row > col
    incl = row >= col
    l_ab = jnp.where(strict, quad[:, :n2, :n2], 0.0)
    a_rb = jnp.where(incl, quad[:, n2:, :n2], 0.0).astype(BF16)
    vmask = jnp.concatenate([jnp.broadcast_to(strict, (1, n2, n2)), jnp.broadcast_to(incl, (1, n2, n2))], axis=1)
    own = _mm(jnp.where(vmask, quad[:, :, n2:], 0.0).astype(BF16), vst)
    y0 = own[:, n2:]
    t_inv = _unit_lower_inverse(l_ab, c // SOLVE_BLOCK, p_inv).astype(BF16)
    t_both = _mm(t_inv, jnp.concatenate([at, own[:, :n2].astype(BF16)], axis=-1))
    u0 = t_both[:, :, LANES:]
    wr = jnp.concatenate([t_both[:, :, :LANES].astype(BF16), rt], axis=1)
    kv = _mm(vst, kh, form="tn")

    s = s_ref[...]
    for j in range(ns):
        gs = slice(j * n_pairs, (j + 1) * n_pairs)
        from_s = _mm(wr[gs], s.astype(BF16), form="nt")
        u = u0[gs] + from_s[:, :n2]
        u_b = u.astype(BF16)
        y_st = from_s[:, n2:] + _mm(a_rb[gs], u_b) + y0[gs]
        d_end = jnp.exp(jnp.stack([g_end[j][:, pair(pi)] for pi in range(n_pairs)]))
        s = s * d_end + _mm(u_b, bh[gs], form="tn") + kv[gs]
        y = y_st[:, :c] + y_st[:, c:]
        for pi in range(n_pairs):
            ybuf_ref[sub(j), pair(pi)] = y[pi]
    s_ref[...] = s

    y = ybuf_ref[...]
    inv_n = 1.0 / HEAD_DIM_RWKV
    mean = gsum(y) * inv_n
    dlt = y - mean
    var = gsum(dlt * dlt) * inv_n
    yn = dlt * lax.rsqrt(var + RWKV_GN_EPS) * lnw_ref[...] + lnb_ref[...]
    bonus = gsum(r * k2 * rk_ref[...]) * v
    y_ref[0] = ((yn + bonus) * g).astype(y_ref.dtype)


def _rwkv_mix(p3d, v_first, mu_pad, w0, wbw, a0, wba, wbg, v0, wbv, k_k, k_a, r_k, ln_w, ln_b, rows=256, chunk=64):
    bsz, t, _ = p3d.shape
    rows = min(rows, t)
    has_vres = v_first is not None
    n_lora = PREC["lora"][1]
    row = lambda z: z.reshape(1, -1).astype(F32)
    lane = jnp.arange(LANES)
    gmat = ((lane[:, None] // HEAD_DIM_RWKV) == (lane[None, :] // HEAD_DIM_RWKV)).astype(BF16)
    zeros = jnp.zeros((LORA_DECAY, W_RWKV), F32)
    wbw_pad = _split_bf16(jnp.concatenate([wbw, zeros], axis=0), n_lora)
    wba_pad = _split_bf16(jnp.concatenate([zeros, wba], axis=0), n_lora)
    const = lambda shape: pl.BlockSpec(shape, lambda b, c: (0,) * len(shape))
    lora_spec = const((n_lora, LANES, W_RWKV))
    p_spec = pl.BlockSpec((1, rows, RW_BLOCK), lambda b, c: (b, c, 0))
    seq_spec = pl.BlockSpec((1, rows, W_RWKV), lambda b, c: (b, c, 0))
    args = [p3d]
    specs = [p_spec]
    if has_vres:
        args.append(v_first)
        specs.append(seq_spec)
    args += [mu_pad, row(w0), wbw_pad, row(a0), wba_pad, _split_bf16(wbg, n_lora)]
    specs += [const((1, RW_BLOCK)), const((1, W_RWKV)), lora_spec, const((1, W_RWKV)), lora_spec, lora_spec]
    if has_vres:
        wbv_pad = jnp.concatenate([wbv, jnp.zeros((LANES - LORA_VRES, W_RWKV), F32)], axis=0)
        args += [row(v0), _split_bf16(wbv_pad, n_lora)]
        specs += [const((1, W_RWKV)), lora_spec]
    args += [row(k_k), row(k_a), row(r_k), row(ln_w), row(ln_b), gmat, _block_tri(rows, chunk)]
    specs += [const((1, W_RWKV))] * 5 + [const((LANES, LANES)), const((rows, rows))]
    y_shape = jax.ShapeDtypeStruct((bsz, t, W_RWKV), BF16)
    if has_vres:
        out_shape, out_specs = y_shape, seq_spec
    else:
        out_shape = (y_shape, jax.ShapeDtypeStruct((bsz, t, W_RWKV), F32))
        out_specs = (seq_spec, seq_spec)
    return pl.pallas_call(
        functools.partial(_rwkv_kernel, rows=rows, chunk=chunk, has_vres=has_vres),
        grid=(bsz, t // rows),
        in_specs=specs,
        out_specs=out_specs,
        out_shape=out_shape,
        scratch_shapes=[pltpu.VMEM((rows + 8, RW_BLOCK), F32),
                        pltpu.VMEM((W_RWKV // LANES, LANES, LANES), F32),
                        pltpu.VMEM((rows, W_RWKV), F32)],
        compiler_params=pltpu.CompilerParams(
            dimension_semantics=("parallel", "arbitrary"), vmem_limit_bytes=VMEM_LIMIT),
    )(*args)


def _gdn_kernel(x_ref, ab_ref, convw_ref, hp_ref, normw_ref, tri_ref, eye_ref, ones_ref,
                y_ref, buf_ref, s_ref, obuf_ref, *, rows, chunk):
    c = chunk
    ns = rows // c
    nh = N_HEADS_GDN
    wq = 3 * W_GDN
    p_ones, p_inv = PREC["ones"], PREC["inv"]
    head = lambda h: slice(h * LANES, (h + 1) * LANES)
    sub = lambda j: slice(j * c, (j + 1) * c)

    @pl.when(pl.program_id(1) == 0)
    def _():
        buf_ref[0:8, :] = jnp.zeros((8, wq), F32)
        s_ref[...] = jnp.zeros_like(s_ref)

    x = x_ref[0]
    xc = x[:, :wq]
    buf_ref[8:8 + rows, :] = xc
    conv = xc * convw_ref[3:4, :]
    for i in range(CONV_WIDTH - 1):
        conv = conv + buf_ref[5 + i:5 + i + rows, :] * convw_ref[i:i + 1, :]
    buf_ref[0:8, :] = xc[rows - 8:rows, :]
    qkv = conv * _sigmoid(conv)

    ab = ab_ref[0]
    g_all = -jnp.exp(hp_ref[0:1, :]) * _softplus(ab + hp_ref[1:2, :])
    beta_all = _sigmoid(ab)
    gc_all = _mm(tri_ref[...], g_all, p_ones)
    gc_t = _mm(eye_ref[...], gc_all, p_ones, "nt")
    ones = ones_ref[...]

    def l2n(z):
        return z * lax.rsqrt(_mm(z * z, ones, (2, 1)) + NORM_EPS)

    chains = [(j, h) for j in range(ns) for h in range(nh)]
    k_l, kb_l, q_l, vk_l, qe_l, ke_l, gcol_l, grow_l, gend_l = [], [], [], [], [], [], [], [], []
    for h in range(nh):
        q = l2n(qkv[:, head(h)]) * (HEAD_DIM_GDN ** -0.5)
        k = l2n(qkv[:, W_GDN + h * LANES:W_GDN + (h + 1) * LANES])
        v = qkv[:, 2 * W_GDN + h * LANES:2 * W_GDN + (h + 1) * LANES]
        gcol = gc_all[:, h:h + 1]
        beta = beta_all[:, nh + h:nh + h + 1]
        g_end = [gcol[j * c + c - 1:j * c + c, :] for j in range(ns)]
        e_gc = jnp.exp(gcol)
        e_end = jnp.exp(jnp.concatenate([g_end[j] - gcol[sub(j), :] for j in range(ns)], axis=0))
        kb = k * beta
        k_l.append(k.astype(BF16))
        kb_l.append(kb.astype(BF16))
        q_l.append(q.astype(BF16))
        vk_l.append(jnp.concatenate([v * beta, kb * e_gc], axis=1).astype(BF16))
        qe_l.append((q * e_gc).astype(BF16))
        ke_l.append((k * e_end).astype(BF16))
        gcol_l.append(gcol)
        grow_l.append(gc_t[h:h + 1, :])
        gend_l.append(g_end)
    gather = lambda lst: jnp.stack([lst[h][sub(j), :] for j, h in chains])
    k_b, kb_b, q_b, vk_b, qe_b, ke_b = (gather(l) for l in (k_l, kb_l, q_l, vk_l, qe_l, ke_l))
    gcol_b = gather(gcol_l)
    grow_b = jnp.stack([grow_l[h][:, sub(j)] for j, h in chains])

    row = lax.broadcasted_iota(jnp.int32, (1, c, c), 1)
    col = lax.broadcasted_iota(jnp.int32, (1, c, c), 2)
    decay = jnp.exp(jnp.where(row >= col, gcol_b - grow_b, -jnp.inf))
    both = _mm(jnp.concatenate([kb_b, q_b], axis=1), k_b, form="nt")
    kkt, qkt = both[:, :c], both[:, c:]
    t_inv = _unit_lower_inverse(-jnp.where(row > col, kkt * decay, 0.0), c // SOLVE_BLOCK, p_inv).astype(BF16)
    uw = _mm(t_inv, vk_b)
    u = uw[:, :, :LANES]
    wq_b = jnp.concatenate([uw[:, :, LANES:].astype(BF16), qe_b], axis=1)
    attn = (qkt * decay).astype(BF16)

    s = s_ref[...]
    for j in range(ns):
        gs = slice(j * nh, (j + 1) * nh)
        ws_qs = _mm(wq_b[gs], s.astype(BF16))
        v_new = (u[gs] - ws_qs[:, :c]).astype(BF16)
        o = ws_qs[:, c:] + _mm(attn[gs], v_new)
        d_end = jnp.exp(jnp.stack([gend_l[h][j] for h in range(nh)]))
        s = s * d_end + _mm(ke_b[gs], v_new, form="tn")
        for h in range(nh):
            obuf_ref[sub(j), head(h)] = o[h]
    s_ref[...] = s

    o = obuf_ref[...]
    ms = jnp.concatenate([_mm(o[:, head(h)] * o[:, head(h)], ones, (2, 1)) for h in range(nh)], axis=1)
    on = o * lax.rsqrt(ms * (1.0 / HEAD_DIM_GDN) + NORM_EPS) * normw_ref[...]
    z = x[:, wq:]
    y_ref[0] = (on * (z * _sigmoid(z))).astype(y_ref.dtype)


def _gdn_mix(p3d, conv_w, a_log, dt_bias, norm_w, rows=256, chunk=128):
    bsz, t, _ = p3d.shape
    rows = min(rows, t)
    pad = jnp.zeros((LANES - N_HEADS_GDN,), F32)
    hp = jnp.stack([jnp.concatenate([a_log.astype(F32), pad]), jnp.concatenate([dt_bias.astype(F32), pad])])
    eye = jnp.eye(LANES, dtype=BF16)
    ones = jnp.ones((LANES, LANES), BF16)
    normw = jnp.tile(norm_w.astype(F32), N_HEADS_GDN).reshape(1, W_GDN)
    const = lambda shape: pl.BlockSpec(shape, lambda b, c: (0,) * len(shape))
    return pl.pallas_call(
        functools.partial(_gdn_kernel, rows=rows, chunk=chunk),
        grid=(bsz, t // rows),
        in_specs=[
            pl.BlockSpec((1, rows, 4 * W_GDN), lambda b, c: (b, c, COL_GDN // (4 * W_GDN))),
            pl.BlockSpec((1, rows, LANES), lambda b, c: (b, c, COL_GDN_AB // LANES)),
            const((CONV_WIDTH, 3 * W_GDN)), const((2, LANES)), const((1, W_GDN)),
            const((rows, rows)), const((LANES, LANES)), const((LANES, LANES)),
        ],
        out_specs=pl.BlockSpec((1, rows, W_GDN), lambda b, c: (b, c, 0)),
        out_shape=jax.ShapeDtypeStruct((bsz, t, W_GDN), BF16),
        scratch_shapes=[pltpu.VMEM((rows + 8, 3 * W_GDN), F32),
                        pltpu.VMEM((N_HEADS_GDN, HEAD_DIM_GDN, HEAD_DIM_GDN), F32),
                        pltpu.VMEM((rows, W_GDN), F32)],
        compiler_params=pltpu.CompilerParams(
            dimension_semantics=("parallel", "arbitrary"), vmem_limit_bytes=VMEM_LIMIT),
    )(p3d, p3d, conv_w.astype(F32), hp, normw, _block_tri(rows, chunk), eye, ones)


def _diff_attn_kernel(q_ref, k_ref, v_ref, qw_ref, kw_ref, gmat_ref, eye_ref, lam_ref, subw_ref, o_ref,
                      qt_ref, kn_ref, vt_ref, *, tq, tk, lambda_init):
    t = k_ref.shape[1]
    gmat = gmat_ref[...]
    eye = eye_ref[...]

    def normed(ref, wref, rows):
        x = ref[0, rows, :]
        ms = _mm(x * x, gmat, PREC["gsum"]) * (1.0 / HEAD_DIM_DIFF)
        return (x * lax.rsqrt(ms + NORM_EPS) * wref[...]).astype(BF16)

    for r0 in range(0, t, tk):
        rows = slice(r0, r0 + tk)
        kn_ref[rows, :] = normed(k_ref, kw_ref, rows)
        qt_ref[:, rows] = _mm(eye, normed(q_ref, qw_ref, rows), form="nt").astype(BF16)
        vt_ref[:, rows] = _mm(eye, v_ref[0, rows, :].astype(BF16), form="nt").astype(BF16)

    feat = lax.broadcasted_iota(jnp.int32, (LANES, 1), 0)
    lv = lam_ref[...]
    lam = (jnp.exp(jnp.sum(lv[0:1] * lv[1:2], axis=-1, keepdims=True))
           - jnp.exp(jnp.sum(lv[2:3] * lv[3:4], axis=-1, keepdims=True)) + lambda_init)

    tiles = []
    for i in range(t // tq):
        q_end = (i + 1) * tq
        for k0 in range(0, q_end, tk):
            klen = min(tk, q_end - k0)
            tiles.append((i, k0, klen, k0 + klen > i * tq))

    q_maps = {}

    def scores(i, k0, klen):
        if i not in q_maps:
            q_t = qt_ref[:, i * tq:(i + 1) * tq]
            zero = jnp.zeros_like(q_t)
            q_maps.clear()
            q_maps[i] = (jnp.where(feat < HEAD_DIM_DIFF, q_t, zero), jnp.where(feat >= HEAD_DIM_DIFF, q_t, zero))
        k = kn_ref[k0:k0 + klen, :]
        return [_dot(k, qm) for qm in q_maps[i]]

    state = None
    pending = scores(*tiles[0][:3])
    for n, (i, k0, klen, diag) in enumerate(tiles):
        cur = pending
        if n + 1 < len(tiles):
            pending = scores(*tiles[n + 1][:3])
        v_t = vt_ref[:, k0:k0 + klen]
        new_state = []
        for mp, s in enumerate(cur):
            if diag:
                key = lax.broadcasted_iota(jnp.int32, (klen, tq), 0)
                qry = lax.broadcasted_iota(jnp.int32, (klen, tq), 1)
                s = jnp.where(key - qry <= i * tq - k0, s, -jnp.inf)
            if k0 == 0:
                m_new = jnp.max(s, axis=0, keepdims=True)
                p = jnp.exp2(s - m_new)
                l_new = jnp.sum(p, axis=0, keepdims=True)
                acc_new = _dot(v_t, p.astype(BF16))
            else:
                m_prev, l_prev, acc_prev = state[mp]
                m_new = jnp.maximum(m_prev, jnp.max(s, axis=0, keepdims=True))
                alpha = jnp.exp2(m_prev - m_new)
                p = jnp.exp2(s - m_new)
                l_new = alpha * l_prev + jnp.sum(p, axis=0, keepdims=True)
                acc_new = alpha * acc_prev + _dot(v_t, p.astype(BF16))
            new_state.append((m_new, l_new, acc_new))
        state = new_state
        if k0 + klen == (i + 1) * tq:
            (_, l0, acc0), (_, l1, acc1) = state
            o_t = acc0 * (1.0 / l0) - lam * (acc1 * (1.0 / l1))
            ms = jnp.sum(o_t * o_t, axis=0, keepdims=True) * (1.0 / LANES)
            on_t = (o_t * lax.rsqrt(ms + NORM_EPS) * subw_ref[...] * (1.0 - lambda_init)).astype(BF16)
            o_ref[0, i * tq:(i + 1) * tq, :] = _mm(on_t, eye, form="tn").astype(o_ref.dtype)


def _diff_attn(p3d, q_norm_w, k_norm_w, lam_vecs, subln_w, lambda_init, tq=512, tk=1024):
    bsz, t, _ = p3d.shape
    tk = min(tk, t)
    tq = min(tq, tk)
    lane = jnp.arange(LANES)
    gmat = ((lane[:, None] // HEAD_DIM_DIFF) == (lane[None, :] // HEAD_DIM_DIFF)).astype(BF16)
    qw = (jnp.tile(q_norm_w.astype(F32), 2) * (HEAD_DIM_DIFF ** -0.5 * math.log2(math.e))).reshape(1, LANES)
    kw = jnp.tile(k_norm_w.astype(F32), 2).reshape(1, LANES)
    base = COL_DIFF // LANES
    sec = lambda j: pl.BlockSpec((1, t, LANES), lambda b, h: (b, 0, base + j * N_HEADS_DIFF + h))
    const = lambda shape: pl.BlockSpec(shape, lambda b, h: (0,) * len(shape))
    return pl.pallas_call(
        functools.partial(_diff_attn_kernel, tq=tq, tk=tk, lambda_init=lambda_init),
        grid=(bsz, N_HEADS_DIFF),
        in_specs=[sec(0), sec(1), sec(2), const((1, LANES)), const((1, LANES)),
                  const((LANES, LANES)), const((LANES, LANES)), const((4, HEAD_DIM_DIFF)), const((LANES, 1))],
        out_specs=pl.BlockSpec((1, t, LANES), lambda b, h: (b, 0, h)),
        out_shape=jax.ShapeDtypeStruct((bsz, t, W_DIFF), BF16),
        scratch_shapes=[pltpu.VMEM((LANES, t), BF16), pltpu.VMEM((t, LANES), BF16), pltpu.VMEM((LANES, t), BF16)],
        compiler_params=pltpu.CompilerParams(
            dimension_semantics=("parallel", "parallel"), vmem_limit_bytes=VMEM_LIMIT),
    )(p3d, p3d, p3d, qw, kw, gmat, jnp.eye(LANES, dtype=BF16), lam_vecs, subln_w.reshape(LANES, 1).astype(F32))


def _mix_ffn_kernel(x_ref, yr_ref, yg_ref, yd_ref, w1_ref, w2_ref, w3_ref, nw_ref, wg_ref, wu_ref, wo_ref,
                    o_ref, h_ref):
    @pl.when(pl.program_id(1) == 0)
    def _():
        x1 = (x_ref[...] + _dot(yr_ref[...], w1_ref[...]) + _dot(yg_ref[...], w2_ref[...])
              + _dot(yd_ref[...], w3_ref[...]))
        ms = jnp.mean(x1 * x1, axis=-1, keepdims=True)
        h_ref[...] = (x1 * lax.rsqrt(ms + NORM_EPS) * nw_ref[...]).astype(BF16)
        o_ref[...] = x1

    h = h_ref[...]
    gate = _dot(h, wg_ref[...])
    up = _dot(h, wu_ref[...])
    act = (gate * _sigmoid(gate) * up).astype(BF16)
    o_ref[...] += _dot(act, wo_ref[...])


def _mix_ffn(x2d, y_rw, y_gdn, y_diff, w_mix_bf16, nw, w_in_bf16, w_out_bf16, tm=512, tf=512):
    n, d = x2d.shape
    hidden = w_out_bf16.shape[0]
    nf = hidden // tf
    rowblk = lambda w: pl.BlockSpec((tm, w), lambda i, f: (i, 0))
    resident = lambda rows, blk: pl.BlockSpec((rows, d), lambda i, f: (blk, 0), pipeline_mode=pl.Buffered(1))
    return pl.pallas_call(
        _mix_ffn_kernel,
        grid=(n // tm, nf),
        in_specs=[
            rowblk(d), rowblk(W_RWKV), rowblk(W_GDN), rowblk(W_DIFF),
            resident(W_RWKV, 0), resident(W_GDN, 1), resident(W_DIFF, 1),
            pl.BlockSpec((1, d), lambda i, f: (0, 0)),
            pl.BlockSpec((d, tf), lambda i, f: (0, f)),
            pl.BlockSpec((d, tf), lambda i, f: (0, nf + f)),
            pl.BlockSpec((tf, d), lambda i, f: (f, 0)),
        ],
        out_specs=pl.BlockSpec((tm, d), lambda i, f: (i, 0)),
        out_shape=jax.ShapeDtypeStruct((n, d), F32),
        scratch_shapes=[pltpu.VMEM((tm, d), BF16)],
        compiler_params=pltpu.CompilerParams(
            dimension_semantics=("parallel", "arbitrary"), vmem_limit_bytes=VMEM_LIMIT),
    )(x2d, y_rw, y_gdn, y_diff, w_mix_bf16, w_mix_bf16, w_mix_bf16, nw.reshape(1, d),
      w_in_bf16, w_in_bf16, w_out_bf16)


def _pack_in_proj(w_in_l, w_vres_l):
    d = w_in_l.shape[0]
    z = lambda n: jnp.zeros((d, n), w_in_l.dtype)
    gdn0 = RWKV_COLS
    vres = z(LANES) if w_vres_l is None else jnp.concatenate([w_vres_l, z(LANES - LORA_VRES)], axis=1)
    first = jnp.concatenate([w_in_l[:, :RWKV_COLS], vres,
                             w_in_l[:, gdn0 + 4 * W_GDN:gdn0 + GDN_COLS], z(LANES - 2 * N_HEADS_GDN)], axis=1)
    return (first.astype(BF16), w_in_l[:, gdn0:gdn0 + 4 * W_GDN].astype(BF16),
            w_in_l[:, gdn0 + GDN_COLS:].astype(BF16))


def kernel(x, attn_norm_w, w_in, w_vres_a, mu_rwkv, mu_vres, rwkv_w0, rwkv_w_lora_b, rwkv_a0, rwkv_a_lora_b, rwkv_g_lora_b, rwkv_v0, rwkv_v_lora_b, rwkv_k_k, rwkv_k_a, rwkv_r_k, rwkv_ln_w, rwkv_ln_b, gdn_conv_w, gdn_A_log, gdn_dt_bias, gdn_norm_w, diff_q_norm_w, diff_k_norm_w, diff_lambda_q1, diff_lambda_k1, diff_lambda_q2, diff_lambda_k2, diff_subln_w, w_out, ffn_norm_w, w_ffn_in, w_ffn_out):
    bsz, t, d = x.shape
    depth = w_in.shape[0]
    x2d = x.reshape(bsz * t, d)
    v_first = None
    for l in range(depth):
        w_proj = _pack_in_proj(w_in[l], None if l == 0 else w_vres_a[l - 1])
        mu_tail = jnp.zeros((RW_BLOCK - RWKV_COLS,), F32)
        if l > 0:
            mu_tail = mu_tail.at[:LORA_VRES].set(mu_vres[l - 1])
        mu_pad = jnp.concatenate([mu_rwkv[l], mu_tail]).reshape(1, RW_BLOCK)
        p3d = _norm_matmul(x2d, attn_norm_w[l], w_proj).reshape(bsz, t, N_PAD)
        rw_args = (rwkv_w0[l], rwkv_w_lora_b[l], rwkv_a0[l], rwkv_a_lora_b[l], rwkv_g_lora_b[l])
        rw_tail = (rwkv_k_k[l], rwkv_k_a[l], rwkv_r_k[l], rwkv_ln_w[l], rwkv_ln_b[l])
        if l == 0:
            y_rw, v_first = _rwkv_mix(p3d, None, mu_pad, *rw_args, None, None, *rw_tail)
        else:
            y_rw = _rwkv_mix(p3d, v_first, mu_pad, *rw_args, rwkv_v0[l - 1], rwkv_v_lora_b[l - 1], *rw_tail)
        y_gdn = _gdn_mix(p3d, gdn_conv_w[l], gdn_A_log[l], gdn_dt_bias[l], gdn_norm_w[l])
        lambda_init = 0.8 - 0.6 * math.exp(-0.3 * l)
        lam_vecs = jnp.stack([diff_lambda_q1[l], diff_lambda_k1[l], diff_lambda_q2[l], diff_lambda_k2[l]]).astype(F32)
        y_diff = _diff_attn(p3d, diff_q_norm_w[l], diff_k_norm_w[l], lam_vecs, diff_subln_w[l], lambda_init)
        n = bsz * t
        x2d = _mix_ffn(x2d, y_rw.reshape(n, W_RWKV), y_gdn.reshape(n, W_GDN), y_diff.reshape(n, W_DIFF),
                       w_out[l].astype(BF16), ffn_norm_w[l], w_ffn_in[l].astype(BF16), w_ffn_out[l].astype(BF16))
    return x2d.reshape(bsz, t, d)
```

```python
import functools
import math

import jax
import jax.numpy as jnp
from jax import lax
from jax.experimental import pallas as pl
from jax.experimental.pallas import tpu as pltpu

F32 = jnp.float32
BF16 = jnp.bfloat16

D_MODEL = 2048
W_RWKV = D_MODEL // 4
HEAD_DIM_RWKV = 64
LORA_DECAY = 64
LORA_ICLR = 64
LORA_VRES = 32
LORA_GATE = 128
RWKV_GN_EPS = 64e-5
W_GDN = D_MODEL // 4
HEAD_DIM_GDN = 128
N_HEADS_GDN = W_GDN // HEAD_DIM_GDN
CONV_WIDTH = 4
W_DIFF = D_MODEL // 2
HEAD_DIM_DIFF = 64
N_HEADS_DIFF = W_DIFF // (2 * HEAD_DIM_DIFF)
RWKV_COLS = 3 * W_RWKV + LORA_DECAY + LORA_ICLR + LORA_GATE
GDN_COLS = 4 * W_GDN + 2 * N_HEADS_GDN
DIFF_COLS = 3 * W_DIFF
N_IN = RWKV_COLS + GDN_COLS + DIFF_COLS
FFN_HIDDEN = -(-8 * D_MODEL // (3 * 256)) * 256
NORM_EPS = 1e-6

LANES = 128
SOLVE_BLOCK = 16
NORM_CHUNK = 256

COL_VRES = RWKV_COLS
COL_GDN_AB = COL_VRES + LANES
COL_GDN = COL_GDN_AB + LANES
COL_DIFF = COL_GDN + 4 * W_GDN
N_PAD = COL_DIFF + DIFF_COLS
RW_BLOCK = COL_GDN

VMEM_LIMIT = 56 * 1024 * 1024

PREC = {
    "lora": (1, 1),
    "ones": (1, 3),
    "gsum": (1, 1),
    "inv": (1, 1),
}


def _pieces(x, n):
    if isinstance(x, (list, tuple)):
        return list(x)[:n]
    if x.dtype == BF16:
        return [x]
    out, rem = [], x
    for i in range(n):
        piece = rem.astype(BF16)
        out.append(piece)
        if i + 1 < n:
            rem = rem - piece.astype(F32)
    return out


_DIMS = {"nn": (((1,), (0,)), ((), ())), "nt": (((1,), (1,)), ((), ())), "tn": (((0,), (0,)), ((), ()))}
_DIMS_B = {"nn": (((2,), (1,)), ((0,), (0,))), "nt": (((2,), (2,)), ((0,), (0,))), "tn": (((1,), (1,)), ((0,), (0,)))}


def _mm(a, b, prec=(1, 1), form="nn"):
    pa, pb = _pieces(a, prec[0]), _pieces(b, prec[1])
    dims = (_DIMS_B if pa[0].ndim == 3 else _DIMS)[form]
    depth = max(len(pa), len(pb))
    terms = sorted(((i, j) for i in range(len(pa)) for j in range(len(pb)) if i + j < depth),
                   key=lambda ij: -(ij[0] + ij[1]))
    acc = None
    for i, j in terms:
        d = lax.dot_general(pa[i], pb[j], dims, preferred_element_type=F32)
        acc = d if acc is None else acc + d
    return acc


def _dot(a, b):
    return jnp.dot(a, b, preferred_element_type=F32)


def _sigmoid(x):
    return 1.0 / (1.0 + jnp.exp(-x))


def _softplus(x):
    return jnp.maximum(x, 0.0) + jnp.log(1.0 + jnp.exp(-jnp.abs(x)))


def _split_bf16(w, n):
    out, rem = [], w.astype(F32)
    for _ in range(n):
        piece = rem.astype(BF16)
        out.append(piece)
        rem = rem - piece.astype(F32)
    return jnp.stack(out)


def _block_tri(rows, chunk):
    idx = jnp.arange(rows)
    same = (idx[:, None] // chunk) == (idx[None, :] // chunk)
    return (same & (idx[:, None] >= idx[None, :])).astype(BF16)


def _unit_lower_inverse(lo_tri, n_sub, prec):
    n = lo_tri.shape[-1]
    lead = (1,) * (lo_tri.ndim - 2)
    row = lax.broadcasted_iota(jnp.int32, lead + (n, n), lo_tri.ndim - 2)
    col = lax.broadcasted_iota(jnp.int32, lead + (n, n), lo_tri.ndim - 1)
    eye = (row == col).astype(F32)
    same = (row // SOLVE_BLOCK) == (col // SOLVE_BLOCK)
    l_diag = jnp.where(same, lo_tri, 0.0)
    l_off = jnp.where(same, 0.0, lo_tri)

    def neumann(a, s, order):
        width = 1
        while width < order:
            if 2 * width < order:
                both = _mm(a, jnp.concatenate([s, a], axis=-1), prec)
                s = s + both[..., :n]
                a = both[..., n:]
            else:
                s = s + _mm(a, s, prec)
            width *= 2
        return s

    t_diag = neumann(_mm(l_diag, l_diag, prec), eye + l_diag, SOLVE_BLOCK // 2)
    return neumann(_mm(t_diag, l_off, prec), t_diag, n_sub)


def _norm_matmul_kernel(x_ref, nw_ref, wa_ref, wg_ref, wd_ref, o_ref, h_ref, *, na, ng):
    j = pl.program_id(1)

    @pl.when(j == 0)
    def _():
        for r0 in range(0, x_ref.shape[0], NORM_CHUNK):
            rows = slice(r0, r0 + NORM_CHUNK)
            x = x_ref[rows, :]
            ms = jnp.mean(x * x, axis=-1, keepdims=True)
            h = (x * lax.rsqrt(ms + NORM_EPS) * nw_ref[...]).astype(BF16)
            h_ref[rows, :] = h
            o_ref[rows, :] = _dot(h, wa_ref[...])

    @pl.when((j > 0) & (j < na))
    def _():
        o_ref[...] = _dot(h_ref[...], wa_ref[...])

    @pl.when((j >= na) & (j < na + ng))
    def _():
        o_ref[...] = _dot(h_ref[...], wg_ref[...])

    @pl.when(j >= na + ng)
    def _():
        o_ref[...] = _dot(h_ref[...], wd_ref[...])


def _norm_matmul(x2d, nw, w_sections, tm=1024, tn=1024):
    n, d = x2d.shape
    tm = min(tm, n)
    wa, wg, wd = w_sections
    na, ng, nd = wa.shape[1] // tn, wg.shape[1] // tn, wd.shape[1] // tn
    return pl.pallas_call(
        functools.partial(_norm_matmul_kernel, na=na, ng=ng),
        grid=(n // tm, na + ng + nd),
        in_specs=[
            pl.BlockSpec((tm, d), lambda i, j: (i, 0)),
            pl.BlockSpec((1, d), lambda i, j: (0, 0)),
            pl.BlockSpec((d, tn), lambda i, j: (0, jnp.minimum(j, na - 1))),
            pl.BlockSpec((d, tn), lambda i, j: (0, jnp.clip(j - na, 0, ng - 1))),
            pl.BlockSpec((d, tn), lambda i, j: (0, jnp.maximum(j - na - ng, 0))),
        ],
        out_specs=pl.BlockSpec((tm, tn), lambda i, j: (i, j)),
        out_shape=jax.ShapeDtypeStruct((n, (na + ng + nd) * tn), F32),
        scratch_shapes=[pltpu.VMEM((tm, d), BF16)],
        compiler_params=pltpu.CompilerParams(
            dimension_semantics=("parallel", "arbitrary"), vmem_limit_bytes=VMEM_LIMIT),
    )(x2d, nw.reshape(1, d), wa, wg, wd)


def _rwkv_kernel(*refs, rows, chunk, has_vres):
    if has_vres:
        (p_ref, vf_ref, mu_ref, w0_ref, wbw_ref, a0_ref, wba_ref, wbg_ref, v0_ref, wbv_ref,
         kk_ref, ka_ref, rk_ref, lnw_ref, lnb_ref, gmat_ref, tri_ref,
         y_ref, buf_ref, s_ref, ybuf_ref) = refs
    else:
        (p_ref, mu_ref, w0_ref, wbw_ref, a0_ref, wba_ref, wbg_ref,
         kk_ref, ka_ref, rk_ref, lnw_ref, lnb_ref, gmat_ref, tri_ref,
         y_ref, vout_ref, buf_ref, s_ref, ybuf_ref) = refs
    c = chunk
    ns = rows // c
    n_pairs = W_RWKV // LANES
    p_lora, p_ones, p_gsum, p_inv = PREC["lora"], PREC["ones"], PREC["gsum"], PREC["inv"]
    wpieces = lambda ref: [ref[i] for i in range(ref.shape[0])]
    pair = lambda pi: slice(pi * LANES, (pi + 1) * LANES)
    sub = lambda j: slice(j * c, (j + 1) * c)

    @pl.when(pl.program_id(1) == 0)
    def _():
        buf_ref[0:8, :] = jnp.zeros((8, RW_BLOCK), F32)
        s_ref[...] = jnp.zeros_like(s_ref)

    x = p_ref[0]
    buf_ref[8:8 + rows, :] = x
    prev = buf_ref[7:7 + rows, :]
    buf_ref[0:8, :] = x[rows - 8:rows, :]
    pm = x + (prev - x) * mu_ref[...]

    r = pm[:, 0:W_RWKV]
    k = pm[:, W_RWKV:2 * W_RWKV]
    v = pm[:, 2 * W_RWKV:3 * W_RWKV]
    lw = pm[:, 3 * W_RWKV:3 * W_RWKV + LANES]
    xg = pm[:, 3 * W_RWKV + LANES:3 * W_RWKV + 2 * LANES]
    w_log = -_softplus(-(w0_ref[...] + _mm(jnp.tanh(lw), wpieces(wbw_ref), p_lora))) - 0.5
    logd = -jnp.exp(w_log)
    a = _sigmoid(a0_ref[...] + _mm(lw, wpieces(wba_ref), p_lora))
    g = _mm(_sigmoid(xg), wpieces(wbg_ref), p_lora)
    if has_vres:
        xv = pm[:, COL_VRES:COL_VRES + LANES]
        v = v + (vf_ref[0] - v) * _sigmoid(v0_ref[...] + _mm(xv, wpieces(wbv_ref), p_lora))
    else:
        vout_ref[0] = v

    gmat = gmat_ref[...]

    def gsum(z):
        return jnp.concatenate([_mm(z[:, pair(pi)], gmat, p_gsum) for pi in range(n_pairs)], axis=1)

    kk = k * kk_ref[...]
    kk = kk * lax.rsqrt(gsum(kk * kk) + NORM_EPS)
    k2 = k * (1.0 + (a - 1.0) * ka_ref[...])
    b = kk * a
    gc = _mm(tri_ref[...], logd, p_ones)
    g_end = [gc[j * c + c - 1:j * c + c, :] for j in range(ns)]
    e_end = jnp.exp(jnp.concatenate([g_end[j] - gc[sub(j), :] for j in range(ns)], axis=0))
    e_inv = jnp.exp(-gc)

    lane = lax.broadcasted_iota(jnp.int32, (1, W_RWKV), 1)
    h0 = ((lane // HEAD_DIM_RWKV) % 2 == 0).astype(F32)
    h1 = 1.0 - h0

    def stack_all(z):
        z0 = (z * h0).astype(BF16)
        z1 = (z * h1).astype(BF16)
        return jnp.stack([jnp.concatenate([z0[sub(j), pair(pi)], z1[sub(j), pair(pi)]], axis=0)
                          for j in range(ns) for pi in range(n_pairs)])

    at = stack_all(-kk * jnp.exp(gc - logd))
    rt = stack_all(r * jnp.exp(gc))
    bt = stack_all(b * e_inv)
    kt = stack_all(k2 * e_inv)
    vst = stack_all(v)
    bh = stack_all(b * e_end)
    kh = stack_all(k2 * e_end)

    n2 = 2 * c
    quad = _mm(jnp.concatenate([at, rt], axis=1), jnp.concatenate([bt, kt], axis=1), form="nt")
    row = lax.broadcasted_iota(jnp.int32, (1, n2, n2), 1) % c
    col = lax.broadcasted_iota(jnp.int32, (1, n2, n2), 2) % c
    strict = row > col
    incl = row >= col
    l_ab = jnp.where(strict, quad[:, :n2, :n2], 0.0)
    a_rb = jnp.where(incl, quad[:, n2:, :n2], 0.0).astype(BF16)
    vmask = jnp.concatenate([jnp.broadcast_to(strict, (1, n2, n2)), jnp.broadcast_to(incl, (1, n2, n2))], axis=1)
    own = _mm(jnp.where(vmask, quad[:, :, n2:], 0.0).astype(BF16), vst)
    y0 = own[:, n2:]
    t_inv = _unit_lower_inverse(l_ab, c // SOLVE_BLOCK, p_inv).astype(BF16)
    t_both = _mm(t_inv, jnp.concatenate([at, own[:, :n2].astype(BF16)], axis=-1))
    u0 = t_both[:, :, LANES:]
    wr = jnp.concatenate([t_both[:, :, :LANES].astype(BF16), rt], axis=1)
    kv = _mm(vst, kh, form="tn")

    s = s_ref[...]
    for j in range(ns):
        gs = slice(j * n_pairs, (j + 1) * n_pairs)
        from_s = _mm(wr[gs], s.astype(BF16), form="nt")
        u = u0[gs] + from_s[:, :n2]
        u_b = u.astype(BF16)
        y_st = from_s[:, n2:] + _mm(a_rb[gs], u_b) + y0[gs]
        d_end = jnp.exp(jnp.stack([g_end[j][:, pair(pi)] for pi in range(n_pairs)]))
        s = s * d_end + _mm(u_b, bh[gs], form="tn") + kv[gs]
        y = y_st[:, :c] + y_st[:, c:]
        for pi in range(n_pairs):
            ybuf_ref[sub(j), pair(pi)] = y[pi]
    s_ref[...] = s

    y = ybuf_ref[...]
    inv_n = 1.0 / HEAD_DIM_RWKV
    mean = gsum(y) * inv_n
    dlt = y - mean
    var = gsum(dlt * dlt) * inv_n
    yn = dlt * lax.rsqrt(var + RWKV_GN_EPS) * lnw_ref[...] + lnb_ref[...]
    bonus = gsum(r * k2 * rk_ref[...]) * v
    y_ref[0] = ((yn + bonus) * g).astype(y_ref.dtype)


def _rwkv_mix(p3d, v_first, mu_pad, w0, wbw, a0, wba, wbg, v0, wbv, k_k, k_a, r_k, ln_w, ln_b, rows=256, chunk=64):
    bsz, t, _ = p3d.shape
    rows = min(rows, t)
    has_vres = v_first is not None
    n_lora = PREC["lora"][1]
    row = lambda z: z.reshape(1, -1).astype(F32)
    lane = jnp.arange(LANES)
    gmat = ((lane[:, None] // HEAD_DIM_RWKV) == (lane[None, :] // HEAD_DIM_RWKV)).astype(BF16)
    zeros = jnp.zeros((LORA_DECAY, W_RWKV), F32)
    wbw_pad = _split_bf16(jnp.concatenate([wbw, zeros], axis=0), n_lora)
    wba_pad = _split_bf16(jnp.concatenate([zeros, wba], axis=0), n_lora)
    const = lambda shape: pl.BlockSpec(shape, lambda b, c: (0,) * len(shape))
    lora_spec = const((n_lora, LANES, W_RWKV))
    p_spec = pl.BlockSpec((1, rows, RW_BLOCK), lambda b, c: (b, c, 0))
    seq_spec = pl.BlockSpec((1, rows, W_RWKV), lambda b, c: (b, c, 0))
    args = [p3d]
    specs = [p_spec]
    if has_vres:
        args.append(v_first)
        specs.append(seq_spec)
    args += [mu_pad, row(w0), wbw_pad, row(a0), wba_pad, _split_bf16(wbg, n_lora)]
    specs += [const((1, RW_BLOCK)), const((1, W_RWKV)), lora_spec, const((1, W_RWKV)), lora_spec, lora_spec]
    if has_vres:
        wbv_pad = jnp.concatenate([wbv, jnp.zeros((LANES - LORA_VRES, W_RWKV), F32)], axis=0)
        args += [row(v0), _split_bf16(wbv_pad, n_lora)]
        specs += [const((1, W_RWKV)), lora_spec]
    args += [row(k_k), row(k_a), row(r_k), row(ln_w), row(ln_b), gmat, _block_tri(rows, chunk)]
    specs += [const((1, W_RWKV))] * 5 + [const((LANES, LANES)), const((rows, rows))]
    y_shape = jax.ShapeDtypeStruct((bsz, t, W_RWKV), BF16)
    if has_vres:
        out_shape, out_specs = y_shape, seq_spec
    else:
        out_shape = (y_shape, jax.ShapeDtypeStruct((bsz, t, W_RWKV), F32))
        out_specs = (seq_spec, seq_spec)
    return pl.pallas_call(
        functools.partial(_rwkv_kernel, rows=rows, chunk=chunk, has_vres=has_vres),
        grid=(bsz, t // rows),
        in_specs=specs,
        out_specs=out_specs,
        out_shape=out_shape,
        scratch_shapes=[pltpu.VMEM((rows + 8, RW_BLOCK), F32),
                        pltpu.VMEM((W_RWKV // LANES, LANES, LANES), F32),
                        pltpu.VMEM((rows, W_RWKV), F32)],
        compiler_params=pltpu.CompilerParams(
            dimension_semantics=("parallel", "arbitrary"), vmem_limit_bytes=VMEM_LIMIT),
    )(*args)


def _gdn_kernel(x_ref, ab_ref, convw_ref, hp_ref, normw_ref, tri_ref, eye_ref, ones_ref,
                y_ref, buf_ref, s_ref, obuf_ref, *, rows, chunk):
    c = chunk
    ns = rows // c
    nh = N_HEADS_GDN
    wq = 3 * W_GDN
    p_ones, p_inv = PREC["ones"], PREC["inv"]
    head = lambda h: slice(h * LANES, (h + 1) * LANES)
    sub = lambda j: slice(j * c, (j + 1) * c)

    @pl.when(pl.program_id(1) == 0)
    def _():
        buf_ref[0:8, :] = jnp.zeros((8, wq), F32)
        s_ref[...] = jnp.zeros_like(s_ref)

    x = x_ref[0]
    xc = x[:, :wq]
    buf_ref[8:8 + rows, :] = xc
    conv = xc * convw_ref[3:4, :]
    for i in range(CONV_WIDTH - 1):
        conv = conv + buf_ref[5 + i:5 + i + rows, :] * convw_ref[i:i + 1, :]
    buf_ref[0:8, :] = xc[rows - 8:rows, :]
    qkv = conv * _sigmoid(conv)

    ab = ab_ref[0]
    g_all = -jnp.exp(hp_ref[0:1, :]) * _softplus(ab + hp_ref[1:2, :])
    beta_all = _sigmoid(ab)
    gc_all = _mm(tri_ref[...], g_all, p_ones)
    gc_t = _mm(eye_ref[...], gc_all, p_ones, "nt")
    ones = ones_ref[...]

    def l2n(z):
        return z * lax.rsqrt(_mm(z * z, ones, (2, 1)) + NORM_EPS)

    chains = [(j, h) for j in range(ns) for h in range(nh)]
    k_l, kb_l, q_l, vk_l, qe_l, ke_l, gcol_l, grow_l, gend_l = [], [], [], [], [], [], [], [], []
    for h in range(nh):
        q = l2n(qkv[:, head(h)]) * (HEAD_DIM_GDN ** -0.5)
        k = l2n(qkv[:, W_GDN + h * LANES:W_GDN + (h + 1) * LANES])
        v = qkv[:, 2 * W_GDN + h * LANES:2 * W_GDN + (h + 1) * LANES]
        gcol = gc_all[:, h:h + 1]
        beta = beta_all[:, nh + h:nh + h + 1]
        g_end = [gcol[j * c + c - 1:j * c + c, :] for j in range(ns)]
        e_gc = jnp.exp(gcol)
        e_end = jnp.exp(jnp.concatenate([g_end[j] - gcol[sub(j), :] for j in range(ns)], axis=0))
        kb = k * beta
        k_l.append(k.astype(BF16))
        kb_l.append(kb.astype(BF16))
        q_l.append(q.astype(BF16))
        vk_l.append(jnp.concatenate([v * beta, kb * e_gc], axis=1).astype(BF16))
        qe_l.append((q * e_gc).astype(BF16))
        ke_l.append((k * e_end).astype(BF16))
        gcol_l.append(gcol)
        grow_l.append(gc_t[h:h + 1, :])
        gend_l.append(g_end)
    gather = lambda lst: jnp.stack([lst[h][sub(j), :] for j, h in chains])
    k_b, kb_b, q_b, vk_b, qe_b, ke_b = (gather(l) for l in (k_l, kb_l, q_l, vk_l, qe_l, ke_l))
    gcol_b = gather(gcol_l)
    grow_b = jnp.stack([grow_l[h][:, sub(j)] for j, h in chains])

    row = lax.broadcasted_iota(jnp.int32, (1, c, c), 1)
    col = lax.broadcasted_iota(jnp.int32, (1, c, c), 2)
    decay = jnp.exp(jnp.where(row >= col, gcol_b - grow_b, -jnp.inf))
    both = _mm(jnp.concatenate([kb_b, q_b], axis=1), k_b, form="nt")
    kkt, qkt = both[:, :c], both[:, c:]
    t_inv = _unit_lower_inverse(-jnp.where(row > col, kkt * decay, 0.0), c // SOLVE_BLOCK, p_inv).astype(BF16)
    uw = _mm(t_inv, vk_b)
    u = uw[:, :, :LANES]
    wq_b = jnp.concatenate([uw[:, :, LANES:].astype(BF16), qe_b], axis=1)
    attn = (qkt * decay).astype(BF16)

    s = s_ref[...]
    for j in range(ns):
        gs = slice(j * nh, (j + 1) * nh)
        ws_qs = _mm(wq_b[gs], s.astype(BF16))
        v_new = (u[gs] - ws_qs[:, :c]).astype(BF16)
        o = ws_qs[:, c:] + _mm(attn[gs], v_new)
        d_end = jnp.exp(jnp.stack([gend_l[h][j] for h in range(nh)]))
        s = s * d_end + _mm(ke_b[gs], v_new, form="tn")
        for h in range(nh):
            obuf_ref[sub(j), head(h)] = o[h]
    s_ref[...] = s

    o = obuf_ref[...]
    ms = jnp.concatenate([_mm(o[:, head(h)] * o[:, head(h)], ones, (2, 1)) for h in range(nh)], axis=1)
    on = o * lax.rsqrt(ms * (1.0 / HEAD_DIM_GDN) + NORM_EPS) * normw_ref[...]
    z = x[:, wq:]
    y_ref[0] = (on * (z * _sigmoid(z))).astype(y_ref.dtype)


def _gdn_mix(p3d, conv_w, a_log, dt_bias, norm_w, rows=256, chunk=128):
    bsz, t, _ = p3d.shape
    rows = min(rows, t)
    pad = jnp.zeros((LANES - N_HEADS_GDN,), F32)
    hp = jnp.stack([jnp.concatenate([a_log.astype(F32), pad]), jnp.concatenate([dt_bias.astype(F32), pad])])
    eye = jnp.eye(LANES, dtype=BF16)
    ones = jnp.ones((LANES, LANES), BF16)
    normw = jnp.tile(norm_w.astype(F32), N_HEADS_GDN).reshape(1, W_GDN)
    const = lambda shape: pl.BlockSpec(shape, lambda b, c: (0,) * len(shape))
    return pl.pallas_call(
        functools.partial(_gdn_kernel, rows=rows, chunk=chunk),
        grid=(bsz, t // rows),
        in_specs=[
            pl.BlockSpec((1, rows, 4 * W_GDN), lambda b, c: (b, c, COL_GDN // (4 * W_GDN))),
            pl.BlockSpec((1, rows, LANES), lambda b, c: (b, c, COL_GDN_AB // LANES)),
            const((CONV_WIDTH, 3 * W_GDN)), const((2, LANES)), const((1, W_GDN)),
            const((rows, rows)), const((LANES, LANES)), const((LANES, LANES)),
        ],
        out_specs=pl.BlockSpec((1, rows, W_GDN), lambda b, c: (b, c, 0)),
        out_shape=jax.ShapeDtypeStruct((bsz, t, W_GDN), BF16),
        scratch_shapes=[pltpu.VMEM((rows + 8, 3 * W_GDN), F32),
                        pltpu.VMEM((N_HEADS_GDN, HEAD_DIM_GDN, HEAD_DIM_GDN), F32),
                        pltpu.VMEM((rows, W_GDN), F32)],
        compiler_params=pltpu.CompilerParams(
            dimension_semantics=("parallel", "arbitrary"), vmem_limit_bytes=VMEM_LIMIT),
    )(p3d, p3d, conv_w.astype(F32), hp, normw, _block_tri(rows, chunk), eye, ones)


def _diff_attn_kernel(q_ref, k_ref, v_ref, qw_ref, kw_ref, gmat_ref, eye_ref, lam_ref, subw_ref, o_ref,
                      qt_ref, kn_ref, vt_ref, *, tq, tk, lambda_init):
    t = k_ref.shape[1]
    gmat = gmat_ref[...]
    eye = eye_ref[...]

    def normed(ref, wref, rows):
        x = ref[0, rows, :]
        ms = _mm(x * x, gmat, PREC["gsum"]) * (1.0 / HEAD_DIM_DIFF)
        return (x * lax.rsqrt(ms + NORM_EPS) * wref[...]).astype(BF16)

    for r0 in range(0, t, tk):
        rows = slice(r0, r0 + tk)
        kn_ref[rows, :] = normed(k_ref, kw_ref, rows)
        qt_ref[:, rows] = _mm(eye, normed(q_ref, qw_ref, rows), form="nt").astype(BF16)
        vt_ref[:, rows] = _mm(eye, v_ref[0, rows, :].astype(BF16), form="nt").astype(BF16)

    feat = lax.broadcasted_iota(jnp.int32, (LANES, 1), 0)
    lv = lam_ref[...]
    lam = (jnp.exp(jnp.sum(lv[0:1] * lv[1:2], axis=-1, keepdims=True))
           - jnp.exp(jnp.sum(lv[2:3] * lv[3:4], axis=-1, keepdims=True)) + lambda_init)

    tiles = []
    for i in range(t // tq):
        q_end = (i + 1) * tq
        for k0 in range(0, q_end, tk):
            klen = min(tk, q_end - k0)
            tiles.append((i, k0, klen, k0 + klen > i * tq))

    q_maps = {}

    def scores(i, k0, klen):
        if i not in q_maps:
            q_t = qt_ref[:, i * tq:(i + 1) * tq]
            zero = jnp.zeros_like(q_t)
            q_maps.clear()
            q_maps[i] = (jnp.where(feat < HEAD_DIM_DIFF, q_t, zero), jnp.where(feat >= HEAD_DIM_DIFF, q_t, zero))
        k = kn_ref[k0:k0 + klen, :]
        return [_dot(k, qm) for qm in q_maps[i]]

    state = None
    pending = scores(*tiles[0][:3])
    for n, (i, k0, klen, diag) in enumerate(tiles):
        cur = pending
        if n + 1 < len(tiles):
            pending = scores(*tiles[n + 1][:3])
        v_t = vt_ref[:, k0:k0 + klen]
        new_state = []
        for mp, s in enumerate(cur):
            if diag:
                key = lax.broadcasted_iota(jnp.int32, (klen, tq), 0)
                qry = lax.broadcasted_iota(jnp.int32, (klen, tq), 1)
                s = jnp.where(key - qry <= i * tq - k0, s, -jnp.inf)
            if k0 == 0:
                m_new = jnp.max(s, axis=0, keepdims=True)
                p = jnp.exp2(s - m_new)
                l_new = jnp.sum(p, axis=0, keepdims=True)
                acc_new = _dot(v_t, p.astype(BF16))
            else:
                m_prev, l_prev, acc_prev = state[mp]
                m_new = jnp.maximum(m_prev, jnp.max(s, axis=0, keepdims=True))
                alpha = jnp.exp2(m_prev - m_new)
                p = jnp.exp2(s - m_new)
                l_new = alpha * l_prev + jnp.sum(p, axis=0, keepdims=True)
                acc_new = alpha * acc_prev + _dot(v_t, p.astype(BF16))
            new_state.append((m_new, l_new, acc_new))
        state = new_state
        if k0 + klen == (i + 1) * tq:
            (_, l0, acc0), (_, l1, acc1) = state
            o_t = acc0 * (1.0 / l0) - lam * (acc1 * (1.0 / l1))
            ms = jnp.sum(o_t * o_t, axis=0, keepdims=True) * (1.0 / LANES)
            on_t = (o_t * lax.rsqrt(ms + NORM_EPS) * subw_ref[...] * (1.0 - lambda_init)).astype(BF16)
            o_ref[0, i * tq:(i + 1) * tq, :] = _mm(on_t, eye, form="tn").astype(o_ref.dtype)


def _diff_attn(p3d, q_norm_w, k_norm_w, lam_vecs, subln_w, lambda_init, tq=512, tk=1024):
    bsz, t, _ = p3d.shape
    tk = min(tk, t)
    tq = min(tq, tk)
    lane = jnp.arange(LANES)
    gmat = ((lane[:, None] // HEAD_DIM_DIFF) == (lane[None, :] // HEAD_DIM_DIFF)).astype(BF16)
    qw = (jnp.tile(q_norm_w.astype(F32), 2) * (HEAD_DIM_DIFF ** -0.5 * math.log2(math.e))).reshape(1, LANES)
    kw = jnp.tile(k_norm_w.astype(F32), 2).reshape(1, LANES)
    base = COL_DIFF // LANES
    sec = lambda j: pl.BlockSpec((1, t, LANES), lambda b, h: (b, 0, base + j * N_HEADS_DIFF + h))
    const = lambda shape: pl.BlockSpec(shape, lambda b, h: (0,) * len(shape))
    return pl.pallas_call(
        functools.partial(_diff_attn_kernel, tq=tq, tk=tk, lambda_init=lambda_init),
        grid=(bsz, N_HEADS_DIFF),
        in_specs=[sec(0), sec(1), sec(2), const((1, LANES)), const((1, LANES)),
                  const((LANES, LANES)), const((LANES, LANES)), const((4, HEAD_DIM_DIFF)), const((LANES, 1))],
        out_specs=pl.BlockSpec((1, t, LANES), lambda b, h: (b, 0, h)),
        out_shape=jax.ShapeDtypeStruct((bsz, t, W_DIFF), BF16),
        scratch_shapes=[pltpu.VMEM((LANES, t), BF16), pltpu.VMEM((t, LANES), BF16), pltpu.VMEM((LANES, t), BF16)],
        compiler_params=pltpu.CompilerParams(
            dimension_semantics=("parallel", "parallel"), vmem_limit_bytes=VMEM_LIMIT),
    )(p3d, p3d, p3d, qw, kw, gmat, jnp.eye(LANES, dtype=BF16), lam_vecs, subln_w.reshape(LANES, 1).astype(F32))


def _mix_ffn_kernel(x_ref, yr_ref, yg_ref, yd_ref, w1_ref, w2_ref, w3_ref, nw_ref, wg_ref, wu_ref, wo_ref,
                    o_ref, h_ref):
    @pl.when(pl.program_id(1) == 0)
    def _():
        x1 = (x_ref[...] + _dot(yr_ref[...], w1_ref[...]) + _dot(yg_ref[...], w2_ref[...])
              + _dot(yd_ref[...], w3_ref[...]))
        ms = jnp.mean(x1 * x1, axis=-1, keepdims=True)
        h_ref[...] = (x1 * lax.rsqrt(ms + NORM_EPS) * nw_ref[...]).astype(BF16)
        o_ref[...] = x1

    h = h_ref[...]
    gate = _dot(h, wg_ref[...])
    up = _dot(h, wu_ref[...])
    act = (gate * _sigmoid(gate) * up).astype(BF16)
    o_ref[...] += _dot(act, wo_ref[...])


def _mix_ffn(x2d, y_rw, y_gdn, y_diff, w_mix_bf16, nw, w_in_bf16, w_out_bf16, tm=512, tf=512):
    n, d = x2d.shape
    hidden = w_out_bf16.shape[0]
    nf = hidden // tf
    rowblk = lambda w: pl.BlockSpec((tm, w), lambda i, f: (i, 0))
    resident = lambda rows, blk: pl.BlockSpec((rows, d), lambda i, f: (blk, 0), pipeline_mode=pl.Buffered(1))
    return pl.pallas_call(
        _mix_ffn_kernel,
        grid=(n // tm, nf),
        in_specs=[
            rowblk(d), rowblk(W_RWKV), rowblk(W_GDN), rowblk(W_DIFF),
            resident(W_RWKV, 0), resident(W_GDN, 1), resident(W_DIFF, 1),
            pl.BlockSpec((1, d), lambda i, f: (0, 0)),
            pl.BlockSpec((d, tf), lambda i, f: (0, f)),
            pl.BlockSpec((d, tf), lambda i, f: (0, nf + f)),
            pl.BlockSpec((tf, d), lambda i, f: (f, 0)),
        ],
        out_specs=pl.BlockSpec((tm, d), lambda i, f: (i, 0)),
        out_shape=jax.ShapeDtypeStruct((n, d), F32),
        scratch_shapes=[pltpu.VMEM((tm, d), BF16)],
        compiler_params=pltpu.CompilerParams(
            dimension_semantics=("parallel", "arbitrary"), vmem_limit_bytes=VMEM_LIMIT),
    )(x2d, y_rw, y_gdn, y_diff, w_mix_bf16, w_mix_bf16, w_mix_bf16, nw.reshape(1, d),
      w_in_bf16, w_in_bf16, w_out_bf16)


def _pack_in_proj(w_in_l, w_vres_l):
    d = w_in_l.shape[0]
    vres = jnp.zeros((d, LANES), F32)
    if w_vres_l is not None:
        vres = vres.at[:, :LORA_VRES].set(w_vres_l)
    rb = 64
    return pl.pallas_call(
        _pack_w_kernel,
        grid=(d // rb,),
        in_specs=[pl.BlockSpec((rb, N_IN), lambda i: (i, 0)), pl.BlockSpec((rb, LANES), lambda i: (i, 0))],
        out_specs=(pl.BlockSpec((rb, COL_GDN), lambda i: (i, 0)),
                   pl.BlockSpec((rb, 4 * W_GDN), lambda i: (i, 0)),
                   pl.BlockSpec((rb, DIFF_COLS), lambda i: (i, 0))),
        out_shape=(jax.ShapeDtypeStruct((d, COL_GDN), BF16), jax.ShapeDtypeStruct((d, 4 * W_GDN), BF16),
                   jax.ShapeDtypeStruct((d, DIFF_COLS), BF16)),
        compiler_params=pltpu.CompilerParams(dimension_semantics=("parallel",), vmem_limit_bytes=VMEM_LIMIT),
    )(w_in_l, vres)


def _pack_w_kernel(w_ref, vres_ref, wa_ref, wg_ref, wd_ref):
    gdn0 = RWKV_COLS
    ab0 = gdn0 + 4 * W_GDN
    lane = lax.broadcasted_iota(jnp.int32, (1, LANES), 1)
    wa_ref[:, :RWKV_COLS] = w_ref[:, :RWKV_COLS].astype(BF16)
    wa_ref[:, COL_VRES:COL_GDN_AB] = vres_ref[...].astype(BF16)
    wa_ref[:, COL_GDN_AB:] = jnp.where(lane < 2 * N_HEADS_GDN, w_ref[:, ab0:ab0 + LANES], 0.0).astype(BF16)
    wg_ref[...] = w_ref[:, gdn0:ab0].astype(BF16)
    wd_ref[...] = w_ref[:, gdn0 + GDN_COLS:].astype(BF16)


def kernel(x, attn_norm_w, w_in, w_vres_a, mu_rwkv, mu_vres, rwkv_w0, rwkv_w_lora_b, rwkv_a0, rwkv_a_lora_b, rwkv_g_lora_b, rwkv_v0, rwkv_v_lora_b, rwkv_k_k, rwkv_k_a, rwkv_r_k, rwkv_ln_w, rwkv_ln_b, gdn_conv_w, gdn_A_log, gdn_dt_bias, gdn_norm_w, diff_q_norm_w, diff_k_norm_w, diff_lambda_q1, diff_lambda_k1, diff_lambda_q2, diff_lambda_k2, diff_subln_w, w_out, ffn_norm_w, w_ffn_in, w_ffn_out):
    bsz, t, d = x.shape
    depth = w_in.shape[0]
    x2d = x.reshape(bsz * t, d)
    v_first = None
    for l in range(depth):
        w_proj = _pack_in_proj(w_in[l], None if l == 0 else w_vres_a[l - 1])
        mu_tail = jnp.zeros((RW_BLOCK - RWKV_COLS,), F32)
        if l > 0:
            mu_tail = mu_tail.at[:LORA_VRES].set(mu_vres[l - 1])
        mu_pad = jnp.concatenate([mu_rwkv[l], mu_tail]).reshape(1, RW_BLOCK)
        p3d = _norm_matmul(x2d, attn_norm_w[l], w_proj).reshape(bsz, t, N_PAD)
        rw_args = (rwkv_w0[l], rwkv_w_lora_b[l], rwkv_a0[l], rwkv_a_lora_b[l], rwkv_g_lora_b[l])
        rw_tail = (rwkv_k_k[l], rwkv_k_a[l], rwkv_r_k[l], rwkv_ln_w[l], rwkv_ln_b[l])
        if l == 0:
            y_rw, v_first = _rwkv_mix(p3d, None, mu_pad, *rw_args, None, None, *rw_tail)
        else:
            y_rw = _rwkv_mix(p3d, v_first, mu_pad, *rw_args, rwkv_v0[l - 1], rwkv_v_lora_b[l - 1], *rw_tail)
        y_gdn = _gdn_mix(p3d, gdn_conv_w[l], gdn_A_log[l], gdn_dt_bias[l], gdn_norm_w[l])
        lambda_init = 0.8 - 0.6 * math.exp(-0.3 * l)
        lam_vecs = jnp.stack([diff_lambda_q1[l], diff_lambda_k1[l], diff_lambda_q2[l], diff_lambda_k2[l]]).astype(F32)
        y_diff = _diff_attn(p3d, diff_q_norm_w[l], diff_k_norm_w[l], lam_vecs, diff_subln_w[l], lambda_init)
        n = bsz * t
        x2d = _mix_ffn(x2d, y_rw.reshape(n, W_RWKV), y_gdn.reshape(n, W_GDN), y_diff.reshape(n, W_DIFF),
                       w_out[l].astype(BF16), ffn_norm_w[l], w_ffn_in[l].astype(BF16), w_ffn_out[l].astype(BF16))
    return x2d.reshape(bsz, t, d)
```

```python
import functools
import math

import jax
import jax.numpy as jnp
from jax import lax
from jax.experimental import pallas as pl
from jax.experimental.pallas import tpu as pltpu

F32 = jnp.float32
BF16 = jnp.bfloat16

D_MODEL = 2048
W_RWKV = D_MODEL // 4
HEAD_DIM_RWKV = 64
LORA_DECAY = 64
LORA_ICLR = 64
LORA_VRES = 32
LORA_GATE = 128
RWKV_GN_EPS = 64e-5
W_GDN = D_MODEL // 4
HEAD_DIM_GDN = 128
N_HEADS_GDN = W_GDN // HEAD_DIM_GDN
CONV_WIDTH = 4
W_DIFF = D_MODEL // 2
HEAD_DIM_DIFF = 64
N_HEADS_DIFF = W_DIFF // (2 * HEAD_DIM_DIFF)
RWKV_COLS = 3 * W_RWKV + LORA_DECAY + LORA_ICLR + LORA_GATE
GDN_COLS = 4 * W_GDN + 2 * N_HEADS_GDN
DIFF_COLS = 3 * W_DIFF
N_IN = RWKV_COLS + GDN_COLS + DIFF_COLS
FFN_HIDDEN = -(-8 * D_MODEL // (3 * 256)) * 256
NORM_EPS = 1e-6

LANES = 128
SOLVE_BLOCK = 16
NORM_CHUNK = 256

COL_VRES = RWKV_COLS
COL_GDN_AB = COL_VRES + LANES
COL_GDN = COL_GDN_AB + LANES
COL_DIFF = COL_GDN + 4 * W_GDN
N_PAD = COL_DIFF + DIFF_COLS
RW_BLOCK = COL_GDN

VMEM_LIMIT = 56 * 1024 * 1024

PREC = {
    "lora": (1, 1),
    "ones": (1, 3),
    "gsum": (1, 1),
    "inv": (1, 1),
}


def _pieces(x, n):
    if isinstance(x, (list, tuple)):
        return list(x)[:n]
    if x.dtype == BF16:
        return [x]
    out, rem = [], x
    for i in range(n):
        piece = rem.astype(BF16)
        out.append(piece)
        if i + 1 < n:
            rem = rem - piece.astype(F32)
    return out


_DIMS = {"nn": (((1,), (0,)), ((), ())), "nt": (((1,), (1,)), ((), ())), "tn": (((0,), (0,)), ((), ()))}
_DIMS_B = {"nn": (((2,), (1,)), ((0,), (0,))), "nt": (((2,), (2,)), ((0,), (0,))), "tn": (((1,), (1,)), ((0,), (0,)))}


def _mm(a, b, prec=(1, 1), form="nn"):
    pa, pb = _pieces(a, prec[0]), _pieces(b, prec[1])
    dims = (_DIMS_B if pa[0].ndim == 3 else _DIMS)[form]
    depth = max(len(pa), len(pb))
    terms = sorted(((i, j) for i in range(len(pa)) for j in range(len(pb)) if i + j < depth),
                   key=lambda ij: -(ij[0] + ij[1]))
    acc = None
    for i, j in terms:
        d = lax.dot_general(pa[i], pb[j], dims, preferred_element_type=F32)
        acc = d if acc is None else acc + d
    return acc


def _dot(a, b):
    return jnp.dot(a, b, preferred_element_type=F32)


def _sigmoid(x):
    return 1.0 / (1.0 + jnp.exp(-x))


def _softplus(x):
    return jnp.maximum(x, 0.0) + jnp.log(1.0 + jnp.exp(-jnp.abs(x)))


def _split_bf16(w, n):
    out, rem = [], w.astype(F32)
    for _ in range(n):
        piece = rem.astype(BF16)
        out.append(piece)
        rem = rem - piece.astype(F32)
    return jnp.stack(out)


def _block_tri(rows, chunk):
    idx = jnp.arange(rows)
    same = (idx[:, None] // chunk) == (idx[None, :] // chunk)
    return (same & (idx[:, None] >= idx[None, :])).astype(BF16)


def _unit_lower_inverse(lo_tri, n_sub, prec):
    n = lo_tri.shape[-1]
    lead = (1,) * (lo_tri.ndim - 2)
    row = lax.broadcasted_iota(jnp.int32, lead + (n, n), lo_tri.ndim - 2)
    col = lax.broadcasted_iota(jnp.int32, lead + (n, n), lo_tri.ndim - 1)
    eye = (row == col).astype(F32)
    same = (row // SOLVE_BLOCK) == (col // SOLVE_BLOCK)
    l_diag = jnp.where(same, lo_tri, 0.0)
    l_off = jnp.where(same, 0.0, lo_tri)

    def neumann(a, s, order):
        width = 1
        while width < order:
            if 2 * width < order:
                both = _mm(a, jnp.concatenate([s, a], axis=-1), prec)
                s = s + both[..., :n]
                a = both[..., n:]
            else:
                s = s + _mm(a, s, prec)
            width *= 2
        return s

    t_diag = neumann(_mm(l_diag, l_diag, prec), eye + l_diag, SOLVE_BLOCK // 2)
    return neumann(_mm(t_diag, l_off, prec), t_diag, n_sub)


def _norm_matmul_kernel(x_ref, nw_ref, wa_ref, wg_ref, wd_ref, o_ref, h_ref, *, na, ng):
    j = pl.program_id(1)

    @pl.when(j == 0)
    def _():
        for r0 in range(0, x_ref.shape[0], NORM_CHUNK):
            rows = slice(r0, r0 + NORM_CHUNK)
            x = x_ref[rows, :]
            ms = jnp.mean(x * x, axis=-1, keepdims=True)
            h = (x * lax.rsqrt(ms + NORM_EPS) * nw_ref[...]).astype(BF16)
            h_ref[rows, :] = h
            o_ref[rows, :] = _dot(h, wa_ref[...])

    @pl.when((j > 0) & (j < na))
    def _():
        o_ref[...] = _dot(h_ref[...], wa_ref[...])

    @pl.when((j >= na) & (j < na + ng))
    def _():
        o_ref[...] = _dot(h_ref[...], wg_ref[...])

    @pl.when(j >= na + ng)
    def _():
        o_ref[...] = _dot(h_ref[...], wd_ref[...])


def _norm_matmul(x2d, nw, w_sections, tm=1024, tn=1024):
    n, d = x2d.shape
    tm = min(tm, n)
    wa, wg, wd = w_sections
    na, ng, nd = wa.shape[1] // tn, wg.shape[1] // tn, wd.shape[1] // tn
    return pl.pallas_call(
        functools.partial(_norm_matmul_kernel, na=na, ng=ng),
        grid=(n // tm, na + ng + nd),
        in_specs=[
            pl.BlockSpec((tm, d), lambda i, j: (i, 0)),
            pl.BlockSpec((1, d), lambda i, j: (0, 0)),
            pl.BlockSpec((d, tn), lambda i, j: (0, jnp.minimum(j, na - 1))),
            pl.BlockSpec((d, tn), lambda i, j: (0, jnp.clip(j - na, 0, ng - 1))),
            pl.BlockSpec((d, tn), lambda i, j: (0, jnp.maximum(j - na - ng, 0))),
        ],
        out_specs=pl.BlockSpec((tm, tn), lambda i, j: (i, j)),
        out_shape=jax.ShapeDtypeStruct((n, (na + ng + nd) * tn), F32),
        scratch_shapes=[pltpu.VMEM((tm, d), BF16)],
        compiler_params=pltpu.CompilerParams(
            dimension_semantics=("parallel", "arbitrary"), vmem_limit_bytes=VMEM_LIMIT),
    )(x2d, nw.reshape(1, d), wa, wg, wd)


def _rwkv_kernel(*refs, rows, chunk, has_vres):
    if has_vres:
        (p_ref, vf_ref, mu_ref, w0_ref, wbw_ref, a0_ref, wba_ref, wbg_ref, v0_ref, wbv_ref,
         kk_ref, ka_ref, rk_ref, lnw_ref, lnb_ref, gmat_ref, tri_ref,
         y_ref, buf_ref, s_ref, ybuf_ref) = refs
    else:
        (p_ref, mu_ref, w0_ref, wbw_ref, a0_ref, wba_ref, wbg_ref,
         kk_ref, ka_ref, rk_ref, lnw_ref, lnb_ref, gmat_ref, tri_ref,
         y_ref, vout_ref, buf_ref, s_ref, ybuf_ref) = refs
    c = chunk
    ns = rows // c
    n_pairs = W_RWKV // LANES
    p_lora, p_ones, p_gsum, p_inv = PREC["lora"], PREC["ones"], PREC["gsum"], PREC["inv"]
    wpieces = lambda ref: [ref[i] for i in range(ref.shape[0])]
    pair = lambda pi: slice(pi * LANES, (pi + 1) * LANES)
    sub = lambda j: slice(j * c, (j + 1) * c)

    @pl.when(pl.program_id(1) == 0)
    def _():
        buf_ref[0:8, :] = jnp.zeros((8, RW_BLOCK), F32)
        s_ref[...] = jnp.zeros_like(s_ref)

    x = p_ref[0]
    buf_ref[8:8 + rows, :] = x
    prev = buf_ref[7:7 + rows, :]
    buf_ref[0:8, :] = x[rows - 8:rows, :]
    pm = x + (prev - x) * mu_ref[...]

    r = pm[:, 0:W_RWKV]
    k = pm[:, W_RWKV:2 * W_RWKV]
    v = pm[:, 2 * W_RWKV:3 * W_RWKV]
    lw = pm[:, 3 * W_RWKV:3 * W_RWKV + LANES]
    xg = pm[:, 3 * W_RWKV + LANES:3 * W_RWKV + 2 * LANES]
    w_log = -_softplus(-(w0_ref[...] + _mm(jnp.tanh(lw), wpieces(wbw_ref), p_lora))) - 0.5
    logd = -jnp.exp(w_log)
    a = _sigmoid(a0_ref[...] + _mm(lw, wpieces(wba_ref), p_lora))
    g = _mm(_sigmoid(xg), wpieces(wbg_ref), p_lora)
    if has_vres:
        xv = pm[:, COL_VRES:COL_VRES + LANES]
        v = v + (vf_ref[0] - v) * _sigmoid(v0_ref[...] + _mm(xv, wpieces(wbv_ref), p_lora))
    else:
        vout_ref[0] = v

    gmat = gmat_ref[...]

    def gsum(z):
        return jnp.concatenate([_mm(z[:, pair(pi)], gmat, p_gsum) for pi in range(n_pairs)], axis=1)

    kk = k * kk_ref[...]
    kk = kk * lax.rsqrt(gsum(kk * kk) + NORM_EPS)
    k2 = k * (1.0 + (a - 1.0) * ka_ref[...])
    b = kk * a
    gc = _mm(tri_ref[...], logd, p_ones)
    g_end = [gc[j * c + c - 1:j * c + c, :] for j in range(ns)]
    e_end = jnp.exp(jnp.concatenate([g_end[j] - gc[sub(j), :] for j in range(ns)], axis=0))
    e_inv = jnp.exp(-gc)

    lane = lax.broadcasted_iota(jnp.int32, (1, W_RWKV), 1)
    h0 = ((lane // HEAD_DIM_RWKV) % 2 == 0).astype(F32)
    h1 = 1.0 - h0

    def stack_all(z):
        z0 = (z * h0).astype(BF16)
        z1 = (z * h1).astype(BF16)
        return jnp.stack([jnp.concatenate([z0[sub(j), pair(pi)], z1[sub(j), pair(pi)]], axis=0)
                          for j in range(ns) for pi in range(n_pairs)])

    at = stack_all(-kk * jnp.exp(gc - logd))
    rt = stack_all(r * jnp.exp(gc))
    bt = stack_all(b * e_inv)
    kt = stack_all(k2 * e_inv)
    vst = stack_all(v)
    bh = stack_all(b * e_end)
    kh = stack_all(k2 * e_end)

    n2 = 2 * c
    quad = _mm(jnp.concatenate([at, rt], axis=1), jnp.concatenate([bt, kt], axis=1), form="nt")
    row = lax.broadcasted_iota(jnp.int32, (1, n2, n2), 1) % c
    col = lax.broadcasted_iota(jnp.int32, (1, n2, n2), 2) % c
    strict = row > col
    incl = row >= col
    l_ab = jnp.where(strict, quad[:, :n2, :n2], 0.0)
    a_rb = jnp.where(incl, quad[:, n2:, :n2], 0.0).astype(BF16)
    vmask = jnp.concatenate([jnp.broadcast_to(strict, (1, n2, n2)), jnp.broadcast_to(incl, (1, n2, n2))], axis=1)
    own = _mm(jnp.where(vmask, quad[:, :, n2:], 0.0).astype(BF16), vst)
    y0 = own[:, n2:]
    t_inv = _unit_lower_inverse(l_ab, c // SOLVE_BLOCK, p_inv).astype(BF16)
    t_both = _mm(t_inv, jnp.concatenate([at, own[:, :n2].astype(BF16)], axis=-1))
    u0 = t_both[:, :, LANES:]
    wr = jnp.concatenate([t_both[:, :, :LANES].astype(BF16), rt], axis=1)
    kv = _mm(vst, kh, form="tn")

    s = s_ref[...]
    for j in range(ns):
        gs = slice(j * n_pairs, (j + 1) * n_pairs)
        from_s = _mm(wr[gs], s.astype(BF16), form="nt")
        u = u0[gs] + from_s[:, :n2]
        u_b = u.astype(BF16)
        y_st = from_s[:, n2:] + _mm(a_rb[gs], u_b) + y0[gs]
        d_end = jnp.exp(jnp.stack([g_end[j][:, pair(pi)] for pi in range(n_pairs)]))
        s = s * d_end + _mm(u_b, bh[gs], form="tn") + kv[gs]
        y = y_st[:, :c] + y_st[:, c:]
        for pi in range(n_pairs):
            ybuf_ref[sub(j), pair(pi)] = y[pi]
    s_ref[...] = s

    y = ybuf_ref[...]
    inv_n = 1.0 / HEAD_DIM_RWKV
    mean = gsum(y) * inv_n
    dlt = y - mean
    var = gsum(dlt * dlt) * inv_n
    yn = dlt * lax.rsqrt(var + RWKV_GN_EPS) * lnw_ref[...] + lnb_ref[...]
    bonus = gsum(r * k2 * rk_ref[...]) * v
    y_ref[0] = ((yn + bonus) * g).astype(y_ref.dtype)


def _rwkv_mix(p3d, v_first, mu_pad, w0, wbw, a0, wba, wbg, v0, wbv, k_k, k_a, r_k, ln_w, ln_b, rows=256, chunk=64):
    bsz, t, _ = p3d.shape
    rows = min(rows, t)
    has_vres = v_first is not None
    n_lora = PREC["lora"][1]
    row = lambda z: z.reshape(1, -1).astype(F32)
    lane = jnp.arange(LANES)
    gmat = ((lane[:, None] // HEAD_DIM_RWKV) == (lane[None, :] // HEAD_DIM_RWKV)).astype(BF16)
    zeros = jnp.zeros((LORA_DECAY, W_RWKV), F32)
    wbw_pad = _split_bf16(jnp.concatenate([wbw, zeros], axis=0), n_lora)
    wba_pad = _split_bf16(jnp.concatenate([zeros, wba], axis=0), n_lora)
    const = lambda shape: pl.BlockSpec(shape, lambda b, c: (0,) * len(shape))
    lora_spec = const((n_lora, LANES, W_RWKV))
    p_spec = pl.BlockSpec((1, rows, RW_BLOCK), lambda b, c: (b, c, 0))
    seq_spec = pl.BlockSpec((1, rows, W_RWKV), lambda b, c: (b, c, 0))
    args = [p3d]
    specs = [p_spec]
    if has_vres:
        args.append(v_first)
        specs.append(seq_spec)
    args += [mu_pad, row(w0), wbw_pad, row(a0), wba_pad, _split_bf16(wbg, n_lora)]
    specs += [const((1, RW_BLOCK)), const((1, W_RWKV)), lora_spec, const((1, W_RWKV)), lora_spec, lora_spec]
    if has_vres:
        wbv_pad = jnp.concatenate([wbv, jnp.zeros((LANES - LORA_VRES, W_RWKV), F32)], axis=0)
        args += [row(v0), _split_bf16(wbv_pad, n_lora)]
        specs += [const((1, W_RWKV)), lora_spec]
    args += [row(k_k), row(k_a), row(r_k), row(ln_w), row(ln_b), gmat, _block_tri(rows, chunk)]
    specs += [const((1, W_RWKV))] * 5 + [const((LANES, LANES)), const((rows, rows))]
    y_shape = jax.ShapeDtypeStruct((bsz, t, W_RWKV), BF16)
    if has_vres:
        out_shape, out_specs = y_shape, seq_spec
    else:
        out_shape = (y_shape, jax.ShapeDtypeStruct((bsz, t, W_RWKV), F32))
        out_specs = (seq_spec, seq_spec)
    return pl.pallas_call(
        functools.partial(_rwkv_kernel, rows=rows, chunk=chunk, has_vres=has_vres),
        grid=(bsz, t // rows),
        in_specs=specs,
        out_specs=out_specs,
        out_shape=out_shape,
        scratch_shapes=[pltpu.VMEM((rows + 8, RW_BLOCK), F32),
                        pltpu.VMEM((W_RWKV // LANES, LANES, LANES), F32),
                        pltpu.VMEM((rows, W_RWKV), F32)],
        compiler_params=pltpu.CompilerParams(
            dimension_semantics=("parallel", "arbitrary"), vmem_limit_bytes=VMEM_LIMIT),
    )(*args)


def _gdn_kernel(x_ref, ab_ref, convw_ref, hp_ref, normw_ref, tri_ref, eye_ref, ones_ref,
                y_ref, buf_ref, s_ref, obuf_ref, *, rows, chunk):
    c = chunk
    ns = rows // c
    nh = N_HEADS_GDN
    wq = 3 * W_GDN
    p_ones, p_inv = PREC["ones"], PREC["inv"]
    head = lambda h: slice(h * LANES, (h + 1) * LANES)
    sub = lambda j: slice(j * c, (j + 1) * c)

    @pl.when(pl.program_id(1) == 0)
    def _():
        buf_ref[0:8, :] = jnp.zeros((8, wq), F32)
        s_ref[...] = jnp.zeros_like(s_ref)

    x = x_ref[0]
    xc = x[:, :wq]
    buf_ref[8:8 + rows, :] = xc
    conv = xc * convw_ref[3:4, :]
    for i in range(CONV_WIDTH - 1):
        conv = conv + buf_ref[5 + i:5 + i + rows, :] * convw_ref[i:i + 1, :]
    buf_ref[0:8, :] = xc[rows - 8:rows, :]
    qkv = conv * _sigmoid(conv)

    ab = ab_ref[0]
    g_all = -jnp.exp(hp_ref[0:1, :]) * _softplus(ab + hp_ref[1:2, :])
    beta_all = _sigmoid(ab)
    gc_all = _mm(tri_ref[...], g_all, p_ones)
    gc_t = _mm(eye_ref[...], gc_all, p_ones, "nt")
    ones = ones_ref[...]

    def l2n(z):
        return z * lax.rsqrt(_mm(z * z, ones, (2, 1)) + NORM_EPS)

    chains = [(j, h) for j in range(ns) for h in range(nh)]
    k_l, kb_l, q_l, vk_l, qe_l, ke_l, gcol_l, grow_l, gend_l = [], [], [], [], [], [], [], [], []
    for h in range(nh):
        q = l2n(qkv[:, head(h)]) * (HEAD_DIM_GDN ** -0.5)
        k = l2n(qkv[:, W_GDN + h * LANES:W_GDN + (h + 1) * LANES])
        v = qkv[:, 2 * W_GDN + h * LANES:2 * W_GDN + (h + 1) * LANES]
        gcol = gc_all[:, h:h + 1]
        beta = beta_all[:, nh + h:nh + h + 1]
        g_end = [gcol[j * c + c - 1:j * c + c, :] for j in range(ns)]
        e_gc = jnp.exp(gcol)
        e_end = jnp.exp(jnp.concatenate([g_end[j] - gcol[sub(j), :] for j in range(ns)], axis=0))
        kb = k * beta
        k_l.append(k.astype(BF16))
        kb_l.append(kb.astype(BF16))
        q_l.append(q.astype(BF16))
        vk_l.append(jnp.concatenate([v * beta, kb * e_gc], axis=1).astype(BF16))
        qe_l.append((q * e_gc).astype(BF16))
        ke_l.append((k * e_end).astype(BF16))
        gcol_l.append(gcol)
        grow_l.append(gc_t[h:h + 1, :])
        gend_l.append(g_end)
    gather = lambda lst: jnp.stack([lst[h][sub(j), :] for j, h in chains])
    k_b, kb_b, q_b, vk_b, qe_b, ke_b = (gather(l) for l in (k_l, kb_l, q_l, vk_l, qe_l, ke_l))
    gcol_b = gather(gcol_l)
    grow_b = jnp.stack([grow_l[h][:, sub(j)] for j, h in chains])

    row = lax.broadcasted_iota(jnp.int32, (1, c, c), 1)
    col = lax.broadcasted_iota(jnp.int32, (1, c, c), 2)
    decay = jnp.exp(jnp.where(row >= col, gcol_b - grow_b, -jnp.inf))
    both = _mm(jnp.concatenate([kb_b, q_b], axis=1), k_b, form="nt")
    kkt, qkt = both[:, :c], both[:, c:]
    t_inv = _unit_lower_inverse(-jnp.where(row > col, kkt * decay, 0.0), c // SOLVE_BLOCK, p_inv).astype(BF16)
    uw = _mm(t_inv, vk_b)
    u = uw[:, :, :LANES]
    wq_b = jnp.concatenate([uw[:, :, LANES:].astype(BF16), qe_b], axis=1)
    attn = (qkt * decay).astype(BF16)

    s = s_ref[...]
    for j in range(ns):
        gs = slice(j * nh, (j + 1) * nh)
        ws_qs = _mm(wq_b[gs], s.astype(BF16))
        v_new = (u[gs] - ws_qs[:, :c]).astype(BF16)
        o = ws_qs[:, c:] + _mm(attn[gs], v_new)
        d_end = jnp.exp(jnp.stack([gend_l[h][j] for h in range(nh)]))
        s = s * d_end + _mm(ke_b[gs], v_new, form="tn")
        for h in range(nh):
            obuf_ref[sub(j), head(h)] = o[h]
    s_ref[...] = s

    o = obuf_ref[...]
    ms = jnp.concatenate([_mm(o[:, head(h)] * o[:, head(h)], ones, (2, 1)) for h in range(nh)], axis=1)
    on = o * lax.rsqrt(ms * (1.0 / HEAD_DIM_GDN) + NORM_EPS) * normw_ref[...]
    z = x[:, wq:]
    y_ref[0] = (on * (z * _sigmoid(z))).astype(y_ref.dtype)


def _gdn_mix(p3d, conv_w, a_log, dt_bias, norm_w, rows=256, chunk=128):
    bsz, t, _ = p3d.shape
    rows = min(rows, t)
    pad = jnp.zeros((LANES - N_HEADS_GDN,), F32)
    hp = jnp.stack([jnp.concatenate([a_log.astype(F32), pad]), jnp.concatenate([dt_bias.astype(F32), pad])])
    eye = jnp.eye(LANES, dtype=BF16)
    ones = jnp.ones((LANES, LANES), BF16)
    normw = jnp.tile(norm_w.astype(F32), N_HEADS_GDN).reshape(1, W_GDN)
    const = lambda shape: pl.BlockSpec(shape, lambda b, c: (0,) * len(shape))
    return pl.pallas_call(
        functools.partial(_gdn_kernel, rows=rows, chunk=chunk),
        grid=(bsz, t // rows),
        in_specs=[
            pl.BlockSpec((1, rows, 4 * W_GDN), lambda b, c: (b, c, COL_GDN // (4 * W_GDN))),
            pl.BlockSpec((1, rows, LANES), lambda b, c: (b, c, COL_GDN_AB // LANES)),
            const((CONV_WIDTH, 3 * W_GDN)), const((2, LANES)), const((1, W_GDN)),
            const((rows, rows)), const((LANES, LANES)), const((LANES, LANES)),
        ],
        out_specs=pl.BlockSpec((1, rows, W_GDN), lambda b, c: (b, c, 0)),
        out_shape=jax.ShapeDtypeStruct((bsz, t, W_GDN), BF16),
        scratch_shapes=[pltpu.VMEM((rows + 8, 3 * W_GDN), F32),
                        pltpu.VMEM((N_HEADS_GDN, HEAD_DIM_GDN, HEAD_DIM_GDN), F32),
                        pltpu.VMEM((rows, W_GDN), F32)],
        compiler_params=pltpu.CompilerParams(
            dimension_semantics=("parallel", "arbitrary"), vmem_limit_bytes=VMEM_LIMIT),
    )(p3d, p3d, conv_w.astype(F32), hp, normw, _block_tri(rows, chunk), eye, ones)


def _diff_attn_kernel(q_ref, k_ref, v_ref, qw_ref, kw_ref, gmat_ref, eye_ref, lam_ref, subw_ref, o_ref,
                      qt_ref, kn_ref, vt_ref, *, tq, tk, lambda_init):
    t = k_ref.shape[1]
    gmat = gmat_ref[...]
    eye = eye_ref[...]

    def normed(ref, wref, rows):
        x = ref[0, rows, :]
        ms = _mm(x * x, gmat, PREC["gsum"]) * (1.0 / HEAD_DIM_DIFF)
        return (x * lax.rsqrt(ms + NORM_EPS) * wref[...]).astype(BF16)

    for r0 in range(0, t, tk):
        rows = slice(r0, r0 + tk)
        kn_ref[rows, :] = normed(k_ref, kw_ref, rows)
        qt_ref[:, rows] = _mm(eye, normed(q_ref, qw_ref, rows), form="nt").astype(BF16)
        vt_ref[:, rows] = _mm(eye, v_ref[0, rows, :].astype(BF16), form="nt").astype(BF16)

    feat = lax.broadcasted_iota(jnp.int32, (LANES, 1), 0)
    lv = lam_ref[...]
    lam = (jnp.exp(jnp.sum(lv[0:1] * lv[1:2], axis=-1, keepdims=True))
           - jnp.exp(jnp.sum(lv[2:3] * lv[3:4], axis=-1, keepdims=True)) + lambda_init)

    tiles = []
    for i in range(t // tq):
        q_end = (i + 1) * tq
        for k0 in range(0, q_end, tk):
            klen = min(tk, q_end - k0)
            tiles.append((i, k0, klen, k0 + klen > i * tq))

    q_maps = {}

    def scores(i, k0, klen):
        if i not in q_maps:
            q_t = qt_ref[:, i * tq:(i + 1) * tq]
            zero = jnp.zeros_like(q_t)
            q_maps.clear()
            q_maps[i] = (jnp.where(feat < HEAD_DIM_DIFF, q_t, zero), jnp.where(feat >= HEAD_DIM_DIFF, q_t, zero))
        k = kn_ref[k0:k0 + klen, :]
        return [_dot(k, qm) for qm in q_maps[i]]

    state = None
    pending = scores(*tiles[0][:3])
    for n, (i, k0, klen, diag) in enumerate(tiles):
        cur = pending
        if n + 1 < len(tiles):
            pending = scores(*tiles[n + 1][:3])
        v_t = vt_ref[:, k0:k0 + klen]
        new_state = []
        for mp, s in enumerate(cur):
            if diag:
                key = lax.broadcasted_iota(jnp.int32, (klen, tq), 0)
                qry = lax.broadcasted_iota(jnp.int32, (klen, tq), 1)
                s = jnp.where(key - qry <= i * tq - k0, s, -jnp.inf)
            if k0 == 0:
                m_new = jnp.max(s, axis=0, keepdims=True)
                p = jnp.exp2(s - m_new)
                l_new = jnp.sum(p, axis=0, keepdims=True)
                acc_new = _dot(v_t, p.astype(BF16))
            else:
                m_prev, l_prev, acc_prev = state[mp]
                m_new = jnp.maximum(m_prev, jnp.max(s, axis=0, keepdims=True))
                alpha = jnp.exp2(m_prev - m_new)
                p = jnp.exp2(s - m_new)
                l_new = alpha * l_prev + jnp.sum(p, axis=0, keepdims=True)
                acc_new = alpha * acc_prev + _dot(v_t, p.astype(BF16))
            new_state.append((m_new, l_new, acc_new))
        state = new_state
        if k0 + klen == (i + 1) * tq:
            (_, l0, acc0), (_, l1, acc1) = state
            o_t = acc0 * (1.0 / l0) - lam * (acc1 * (1.0 / l1))
            ms = jnp.sum(o_t * o_t, axis=0, keepdims=True) * (1.0 / LANES)
            on_t = (o_t * lax.rsqrt(ms + NORM_EPS) * subw_ref[...] * (1.0 - lambda_init)).astype(BF16)
            o_ref[0, i * tq:(i + 1) * tq, :] = _mm(on_t, eye, form="tn").astype(o_ref.dtype)


def _diff_attn(p3d, q_norm_w, k_norm_w, lam_vecs, subln_w, lambda_init, tq=512, tk=1024):
    bsz, t, _ = p3d.shape
    tk = min(tk, t)
    tq = min(tq, tk)
    lane = jnp.arange(LANES)
    gmat = ((lane[:, None] // HEAD_DIM_DIFF) == (lane[None, :] // HEAD_DIM_DIFF)).astype(BF16)
    qw = (jnp.tile(q_norm_w.astype(F32), 2) * (HEAD_DIM_DIFF ** -0.5 * math.log2(math.e))).reshape(1, LANES)
    kw = jnp.tile(k_norm_w.astype(F32), 2).reshape(1, LANES)
    base = COL_DIFF // LANES
    sec = lambda j: pl.BlockSpec((1, t, LANES), lambda b, h: (b, 0, base + j * N_HEADS_DIFF + h))
    const = lambda shape: pl.BlockSpec(shape, lambda b, h: (0,) * len(shape))
    return pl.pallas_call(
        functools.partial(_diff_attn_kernel, tq=tq, tk=tk, lambda_init=lambda_init),
        grid=(bsz, N_HEADS_DIFF),
        in_specs=[sec(0), sec(1), sec(2), const((1, LANES)), const((1, LANES)),
                  const((LANES, LANES)), const((LANES, LANES)), const((4, HEAD_DIM_DIFF)), const((LANES, 1))],
        out_specs=pl.BlockSpec((1, t, LANES), lambda b, h: (b, 0, h)),
        out_shape=jax.ShapeDtypeStruct((bsz, t, W_DIFF), BF16),
        scratch_shapes=[pltpu.VMEM((LANES, t), BF16), pltpu.VMEM((t, LANES), BF16), pltpu.VMEM((LANES, t), BF16)],
        compiler_params=pltpu.CompilerParams(
            dimension_semantics=("parallel", "parallel"), vmem_limit_bytes=VMEM_LIMIT),
    )(p3d, p3d, p3d, qw, kw, gmat, jnp.eye(LANES, dtype=BF16), lam_vecs, subln_w.reshape(LANES, 1).astype(F32))


def _mix_ffn_kernel(x_ref, yr_ref, yg_ref, yd_ref, w1_ref, w2_ref, w3_ref, nw_ref, wg_ref, wu_ref, wo_ref,
                    o_ref, h_ref):
    @pl.when(pl.program_id(1) == 0)
    def _():
        x1 = (x_ref[...] + _dot(yr_ref[...], w1_ref[...]) + _dot(yg_ref[...], w2_ref[...])
              + _dot(yd_ref[...], w3_ref[...]))
        ms = jnp.mean(x1 * x1, axis=-1, keepdims=True)
        h_ref[...] = (x1 * lax.rsqrt(ms + NORM_EPS) * nw_ref[...]).astype(BF16)
        o_ref[...] = x1

    h = h_ref[...]
    gate = _dot(h, wg_ref[...])
    up = _dot(h, wu_ref[...])
    act = (gate * _sigmoid(gate) * up).astype(BF16)
    o_ref[...] += _dot(act, wo_ref[...])


def _mix_ffn(x2d, y_rw, y_gdn, y_diff, w_mix_bf16, nw, w_in_bf16, w_out_bf16, layer, tm=512, tf=512):
    n, d = x2d.shape
    hidden = w_out_bf16.shape[1]
    nf = hidden // tf
    rowblk = lambda w: pl.BlockSpec((tm, w), lambda i, f: (i, 0))
    resident = lambda rows, blk: pl.BlockSpec((None, rows, d), lambda i, f: (layer, blk, 0),
                                              pipeline_mode=pl.Buffered(1))
    return pl.pallas_call(
        _mix_ffn_kernel,
        grid=(n // tm, nf),
        in_specs=[
            rowblk(d), rowblk(W_RWKV), rowblk(W_GDN), rowblk(W_DIFF),
            resident(W_RWKV, 0), resident(W_GDN, 1), resident(W_DIFF, 1),
            pl.BlockSpec((1, d), lambda i, f: (0, 0)),
            pl.BlockSpec((None, d, tf), lambda i, f: (layer, 0, f)),
            pl.BlockSpec((None, d, tf), lambda i, f: (layer, 0, nf + f)),
            pl.BlockSpec((None, tf, d), lambda i, f: (layer, f, 0)),
        ],
        out_specs=pl.BlockSpec((tm, d), lambda i, f: (i, 0)),
        out_shape=jax.ShapeDtypeStruct((n, d), F32),
        scratch_shapes=[pltpu.VMEM((tm, d), BF16)],
        compiler_params=pltpu.CompilerParams(
            dimension_semantics=("parallel", "arbitrary"), vmem_limit_bytes=VMEM_LIMIT),
    )(x2d, y_rw, y_gdn, y_diff, w_mix_bf16, w_mix_bf16, w_mix_bf16, nw.reshape(1, d),
      w_in_bf16, w_in_bf16, w_out_bf16)


def _pack_in_proj(w_in_all, layer, w_vres_l):
    d = w_in_all.shape[1]
    vres = jnp.zeros((d, LANES), F32)
    if w_vres_l is not None:
        vres = vres.at[:, :LORA_VRES].set(w_vres_l)
    rb = 64
    return pl.pallas_call(
        _pack_w_kernel,
        grid=(d // rb,),
        in_specs=[pl.BlockSpec((None, rb, N_IN), lambda i: (layer, i, 0)),
                  pl.BlockSpec((rb, LANES), lambda i: (i, 0))],
        out_specs=(pl.BlockSpec((rb, COL_GDN), lambda i: (i, 0)),
                   pl.BlockSpec((rb, 4 * W_GDN), lambda i: (i, 0)),
                   pl.BlockSpec((rb, DIFF_COLS), lambda i: (i, 0))),
        out_shape=(jax.ShapeDtypeStruct((d, COL_GDN), BF16), jax.ShapeDtypeStruct((d, 4 * W_GDN), BF16),
                   jax.ShapeDtypeStruct((d, DIFF_COLS), BF16)),
        compiler_params=pltpu.CompilerParams(dimension_semantics=("parallel",), vmem_limit_bytes=VMEM_LIMIT),
    )(w_in_all, vres)


def _pack_w_kernel(w_ref, vres_ref, wa_ref, wg_ref, wd_ref):
    gdn0 = RWKV_COLS
    ab0 = gdn0 + 4 * W_GDN
    lane = lax.broadcasted_iota(jnp.int32, (1, LANES), 1)
    wa_ref[:, :RWKV_COLS] = w_ref[:, :RWKV_COLS].astype(BF16)
    wa_ref[:, COL_VRES:COL_GDN_AB] = vres_ref[...].astype(BF16)
    wa_ref[:, COL_GDN_AB:] = jnp.where(lane < 2 * N_HEADS_GDN, w_ref[:, ab0:ab0 + LANES], 0.0).astype(BF16)
    wg_ref[...] = w_ref[:, gdn0:ab0].astype(BF16)
    wd_ref[...] = w_ref[:, gdn0 + GDN_COLS:].astype(BF16)


def kernel(x, attn_norm_w, w_in, w_vres_a, mu_rwkv, mu_vres, rwkv_w0, rwkv_w_lora_b, rwkv_a0, rwkv_a_lora_b, rwkv_g_lora_b, rwkv_v0, rwkv_v_lora_b, rwkv_k_k, rwkv_k_a, rwkv_r_k, rwkv_ln_w, rwkv_ln_b, gdn_conv_w, gdn_A_log, gdn_dt_bias, gdn_norm_w, diff_q_norm_w, diff_k_norm_w, diff_lambda_q1, diff_lambda_k1, diff_lambda_q2, diff_lambda_k2, diff_subln_w, w_out, ffn_norm_w, w_ffn_in, w_ffn_out):
    bsz, t, d = x.shape
    depth = w_in.shape[0]
    x2d = x.reshape(bsz * t, d)
    v_first = None
    w_out_b, w_ffn_in_b, w_ffn_out_b = w_out.astype(BF16), w_ffn_in.astype(BF16), w_ffn_out.astype(BF16)
    for l in range(depth):
        w_proj = _pack_in_proj(w_in, l, None if l == 0 else w_vres_a[l - 1])
        mu_tail = jnp.zeros((RW_BLOCK - RWKV_COLS,), F32)
        if l > 0:
            mu_tail = mu_tail.at[:LORA_VRES].set(mu_vres[l - 1])
        mu_pad = jnp.concatenate([mu_rwkv[l], mu_tail]).reshape(1, RW_BLOCK)
        p3d = _norm_matmul(x2d, attn_norm_w[l], w_proj).reshape(bsz, t, N_PAD)
        rw_args = (rwkv_w0[l], rwkv_w_lora_b[l], rwkv_a0[l], rwkv_a_lora_b[l], rwkv_g_lora_b[l])
        rw_tail = (rwkv_k_k[l], rwkv_k_a[l], rwkv_r_k[l], rwkv_ln_w[l], rwkv_ln_b[l])
        if l == 0:
            y_rw, v_first = _rwkv_mix(p3d, None, mu_pad, *rw_args, None, None, *rw_tail)
        else:
            y_rw = _rwkv_mix(p3d, v_first, mu_pad, *rw_args, rwkv_v0[l - 1], rwkv_v_lora_b[l - 1], *rw_tail)
        y_gdn = _gdn_mix(p3d, gdn_conv_w[l], gdn_A_log[l], gdn_dt_bias[l], gdn_norm_w[l])
        lambda_init = 0.8 - 0.6 * math.exp(-0.3 * l)
        lam_vecs = jnp.stack([diff_lambda_q1[l], diff_lambda_k1[l], diff_lambda_q2[l], diff_lambda_k2[l]]).astype(F32)
        y_diff = _diff_attn(p3d, diff_q_norm_w[l], diff_k_norm_w[l], lam_vecs, diff_subln_w[l], lambda_init)
        n = bsz * t
        x2d = _mix_ffn(x2d, y_rw.reshape(n, W_RWKV), y_gdn.reshape(n, W_GDN), y_diff.reshape(n, W_DIFF),
                       w_out_b, ffn_norm_w[l], w_ffn_in_b, w_ffn_out_b, l)
    return x2d.reshape(bsz, t, d)
```

```python
import functools
import math

import jax
import jax.numpy as jnp
from jax import lax
from jax.experimental import pallas as pl
from jax.experimental.pallas import tpu as pltpu

F32 = jnp.float32
BF16 = jnp.bfloat16

D_MODEL = 2048
W_RWKV = D_MODEL // 4
HEAD_DIM_RWKV = 64
LORA_DECAY = 64
LORA_ICLR = 64
LORA_VRES = 32
LORA_GATE = 128
RWKV_GN_EPS = 64e-5
W_GDN = D_MODEL // 4
HEAD_DIM_GDN = 128
N_HEADS_GDN = W_GDN // HEAD_DIM_GDN
CONV_WIDTH = 4
W_DIFF = D_MODEL // 2
HEAD_DIM_DIFF = 64
N_HEADS_DIFF = W_DIFF // (2 * HEAD_DIM_DIFF)
RWKV_COLS = 3 * W_RWKV + LORA_DECAY + LORA_ICLR + LORA_GATE
GDN_COLS = 4 * W_GDN + 2 * N_HEADS_GDN
DIFF_COLS = 3 * W_DIFF
N_IN = RWKV_COLS + GDN_COLS + DIFF_COLS
FFN_HIDDEN = -(-8 * D_MODEL // (3 * 256)) * 256
NORM_EPS = 1e-6

LANES = 128
SOLVE_BLOCK = 16
NORM_CHUNK = 256

COL_VRES = RWKV_COLS
COL_GDN_AB = COL_VRES + LANES
COL_GDN = COL_GDN_AB + LANES
COL_DIFF = COL_GDN + 4 * W_GDN
N_PAD = COL_DIFF + DIFF_COLS
RW_BLOCK = COL_GDN

VMEM_LIMIT = 56 * 1024 * 1024

PREC = {
    "lora": (1, 1),
    "ones": (1, 3),
    "gsum": (1, 1),
    "inv": (1, 1),
}


def _pieces(x, n):
    if isinstance(x, (list, tuple)):
        return list(x)[:n]
    if x.dtype == BF16:
        return [x]
    out, rem = [], x
    for i in range(n):
        piece = rem.astype(BF16)
        out.append(piece)
        if i + 1 < n:
            rem = rem - piece.astype(F32)
    return out


_DIMS = {"nn": (((1,), (0,)), ((), ())), "nt": (((1,), (1,)), ((), ())), "tn": (((0,), (0,)), ((), ()))}
_DIMS_B = {"nn": (((2,), (1,)), ((0,), (0,))), "nt": (((2,), (2,)), ((0,), (0,))), "tn": (((1,), (1,)), ((0,), (0,)))}


def _mm(a, b, prec=(1, 1), form="nn"):
    pa, pb = _pieces(a, prec[0]), _pieces(b, prec[1])
    dims = (_DIMS_B if pa[0].ndim == 3 else _DIMS)[form]
    depth = max(len(pa), len(pb))
    terms = sorted(((i, j) for i in range(len(pa)) for j in range(len(pb)) if i + j < depth),
                   key=lambda ij: -(ij[0] + ij[1]))
    acc = None
    for i, j in terms:
        d = lax.dot_general(pa[i], pb[j], dims, preferred_element_type=F32)
        acc = d if acc is None else acc + d
    return acc


def _dot(a, b):
    return jnp.dot(a, b, preferred_element_type=F32)


def _sigmoid(x):
    return 1.0 / (1.0 + jnp.exp(-x))


def _softplus(x):
    return jnp.maximum(x, 0.0) + jnp.log(1.0 + jnp.exp(-jnp.abs(x)))


def _split_bf16(w, n):
    out, rem = [], w.astype(F32)
    for _ in range(n):
        piece = rem.astype(BF16)
        out.append(piece)
        rem = rem - piece.astype(F32)
    return jnp.stack(out)


def _block_tri(rows, chunk):
    idx = jnp.arange(rows)
    same = (idx[:, None] // chunk) == (idx[None, :] // chunk)
    return (same & (idx[:, None] >= idx[None, :])).astype(BF16)


def _unit_lower_inverse(lo_tri, n_sub, prec):
    n = lo_tri.shape[-1]
    lead = (1,) * (lo_tri.ndim - 2)
    row = lax.broadcasted_iota(jnp.int32, lead + (n, n), lo_tri.ndim - 2)
    col = lax.broadcasted_iota(jnp.int32, lead + (n, n), lo_tri.ndim - 1)
    eye = (row == col).astype(F32)
    same = (row // SOLVE_BLOCK) == (col // SOLVE_BLOCK)
    l_diag = jnp.where(same, lo_tri, 0.0)
    l_off = jnp.where(same, 0.0, lo_tri)

    def neumann(a, s, order):
        width = 1
        while width < order:
            if 2 * width < order:
                both = _mm(a, jnp.concatenate([s, a], axis=-1), prec)
                s = s + both[..., :n]
                a = both[..., n:]
            else:
                s = s + _mm(a, s, prec)
            width *= 2
        return s

    t_diag = neumann(_mm(l_diag, l_diag, prec), eye + l_diag, SOLVE_BLOCK // 2)
    return neumann(_mm(t_diag, l_off, prec), t_diag, n_sub)


def _norm_matmul_kernel(x_ref, nw_ref, wa_ref, wg_ref, wd_ref, o_ref, h_ref, *, na, ng):
    j = pl.program_id(1)

    @pl.when(j == 0)
    def _():
        for r0 in range(0, x_ref.shape[0], NORM_CHUNK):
            rows = slice(r0, r0 + NORM_CHUNK)
            x = x_ref[rows, :]
            ms = jnp.mean(x * x, axis=-1, keepdims=True)
            h = (x * lax.rsqrt(ms + NORM_EPS) * nw_ref[...]).astype(BF16)
            h_ref[rows, :] = h
            o_ref[rows, :] = _dot(h, wa_ref[...])

    @pl.when((j > 0) & (j < na))
    def _():
        o_ref[...] = _dot(h_ref[...], wa_ref[...])

    @pl.when((j >= na) & (j < na + ng))
    def _():
        o_ref[...] = _dot(h_ref[...], wg_ref[...])

    @pl.when(j >= na + ng)
    def _():
        o_ref[...] = _dot(h_ref[...], wd_ref[...])


def _norm_matmul(x2d, nw, w_sections, tm=1024, tn=1024):
    n, d = x2d.shape
    tm = min(tm, n)
    wa, wg, wd = w_sections
    na, ng, nd = wa.shape[1] // tn, wg.shape[1] // tn, wd.shape[1] // tn
    return pl.pallas_call(
        functools.partial(_norm_matmul_kernel, na=na, ng=ng),
        grid=(n // tm, na + ng + nd),
        in_specs=[
            pl.BlockSpec((tm, d), lambda i, j: (i, 0)),
            pl.BlockSpec((1, d), lambda i, j: (0, 0)),
            pl.BlockSpec((d, tn), lambda i, j: (0, jnp.minimum(j, na - 1))),
            pl.BlockSpec((d, tn), lambda i, j: (0, jnp.clip(j - na, 0, ng - 1))),
            pl.BlockSpec((d, tn), lambda i, j: (0, jnp.maximum(j - na - ng, 0))),
        ],
        out_specs=pl.BlockSpec((tm, tn), lambda i, j: (i, j)),
        out_shape=jax.ShapeDtypeStruct((n, (na + ng + nd) * tn), F32),
        scratch_shapes=[pltpu.VMEM((tm, d), BF16)],
        compiler_params=pltpu.CompilerParams(
            dimension_semantics=("parallel", "arbitrary"), vmem_limit_bytes=VMEM_LIMIT),
    )(x2d, nw.reshape(1, d), wa, wg, wd)


def _rwkv_kernel(*refs, rows, chunk, has_vres):
    if has_vres:
        (p_ref, vf_ref, mu_ref, w0_ref, wbw_ref, a0_ref, wba_ref, wbg_ref, v0_ref, wbv_ref,
         kk_ref, ka_ref, rk_ref, lnw_ref, lnb_ref, gmat_ref, tri_ref,
         y_ref, buf_ref, s_ref, ybuf_ref) = refs
    else:
        (p_ref, mu_ref, w0_ref, wbw_ref, a0_ref, wba_ref, wbg_ref,
         kk_ref, ka_ref, rk_ref, lnw_ref, lnb_ref, gmat_ref, tri_ref,
         y_ref, vout_ref, buf_ref, s_ref, ybuf_ref) = refs
    c = chunk
    ns = rows // c
    n_pairs = W_RWKV // LANES
    p_lora, p_ones, p_gsum, p_inv = PREC["lora"], PREC["ones"], PREC["gsum"], PREC["inv"]
    wpieces = lambda ref: [ref[i] for i in range(ref.shape[0])]
    pair = lambda pi: slice(pi * LANES, (pi + 1) * LANES)
    sub = lambda j: slice(j * c, (j + 1) * c)

    @pl.when(pl.program_id(1) == 0)
    def _():
        buf_ref[0:8, :] = jnp.zeros((8, RW_BLOCK), F32)
        s_ref[...] = jnp.zeros_like(s_ref)

    x = p_ref[0]
    buf_ref[8:8 + rows, :] = x
    prev = buf_ref[7:7 + rows, :]
    buf_ref[0:8, :] = x[rows - 8:rows, :]
    pm = x + (prev - x) * mu_ref[...]

    r = pm[:, 0:W_RWKV]
    k = pm[:, W_RWKV:2 * W_RWKV]
    v = pm[:, 2 * W_RWKV:3 * W_RWKV]
    lw = pm[:, 3 * W_RWKV:3 * W_RWKV + LANES]
    xg = pm[:, 3 * W_RWKV + LANES:3 * W_RWKV + 2 * LANES]
    w_log = -_softplus(-(w0_ref[...] + _mm(jnp.tanh(lw), wpieces(wbw_ref), p_lora))) - 0.5
    logd = -jnp.exp(w_log)
    a = _sigmoid(a0_ref[...] + _mm(lw, wpieces(wba_ref), p_lora))
    g = _mm(_sigmoid(xg), wpieces(wbg_ref), p_lora)
    if has_vres:
        xv = pm[:, COL_VRES:COL_VRES + LANES]
        v = v + (vf_ref[0] - v) * _sigmoid(v0_ref[...] + _mm(xv, wpieces(wbv_ref), p_lora))
    else:
        vout_ref[0] = v

    gmat = gmat_ref[...]

    def gsum(z):
        return jnp.concatenate([_mm(z[:, pair(pi)], gmat, p_gsum) for pi in range(n_pairs)], axis=1)

    kk = k * kk_ref[...]
    kk = kk * lax.rsqrt(gsum(kk * kk) + NORM_EPS)
    k2 = k * (1.0 + (a - 1.0) * ka_ref[...])
    b = kk * a
    gc = _mm(tri_ref[...], logd, p_ones)
    g_end = [gc[j * c + c - 1:j * c + c, :] for j in range(ns)]
    e_end = jnp.exp(jnp.concatenate([g_end[j] - gc[sub(j), :] for j in range(ns)], axis=0))
    e_inv = jnp.exp(-gc)

    lane = lax.broadcasted_iota(jnp.int32, (1, W_RWKV), 1)
    h0 = ((lane // HEAD_DIM_RWKV) % 2 == 0).astype(F32)
    h1 = 1.0 - h0

    def stack_all(z):
        z0 = (z * h0).astype(BF16)
        z1 = (z * h1).astype(BF16)
        return jnp.stack([jnp.concatenate([z0[sub(j), pair(pi)], z1[sub(j), pair(pi)]], axis=0)
                          for j in range(ns) for pi in range(n_pairs)])

    at = stack_all(-kk * jnp.exp(gc - logd))
    rt = stack_all(r * jnp.exp(gc))
    bt = stack_all(b * e_inv)
    kt = stack_all(k2 * e_inv)
    vst = stack_all(v)
    bh = stack_all(b * e_end)
    kh = stack_all(k2 * e_end)

    n2 = 2 * c
    quad = _mm(jnp.concatenate([at, rt], axis=1), jnp.concatenate([bt, kt], axis=1), form="nt")
    row = lax.broadcasted_iota(jnp.int32, (1, n2, n2), 1) % c
    col = lax.broadcasted_iota(jnp.int32, (1, n2, n2), 2) % c
    strict = row > col
    incl = row >= col
    l_ab = jnp.where(strict, quad[:, :n2, :n2], 0.0)
    a_rb = jnp.where(incl, quad[:, n2:, :n2], 0.0).astype(BF16)
    vmask = jnp.concatenate([jnp.broadcast_to(strict, (1, n2, n2)), jnp.broadcast_to(incl, (1, n2, n2))], axis=1)
    own = _mm(jnp.where(vmask, quad[:, :, n2:], 0.0).astype(BF16), vst)
    y0 = own[:, n2:]
    t_inv = _unit_lower_inverse(l_ab, c // SOLVE_BLOCK, p_inv).astype(BF16)
    t_both = _mm(t_inv, jnp.concatenate([at, own[:, :n2].astype(BF16)], axis=-1))
    u0 = t_both[:, :, LANES:]
    wr = jnp.concatenate([t_both[:, :, :LANES].astype(BF16), rt], axis=1)
    kv = _mm(vst, kh, form="tn")

    s = s_ref[...]
    for j in range(ns):
        gs = slice(j * n_pairs, (j + 1) * n_pairs)
        from_s = _mm(wr[gs], s.astype(BF16), form="nt")
        u = u0[gs] + from_s[:, :n2]
        u_b = u.astype(BF16)
        y_st = from_s[:, n2:] + _mm(a_rb[gs], u_b) + y0[gs]
        d_end = jnp.exp(jnp.stack([g_end[j][:, pair(pi)] for pi in range(n_pairs)]))
        s = s * d_end + _mm(u_b, bh[gs], form="tn") + kv[gs]
        y = y_st[:, :c] + y_st[:, c:]
        for pi in range(n_pairs):
            ybuf_ref[sub(j), pair(pi)] = y[pi]
    s_ref[...] = s

    y = ybuf_ref[...]
    inv_n = 1.0 / HEAD_DIM_RWKV
    mean = gsum(y) * inv_n
    dlt = y - mean
    var = gsum(dlt * dlt) * inv_n
    yn = dlt * lax.rsqrt(var + RWKV_GN_EPS) * lnw_ref[...] + lnb_ref[...]
    bonus = gsum(r * k2 * rk_ref[...]) * v
    y_ref[0] = ((yn + bonus) * g).astype(y_ref.dtype)


def _rwkv_mix(p3d, v_first, mu_pad, w0, wbw, a0, wba, wbg, v0, wbv, k_k, k_a, r_k, ln_w, ln_b, rows=256, chunk=64):
    bsz, t, _ = p3d.shape
    rows = min(rows, t)
    has_vres = v_first is not None
    n_lora = PREC["lora"][1]
    row = lambda z: z.reshape(1, -1).astype(F32)
    lane = jnp.arange(LANES)
    gmat = ((lane[:, None] // HEAD_DIM_RWKV) == (lane[None, :] // HEAD_DIM_RWKV)).astype(BF16)
    zeros = jnp.zeros((LORA_DECAY, W_RWKV), F32)
    wbw_pad = _split_bf16(jnp.concatenate([wbw, zeros], axis=0), n_lora)
    wba_pad = _split_bf16(jnp.concatenate([zeros, wba], axis=0), n_lora)
    const = lambda shape: pl.BlockSpec(shape, lambda b, c: (0,) * len(shape))
    lora_spec = const((n_lora, LANES, W_RWKV))
    p_spec = pl.BlockSpec((1, rows, RW_BLOCK), lambda b, c: (b, c, 0))
    seq_spec = pl.BlockSpec((1, rows, W_RWKV), lambda b, c: (b, c, 0))
    args = [p3d]
    specs = [p_spec]
    if has_vres:
        args.append(v_first)
        specs.append(seq_spec)
    args += [mu_pad, row(w0), wbw_pad, row(a0), wba_pad, _split_bf16(wbg, n_lora)]
    specs += [const((1, RW_BLOCK)), const((1, W_RWKV)), lora_spec, const((1, W_RWKV)), lora_spec, lora_spec]
    if has_vres:
        wbv_pad = jnp.concatenate([wbv, jnp.zeros((LANES - LORA_VRES, W_RWKV), F32)], axis=0)
        args += [row(v0), _split_bf16(wbv_pad, n_lora)]
        specs += [const((1, W_RWKV)), lora_spec]
    args += [row(k_k), row(k_a), row(r_k), row(ln_w), row(ln_b), gmat, _block_tri(rows, chunk)]
    specs += [const((1, W_RWKV))] * 5 + [const((LANES, LANES)), const((rows, rows))]
    y_shape = jax.ShapeDtypeStruct((bsz, t, W_RWKV), BF16)
    if has_vres:
        out_shape, out_specs = y_shape, seq_spec
    else:
        out_shape = (y_shape, jax.ShapeDtypeStruct((bsz, t, W_RWKV), F32))
        out_specs = (seq_spec, seq_spec)
    return pl.pallas_call(
        functools.partial(_rwkv_kernel, rows=rows, chunk=chunk, has_vres=has_vres),
        grid=(bsz, t // rows),
        in_specs=specs,
        out_specs=out_specs,
        out_shape=out_shape,
        scratch_shapes=[pltpu.VMEM((rows + 8, RW_BLOCK), F32),
                        pltpu.VMEM((W_RWKV // LANES, LANES, LANES), F32),
                        pltpu.VMEM((rows, W_RWKV), F32)],
        compiler_params=pltpu.CompilerParams(
            dimension_semantics=("parallel", "arbitrary"), vmem_limit_bytes=VMEM_LIMIT),
    )(*args)


def _gdn_kernel(x_ref, ab_ref, convw_ref, hp_ref, normw_ref, tri_ref, eye_ref, ones_ref,
                y_ref, buf_ref, s_ref, obuf_ref, *, rows, chunk):
    c = chunk
    ns = rows // c
    nh = N_HEADS_GDN
    wq = 3 * W_GDN
    p_ones, p_inv = PREC["ones"], PREC["inv"]
    head = lambda h: slice(h * LANES, (h + 1) * LANES)
    sub = lambda j: slice(j * c, (j + 1) * c)

    @pl.when(pl.program_id(1) == 0)
    def _():
        buf_ref[0:8, :] = jnp.zeros((8, wq), F32)
        s_ref[...] = jnp.zeros_like(s_ref)

    x = x_ref[0]
    xc = x[:, :wq]
    buf_ref[8:8 + rows, :] = xc
    conv = xc * convw_ref[3:4, :]
    for i in range(CONV_WIDTH - 1):
        conv = conv + buf_ref[5 + i:5 + i + rows, :] * convw_ref[i:i + 1, :]
    buf_ref[0:8, :] = xc[rows - 8:rows, :]
    qkv = conv * _sigmoid(conv)

    ab = ab_ref[0]
    g_all = -jnp.exp(hp_ref[0:1, :]) * _softplus(ab + hp_ref[1:2, :])
    beta_all = _sigmoid(ab)
    gc_all = _mm(tri_ref[...], g_all, p_ones)
    gc_t = _mm(eye_ref[...], gc_all, p_ones, "nt")
    ones = ones_ref[...]

    def l2n(z):
        return z * lax.rsqrt(_mm(z * z, ones, (2, 1)) + NORM_EPS)

    chains = [(j, h) for j in range(ns) for h in range(nh)]
    k_l, kb_l, q_l, vk_l, qe_l, ke_l, gcol_l, grow_l, gend_l = [], [], [], [], [], [], [], [], []
    for h in range(nh):
        q = l2n(qkv[:, head(h)]) * (HEAD_DIM_GDN ** -0.5)
        k = l2n(qkv[:, W_GDN + h * LANES:W_GDN + (h + 1) * LANES])
        v = qkv[:, 2 * W_GDN + h * LANES:2 * W_GDN + (h + 1) * LANES]
        gcol = gc_all[:, h:h + 1]
        beta = beta_all[:, nh + h:nh + h + 1]
        g_end = [gcol[j * c + c - 1:j * c + c, :] for j in range(ns)]
        e_gc = jnp.exp(gcol)
        e_end = jnp.exp(jnp.concatenate([g_end[j] - gcol[sub(j), :] for j in range(ns)], axis=0))
        kb = k * beta
        k_l.append(k.astype(BF16))
        kb_l.append(kb.astype(BF16))
        q_l.append(q.astype(BF16))
        vk_l.append(jnp.concatenate([v * beta, kb * e_gc], axis=1).astype(BF16))
        qe_l.append((q * e_gc).astype(BF16))
        ke_l.append((k * e_end).astype(BF16))
        gcol_l.append(gcol)
        grow_l.append(gc_t[h:h + 1, :])
        gend_l.append(g_end)
    gather = lambda lst: jnp.stack([lst[h][sub(j), :] for j, h in chains])
    k_b, kb_b, q_b, vk_b, qe_b, ke_b = (gather(l) for l in (k_l, kb_l, q_l, vk_l, qe_l, ke_l))
    gcol_b = gather(gcol_l)
    grow_b = jnp.stack([grow_l[h][:, sub(j)] for j, h in chains])

    row = lax.broadcasted_iota(jnp.int32, (1, c, c), 1)
    col = lax.broadcasted_iota(jnp.int32, (1, c, c), 2)
    decay = jnp.exp(jnp.where(row >= col, gcol_b - grow_b, -jnp.inf))
    both = _mm(jnp.concatenate([kb_b, q_b], axis=1), k_b, form="nt")
    kkt, qkt = both[:, :c], both[:, c:]
    t_inv = _unit_lower_inverse(-jnp.where(row > col, kkt * decay, 0.0), c // SOLVE_BLOCK, p_inv).astype(BF16)
    uw = _mm(t_inv, vk_b)
    u = uw[:, :, :LANES]
    wq_b = jnp.concatenate([uw[:, :, LANES:].astype(BF16), qe_b], axis=1)
    attn = (qkt * decay).astype(BF16)

    s = s_ref[...]
    for j in range(ns):
        gs = slice(j * nh, (j + 1) * nh)
        ws_qs = _mm(wq_b[gs], s.astype(BF16))
        v_new = (u[gs] - ws_qs[:, :c]).astype(BF16)
        o = ws_qs[:, c:] + _mm(attn[gs], v_new)
        d_end = jnp.exp(jnp.stack([gend_l[h][j] for h in range(nh)]))
        s = s * d_end + _mm(ke_b[gs], v_new, form="tn")
        for h in range(nh):
            obuf_ref[sub(j), head(h)] = o[h]
    s_ref[...] = s

    o = obuf_ref[...]
    ms = jnp.concatenate([_mm(o[:, head(h)] * o[:, head(h)], ones, (2, 1)) for h in range(nh)], axis=1)
    on = o * lax.rsqrt(ms * (1.0 / HEAD_DIM_GDN) + NORM_EPS) * normw_ref[...]
    z = x[:, wq:]
    y_ref[0] = (on * (z * _sigmoid(z))).astype(y_ref.dtype)


def _gdn_mix(p3d, conv_w, a_log, dt_bias, norm_w, rows=256, chunk=128):
    bsz, t, _ = p3d.shape
    rows = min(rows, t)
    pad = jnp.zeros((LANES - N_HEADS_GDN,), F32)
    hp = jnp.stack([jnp.concatenate([a_log.astype(F32), pad]), jnp.concatenate([dt_bias.astype(F32), pad])])
    eye = jnp.eye(LANES, dtype=BF16)
    ones = jnp.ones((LANES, LANES), BF16)
    normw = jnp.tile(norm_w.astype(F32), N_HEADS_GDN).reshape(1, W_GDN)
    const = lambda shape: pl.BlockSpec(shape, lambda b, c: (0,) * len(shape))
    return pl.pallas_call(
        functools.partial(_gdn_kernel, rows=rows, chunk=chunk),
        grid=(bsz, t // rows),
        in_specs=[
            pl.BlockSpec((1, rows, 4 * W_GDN), lambda b, c: (b, c, COL_GDN // (4 * W_GDN))),
            pl.BlockSpec((1, rows, LANES), lambda b, c: (b, c, COL_GDN_AB // LANES)),
            const((CONV_WIDTH, 3 * W_GDN)), const((2, LANES)), const((1, W_GDN)),
            const((rows, rows)), const((LANES, LANES)), const((LANES, LANES)),
        ],
        out_specs=pl.BlockSpec((1, rows, W_GDN), lambda b, c: (b, c, 0)),
        out_shape=jax.ShapeDtypeStruct((bsz, t, W_GDN), BF16),
        scratch_shapes=[pltpu.VMEM((rows + 8, 3 * W_GDN), F32),
                        pltpu.VMEM((N_HEADS_GDN, HEAD_DIM_GDN, HEAD_DIM_GDN), F32),
                        pltpu.VMEM((rows, W_GDN), F32)],
        compiler_params=pltpu.CompilerParams(
            dimension_semantics=("parallel", "arbitrary"), vmem_limit_bytes=VMEM_LIMIT),
    )(p3d, p3d, conv_w.astype(F32), hp, normw, _block_tri(rows, chunk), eye, ones)


def _diff_attn_kernel(q_ref, k_ref, v_ref, qw_ref, kw_ref, gmat_ref, eye_ref, lam_ref, subw_ref, o_ref,
                      qt_ref, kn_ref, vt_ref, *, tq, tk, lambda_init):
    t = k_ref.shape[1]
    gmat = gmat_ref[...]
    eye = eye_ref[...]

    def normed(ref, wref, rows):
        x = ref[0, rows, :]
        ms = _mm(x * x, gmat, PREC["gsum"]) * (1.0 / HEAD_DIM_DIFF)
        return (x * lax.rsqrt(ms + NORM_EPS) * wref[...]).astype(BF16)

    for r0 in range(0, t, tk):
        rows = slice(r0, r0 + tk)
        kn_ref[rows, :] = normed(k_ref, kw_ref, rows)
        qt_ref[:, rows] = _mm(eye, normed(q_ref, qw_ref, rows), form="nt").astype(BF16)
        vt_ref[:, rows] = _mm(eye, v_ref[0, rows, :].astype(BF16), form="nt").astype(BF16)

    feat = lax.broadcasted_iota(jnp.int32, (LANES, 1), 0)
    lv = lam_ref[...]
    lam = (jnp.exp(jnp.sum(lv[0:1] * lv[1:2], axis=-1, keepdims=True))
           - jnp.exp(jnp.sum(lv[2:3] * lv[3:4], axis=-1, keepdims=True)) + lambda_init)

    tiles = []
    for i in range(t // tq):
        q_end = (i + 1) * tq
        for k0 in range(0, q_end, tk):
            klen = min(tk, q_end - k0)
            tiles.append((i, k0, klen, k0 + klen > i * tq))

    q_maps = {}

    hq = tq // 2

    def scores(i, k0, klen, diag):
        if i not in q_maps:
            q_t = qt_ref[:, i * tq:(i + 1) * tq]
            zero = jnp.zeros_like(q_t)
            q_maps.clear()
            q_maps[i] = (jnp.where(feat < HEAD_DIM_DIFF, q_t, zero), jnp.where(feat >= HEAD_DIM_DIFF, q_t, zero))
        if not diag:
            k = kn_ref[k0:k0 + klen, :]
            return [[_dot(k, qm)] for qm in q_maps[i]]
        return [[_dot(kn_ref[k0:k0 + klen - hq * (1 - h), :], qm[:, h * hq:(h + 1) * hq]) for h in range(2)]
                for qm in q_maps[i]]

    tri_r = lax.broadcasted_iota(jnp.int32, (hq, hq), 0)
    tri_c = lax.broadcasted_iota(jnp.int32, (hq, hq), 1)
    visible = tri_r <= tri_c

    def update(s, v_t, prev):
        if prev is None:
            m_new = jnp.max(s, axis=0, keepdims=True)
            p = jnp.exp2(s - m_new)
            return m_new, jnp.sum(p, axis=0, keepdims=True), _dot(v_t, p.astype(BF16))
        m_prev, l_prev, acc_prev = prev
        m_new = jnp.maximum(m_prev, jnp.max(s, axis=0, keepdims=True))
        alpha = jnp.exp2(m_prev - m_new)
        p = jnp.exp2(s - m_new)
        return (m_new, alpha * l_prev + jnp.sum(p, axis=0, keepdims=True),
                alpha * acc_prev + _dot(v_t, p.astype(BF16)))

    state = None
    pending = scores(*tiles[0])
    for n, (i, k0, klen, diag) in enumerate(tiles):
        cur = pending
        if n + 1 < len(tiles):
            pending = scores(*tiles[n + 1])
        new_state = []
        for mp, parts in enumerate(cur):
            prev = None if k0 == 0 else state[mp]
            if not diag:
                new_state.append(update(parts[0], vt_ref[:, k0:k0 + klen], prev))
                continue
            halves = []
            for h, s in enumerate(parts):
                kl = s.shape[0]
                s_diag = jnp.where(visible, s[kl - hq:], -jnp.inf)
                s = s_diag if kl == hq else jnp.concatenate([s[:kl - hq], s_diag], axis=0)
                cols = slice(h * hq, (h + 1) * hq)
                prev_h = None if prev is None else tuple(z[:, cols] for z in prev)
                halves.append(update(s, vt_ref[:, k0:k0 + kl], prev_h))
            new_state.append(tuple(jnp.concatenate([halves[0][z], halves[1][z]], axis=1) for z in range(3)))
        state = new_state
        if k0 + klen == (i + 1) * tq:
            (_, l0, acc0), (_, l1, acc1) = state
            o_t = acc0 * (1.0 / l0) - lam * (acc1 * (1.0 / l1))
            ms = jnp.sum(o_t * o_t, axis=0, keepdims=True) * (1.0 / LANES)
            on_t = (o_t * lax.rsqrt(ms + NORM_EPS) * subw_ref[...] * (1.0 - lambda_init)).astype(BF16)
            o_ref[0, i * tq:(i + 1) * tq, :] = _mm(on_t, eye, form="tn").astype(o_ref.dtype)


def _diff_attn(p3d, q_norm_w, k_norm_w, lam_vecs, subln_w, lambda_init, tq=512, tk=1024):
    bsz, t, _ = p3d.shape
    tk = min(tk, t)
    tq = min(tq, tk)
    lane = jnp.arange(LANES)
    gmat = ((lane[:, None] // HEAD_DIM_DIFF) == (lane[None, :] // HEAD_DIM_DIFF)).astype(BF16)
    qw = (jnp.tile(q_norm_w.astype(F32), 2) * (HEAD_DIM_DIFF ** -0.5 * math.log2(math.e))).reshape(1, LANES)
    kw = jnp.tile(k_norm_w.astype(F32), 2).reshape(1, LANES)
    base = COL_DIFF // LANES
    sec = lambda j: pl.BlockSpec((1, t, LANES), lambda b, h: (b, 0, base + j * N_HEADS_DIFF + h))
    const = lambda shape: pl.BlockSpec(shape, lambda b, h: (0,) * len(shape))
    return pl.pallas_call(
        functools.partial(_diff_attn_kernel, tq=tq, tk=tk, lambda_init=lambda_init),
        grid=(bsz, N_HEADS_DIFF),
        in_specs=[sec(0), sec(1), sec(2), const((1, LANES)), const((1, LANES)),
                  const((LANES, LANES)), const((LANES, LANES)), const((4, HEAD_DIM_DIFF)), const((LANES, 1))],
        out_specs=pl.BlockSpec((1, t, LANES), lambda b, h: (b, 0, h)),
        out_shape=jax.ShapeDtypeStruct((bsz, t, W_DIFF), BF16),
        scratch_shapes=[pltpu.VMEM((LANES, t), BF16), pltpu.VMEM((t, LANES), BF16), pltpu.VMEM((LANES, t), BF16)],
        compiler_params=pltpu.CompilerParams(
            dimension_semantics=("parallel", "parallel"), vmem_limit_bytes=VMEM_LIMIT),
    )(p3d, p3d, p3d, qw, kw, gmat, jnp.eye(LANES, dtype=BF16), lam_vecs, subln_w.reshape(LANES, 1).astype(F32))


def _mix_ffn_kernel(x_ref, yr_ref, yg_ref, yd_ref, w1_ref, w2_ref, w3_ref, nw_ref, wg_ref, wu_ref, wo_ref,
                    o_ref, h_ref):
    @pl.when(pl.program_id(1) == 0)
    def _():
        x1 = (x_ref[...] + _dot(yr_ref[...], w1_ref[...]) + _dot(yg_ref[...], w2_ref[...])
              + _dot(yd_ref[...], w3_ref[...]))
        ms = jnp.mean(x1 * x1, axis=-1, keepdims=True)
        h_ref[...] = (x1 * lax.rsqrt(ms + NORM_EPS) * nw_ref[...]).astype(BF16)
        o_ref[...] = x1

    h = h_ref[...]
    gate = _dot(h, wg_ref[...])
    up = _dot(h, wu_ref[...])
    act = (gate * _sigmoid(gate) * up).astype(BF16)
    o_ref[...] += _dot(act, wo_ref[...])


def _mix_ffn(x2d, y_rw, y_gdn, y_diff, w_mix_bf16, nw, w_in_bf16, w_out_bf16, layer, tm=512, tf=512):
    n, d = x2d.shape
    hidden = w_out_bf16.shape[1]
    nf = hidden // tf
    rowblk = lambda w: pl.BlockSpec((tm, w), lambda i, f: (i, 0))
    resident = lambda rows, blk: pl.BlockSpec((None, rows, d), lambda i, f: (layer, blk, 0),
                                              pipeline_mode=pl.Buffered(1))
    return pl.pallas_call(
        _mix_ffn_kernel,
        grid=(n // tm, nf),
        in_specs=[
            rowblk(d), rowblk(W_RWKV), rowblk(W_GDN), rowblk(W_DIFF),
            resident(W_RWKV, 0), resident(W_GDN, 1), resident(W_DIFF, 1),
            pl.BlockSpec((1, d), lambda i, f: (0, 0)),
            pl.BlockSpec((None, d, tf), lambda i, f: (layer, 0, f)),
            pl.BlockSpec((None, d, tf), lambda i, f: (layer, 0, nf + f)),
            pl.BlockSpec((None, tf, d), lambda i, f: (layer, f, 0)),
        ],
        out_specs=pl.BlockSpec((tm, d), lambda i, f: (i, 0)),
        out_shape=jax.ShapeDtypeStruct((n, d), F32),
        scratch_shapes=[pltpu.VMEM((tm, d), BF16)],
        compiler_params=pltpu.CompilerParams(
            dimension_semantics=("parallel", "arbitrary"), vmem_limit_bytes=VMEM_LIMIT),
    )(x2d, y_rw, y_gdn, y_diff, w_mix_bf16, w_mix_bf16, w_mix_bf16, nw.reshape(1, d),
      w_in_bf16, w_in_bf16, w_out_bf16)


def _pack_in_proj(w_in_all, layer, w_vres_l):
    d = w_in_all.shape[1]
    vres = jnp.zeros((d, LANES), F32)
    if w_vres_l is not None:
        vres = vres.at[:, :LORA_VRES].set(w_vres_l)
    rb = 64
    return pl.pallas_call(
        _pack_w_kernel,
        grid=(d // rb,),
        in_specs=[pl.BlockSpec((None, rb, N_IN), lambda i: (layer, i, 0)),
                  pl.BlockSpec((rb, LANES), lambda i: (i, 0))],
        out_specs=(pl.BlockSpec((rb, COL_GDN), lambda i: (i, 0)),
                   pl.BlockSpec((rb, 4 * W_GDN), lambda i: (i, 0)),
                   pl.BlockSpec((rb, DIFF_COLS), lambda i: (i, 0))),
        out_shape=(jax.ShapeDtypeStruct((d, COL_GDN), BF16), jax.ShapeDtypeStruct((d, 4 * W_GDN), BF16),
                   jax.ShapeDtypeStruct((d, DIFF_COLS), BF16)),
        compiler_params=pltpu.CompilerParams(dimension_semantics=("parallel",), vmem_limit_bytes=VMEM_LIMIT),
    )(w_in_all, vres)


def _pack_w_kernel(w_ref, vres_ref, wa_ref, wg_ref, wd_ref):
    gdn0 = RWKV_COLS
    ab0 = gdn0 + 4 * W_GDN
    lane = lax.broadcasted_iota(jnp.int32, (1, LANES), 1)
    wa_ref[:, :RWKV_COLS] = w_ref[:, :RWKV_COLS].astype(BF16)
    wa_ref[:, COL_VRES:COL_GDN_AB] = vres_ref[...].astype(BF16)
    wa_ref[:, COL_GDN_AB:] = jnp.where(lane < 2 * N_HEADS_GDN, w_ref[:, ab0:ab0 + LANES], 0.0).astype(BF16)
    wg_ref[...] = w_ref[:, gdn0:ab0].astype(BF16)
    wd_ref[...] = w_ref[:, gdn0 + GDN_COLS:].astype(BF16)


def kernel(x, attn_norm_w, w_in, w_vres_a, mu_rwkv, mu_vres, rwkv_w0, rwkv_w_lora_b, rwkv_a0, rwkv_a_lora_b, rwkv_g_lora_b, rwkv_v0, rwkv_v_lora_b, rwkv_k_k, rwkv_k_a, rwkv_r_k, rwkv_ln_w, rwkv_ln_b, gdn_conv_w, gdn_A_log, gdn_dt_bias, gdn_norm_w, diff_q_norm_w, diff_k_norm_w, diff_lambda_q1, diff_lambda_k1, diff_lambda_q2, diff_lambda_k2, diff_subln_w, w_out, ffn_norm_w, w_ffn_in, w_ffn_out):
    bsz, t, d = x.shape
    depth = w_in.shape[0]
    x2d = x.reshape(bsz * t, d)
    v_first = None
    w_out_b, w_ffn_in_b, w_ffn_out_b = w_out.astype(BF16), w_ffn_in.astype(BF16), w_ffn_out.astype(BF16)
    for l in range(depth):
        w_proj = _pack_in_proj(w_in, l, None if l == 0 else w_vres_a[l - 1])
        mu_tail = jnp.zeros((RW_BLOCK - RWKV_COLS,), F32)
        if l > 0:
            mu_tail = mu_tail.at[:LORA_VRES].set(mu_vres[l - 1])
        mu_pad = jnp.concatenate([mu_rwkv[l], mu_tail]).reshape(1, RW_BLOCK)
        p3d = _norm_matmul(x2d, attn_norm_w[l], w_proj).reshape(bsz, t, N_PAD)
        rw_args = (rwkv_w0[l], rwkv_w_lora_b[l], rwkv_a0[l], rwkv_a_lora_b[l], rwkv_g_lora_b[l])
        rw_tail = (rwkv_k_k[l], rwkv_k_a[l], rwkv_r_k[l], rwkv_ln_w[l], rwkv_ln_b[l])
        if l == 0:
            y_rw, v_first = _rwkv_mix(p3d, None, mu_pad, *rw_args, None, None, *rw_tail)
        else:
            y_rw = _rwkv_mix(p3d, v_first, mu_pad, *rw_args, rwkv_v0[l - 1], rwkv_v_lora_b[l - 1], *rw_tail)
        y_gdn = _gdn_mix(p3d, gdn_conv_w[l], gdn_A_log[l], gdn_dt_bias[l], gdn_norm_w[l])
        lambda_init = 0.8 - 0.6 * math.exp(-0.3 * l)
        lam_vecs = jnp.stack([diff_lambda_q1[l], diff_lambda_k1[l], diff_lambda_q2[l], diff_lambda_k2[l]]).astype(F32)
        y_diff = _diff_attn(p3d, diff_q_norm_w[l], diff_k_norm_w[l], lam_vecs, diff_subln_w[l], lambda_init)
        n = bsz * t
        x2d = _mix_ffn(x2d, y_rw.reshape(n, W_RWKV), y_gdn.reshape(n, W_GDN), y_diff.reshape(n, W_DIFF),
                       w_out_b, ffn_norm_w[l], w_ffn_in_b, w_ffn_out_b, l)
    return x2d.reshape(bsz, t, d)
```

```python
import functools
import math

import jax
import jax.numpy as jnp
from jax import lax
from jax.experimental import pallas as pl
from jax.experimental.pallas import tpu as pltpu

F32 = jnp.float32
BF16 = jnp.bfloat16

D_MODEL = 2048
W_RWKV = D_MODEL // 4
HEAD_DIM_RWKV = 64
LORA_DECAY = 64
LORA_ICLR = 64
LORA_VRES = 32
LORA_GATE = 128
RWKV_GN_EPS = 64e-5
W_GDN = D_MODEL // 4
HEAD_DIM_GDN = 128
N_HEADS_GDN = W_GDN // HEAD_DIM_GDN
CONV_WIDTH = 4
W_DIFF = D_MODEL // 2
HEAD_DIM_DIFF = 64
N_HEADS_DIFF = W_DIFF // (2 * HEAD_DIM_DIFF)
RWKV_COLS = 3 * W_RWKV + LORA_DECAY + LORA_ICLR + LORA_GATE
GDN_COLS = 4 * W_GDN + 2 * N_HEADS_GDN
DIFF_COLS = 3 * W_DIFF
N_IN = RWKV_COLS + GDN_COLS + DIFF_COLS
FFN_HIDDEN = -(-8 * D_MODEL // (3 * 256)) * 256
NORM_EPS = 1e-6

LANES = 128
SOLVE_BLOCK = 16
NORM_CHUNK = 256

COL_VRES = RWKV_COLS
COL_GDN_AB = COL_VRES + LANES
COL_GDN = COL_GDN_AB + LANES
COL_DIFF = COL_GDN + 4 * W_GDN
N_PAD = COL_DIFF + DIFF_COLS
RW_BLOCK = COL_GDN

VMEM_LIMIT = 56 * 1024 * 1024

PREC = {
    "lora": (1, 1),
    "ones": (1, 3),
    "gsum": (1, 1),
    "inv": (1, 1),
}


def _pieces(x, n):
    if isinstance(x, (list, tuple)):
        return list(x)[:n]
    if x.dtype == BF16:
        return [x]
    out, rem = [], x
    for i in range(n):
        piece = rem.astype(BF16)
        out.append(piece)
        if i + 1 < n:
            rem = rem - piece.astype(F32)
    return out


_DIMS = {"nn": (((1,), (0,)), ((), ())), "nt": (((1,), (1,)), ((), ())), "tn": (((0,), (0,)), ((), ()))}
_DIMS_B = {"nn": (((2,), (1,)), ((0,), (0,))), "nt": (((2,), (2,)), ((0,), (0,))), "tn": (((1,), (1,)), ((0,), (0,)))}


def _mm(a, b, prec=(1, 1), form="nn"):
    pa, pb = _pieces(a, prec[0]), _pieces(b, prec[1])
    dims = (_DIMS_B if pa[0].ndim == 3 else _DIMS)[form]
    depth = max(len(pa), len(pb))
    terms = sorted(((i, j) for i in range(len(pa)) for j in range(len(pb)) if i + j < depth),
                   key=lambda ij: -(ij[0] + ij[1]))
    acc = None
    for i, j in terms:
        d = lax.dot_general(pa[i], pb[j], dims, preferred_element_type=F32)
        acc = d if acc is None else acc + d
    return acc


def _dot(a, b):
    return jnp.dot(a, b, preferred_element_type=F32)


def _sigmoid(x):
    return 1.0 / (1.0 + jnp.exp(-x))


def _softplus(x):
    return jnp.maximum(x, 0.0) + jnp.log(1.0 + jnp.exp(-jnp.abs(x)))


def _split_bf16(w, n):
    out, rem = [], w.astype(F32)
    for _ in range(n):
        piece = rem.astype(BF16)
        out.append(piece)
        rem = rem - piece.astype(F32)
    return jnp.stack(out)


def _block_tri(rows, chunk):
    idx = jnp.arange(rows)
    same = (idx[:, None] // chunk) == (idx[None, :] // chunk)
    return (same & (idx[:, None] >= idx[None, :])).astype(BF16)


def _unit_lower_inverse(lo_tri, n_sub, prec):
    n = lo_tri.shape[-1]
    lead = (1,) * (lo_tri.ndim - 2)
    row = lax.broadcasted_iota(jnp.int32, lead + (n, n), lo_tri.ndim - 2)
    col = lax.broadcasted_iota(jnp.int32, lead + (n, n), lo_tri.ndim - 1)
    eye = (row == col).astype(F32)
    same = (row // SOLVE_BLOCK) == (col // SOLVE_BLOCK)
    l_diag = jnp.where(same, lo_tri, 0.0)
    l_off = jnp.where(same, 0.0, lo_tri)

    def neumann(a, s, order):
        width = 1
        while width < order:
            if 2 * width < order:
                both = _mm(a, jnp.concatenate([s, a], axis=-1), prec)
                s = s + both[..., :n]
                a = both[..., n:]
            else:
                s = s + _mm(a, s, prec)
            width *= 2
        return s

    t_diag = neumann(_mm(l_diag, l_diag, prec), eye + l_diag, SOLVE_BLOCK // 2)
    return neumann(_mm(t_diag, l_off, prec), t_diag, n_sub)


def _norm_matmul_kernel(x_hbm, nw_ref, wa_ref, wg_ref, wd_ref, o_ref, h_ref, xbuf_ref, sem_ref, *, na, ng, tm):
    i = pl.program_id(0)
    j = pl.program_id(1)
    slot = i % 2

    def x_copy(blk, s):
        return pltpu.make_async_copy(x_hbm.at[pl.ds(blk * tm, tm), :], xbuf_ref.at[s], sem_ref.at[s])

    @pl.when((i == 0) & (j == 0))
    def _():
        x_copy(0, 0).start()

    @pl.when((j == 1) & (i + 1 < pl.num_programs(0)))
    def _():
        x_copy(i + 1, 1 - slot).start()

    @pl.when(j == 0)
    def _():
        x_copy(i, slot).wait()
        for r0 in range(0, tm, NORM_CHUNK):
            rows = slice(r0, r0 + NORM_CHUNK)
            x = xbuf_ref[slot, rows, :]
            ms = jnp.mean(x * x, axis=-1, keepdims=True)
            h = (x * lax.rsqrt(ms + NORM_EPS) * nw_ref[...]).astype(BF16)
            h_ref[rows, :] = h
            o_ref[rows, :] = _dot(h, wa_ref[...])

    @pl.when((j > 0) & (j < na))
    def _():
        o_ref[...] = _dot(h_ref[...], wa_ref[...])

    @pl.when((j >= na) & (j < na + ng))
    def _():
        o_ref[...] = _dot(h_ref[...], wg_ref[...])

    @pl.when(j >= na + ng)
    def _():
        o_ref[...] = _dot(h_ref[...], wd_ref[...])


def _norm_matmul(x2d, nw, w_sections, tm=1024, tn=1024):
    n, d = x2d.shape
    tm = min(tm, n)
    wa, wg, wd = w_sections
    na, ng, nd = wa.shape[1] // tn, wg.shape[1] // tn, wd.shape[1] // tn
    return pl.pallas_call(
        functools.partial(_norm_matmul_kernel, na=na, ng=ng, tm=tm),
        grid=(n // tm, na + ng + nd),
        in_specs=[
            pl.BlockSpec(memory_space=pl.ANY),
            pl.BlockSpec((1, d), lambda i, j: (0, 0)),
            pl.BlockSpec((d, tn), lambda i, j: (0, jnp.minimum(j, na - 1))),
            pl.BlockSpec((d, tn), lambda i, j: (0, jnp.clip(j - na, 0, ng - 1))),
            pl.BlockSpec((d, tn), lambda i, j: (0, jnp.maximum(j - na - ng, 0))),
        ],
        out_specs=pl.BlockSpec((tm, tn), lambda i, j: (i, j)),
        out_shape=jax.ShapeDtypeStruct((n, (na + ng + nd) * tn), F32),
        scratch_shapes=[pltpu.VMEM((tm, d), BF16), pltpu.VMEM((2, tm, d), F32), pltpu.SemaphoreType.DMA((2,))],
        compiler_params=pltpu.CompilerParams(
            dimension_semantics=("arbitrary", "arbitrary"), vmem_limit_bytes=VMEM_LIMIT),
    )(x2d, nw.reshape(1, d), wa, wg, wd)


def _rwkv_kernel(*refs, rows, chunk, has_vres):
    if has_vres:
        (p_ref, vf_ref, mu_ref, w0_ref, wbw_ref, a0_ref, wba_ref, wbg_ref, v0_ref, wbv_ref,
         kk_ref, ka_ref, rk_ref, lnw_ref, lnb_ref, gmat_ref, tri_ref,
         y_ref, buf_ref, s_ref, ybuf_ref) = refs
    else:
        (p_ref, mu_ref, w0_ref, wbw_ref, a0_ref, wba_ref, wbg_ref,
         kk_ref, ka_ref, rk_ref, lnw_ref, lnb_ref, gmat_ref, tri_ref,
         y_ref, vout_ref, buf_ref, s_ref, ybuf_ref) = refs
    c = chunk
    ns = rows // c
    n_pairs = W_RWKV // LANES
    p_lora, p_ones, p_gsum, p_inv = PREC["lora"], PREC["ones"], PREC["gsum"], PREC["inv"]
    wpieces = lambda ref: [ref[i] for i in range(ref.shape[0])]
    pair = lambda pi: slice(pi * LANES, (pi + 1) * LANES)
    sub = lambda j: slice(j * c, (j + 1) * c)

    @pl.when(pl.program_id(1) == 0)
    def _():
        buf_ref[0:8, :] = jnp.zeros((8, RW_BLOCK), F32)
        s_ref[...] = jnp.zeros_like(s_ref)

    x = p_ref[0]
    buf_ref[8:8 + rows, :] = x
    prev = buf_ref[7:7 + rows, :]
    buf_ref[0:8, :] = x[rows - 8:rows, :]
    pm = x + (prev - x) * mu_ref[...]

    r = pm[:, 0:W_RWKV]
    k = pm[:, W_RWKV:2 * W_RWKV]
    v = pm[:, 2 * W_RWKV:3 * W_RWKV]
    lw = pm[:, 3 * W_RWKV:3 * W_RWKV + LANES]
    xg = pm[:, 3 * W_RWKV + LANES:3 * W_RWKV + 2 * LANES]
    w_log = -_softplus(-(w0_ref[...] + _mm(jnp.tanh(lw), wpieces(wbw_ref), p_lora))) - 0.5
    logd = -jnp.exp(w_log)
    a = _sigmoid(a0_ref[...] + _mm(lw, wpieces(wba_ref), p_lora))
    g = _mm(_sigmoid(xg), wpieces(wbg_ref), p_lora)
    if has_vres:
        xv = pm[:, COL_VRES:COL_VRES + LANES]
        v = v + (vf_ref[0] - v) * _sigmoid(v0_ref[...] + _mm(xv, wpieces(wbv_ref), p_lora))
    else:
        vout_ref[0] = v

    gmat = gmat_ref[...]

    def gsum(z):
        return jnp.concatenate([_mm(z[:, pair(pi)], gmat, p_gsum) for pi in range(n_pairs)], axis=1)

    kk = k * kk_ref[...]
    kk = kk * lax.rsqrt(gsum(kk * kk) + NORM_EPS)
    k2 = k * (1.0 + (a - 1.0) * ka_ref[...])
    b = kk * a
    gc = _mm(tri_ref[...], logd, p_ones)
    g_end = [gc[j * c + c - 1:j * c + c, :] for j in range(ns)]
    e_end = jnp.exp(jnp.concatenate([g_end[j] - gc[sub(j), :] for j in range(ns)], axis=0))
    e_inv = jnp.exp(-gc)

    lane = lax.broadcasted_iota(jnp.int32, (1, W_RWKV), 1)
    h0 = ((lane // HEAD_DIM_RWKV) % 2 == 0).astype(F32)
    h1 = 1.0 - h0

    def stack_all(z):
        z0 = (z * h0).astype(BF16)
        z1 = (z * h1).astype(BF16)
        return jnp.stack([jnp.concatenate([z0[sub(j), pair(pi)], z1[sub(j), pair(pi)]], axis=0)
                          for j in range(ns) for pi in range(n_pairs)])

    at = stack_all(-kk * jnp.exp(gc - logd))
    rt = stack_all(r * jnp.exp(gc))
    bt = stack_all(b * e_inv)
    kt = stack_all(k2 * e_inv)
    vst = stack_all(v)
    bh = stack_all(b * e_end)
    kh = stack_all(k2 * e_end)

    n2 = 2 * c
    quad = _mm(jnp.concatenate([at, rt], axis=1), jnp.concatenate([bt, kt], axis=1), form="nt")
    row = lax.broadcasted_iota(jnp.int32, (1, n2, n2), 1) % c
    col = lax.broadcasted_iota(jnp.int32, (1, n2, n2), 2) % c
    strict = row > col
    incl = row >= col
    l_ab = jnp.where(strict, quad[:, :n2, :n2], 0.0)
    a_rb = jnp.where(incl, quad[:, n2:, :n2], 0.0).astype(BF16)
    vmask = jnp.concatenate([jnp.broadcast_to(strict, (1, n2, n2)), jnp.broadcast_to(incl, (1, n2, n2))], axis=1)
    own = _mm(jnp.where(vmask, quad[:, :, n2:], 0.0).astype(BF16), vst)
    y0 = own[:, n2:]
    t_inv = _unit_lower_inverse(l_ab, c // SOLVE_BLOCK, p_inv).astype(BF16)
    t_both = _mm(t_inv, jnp.concatenate([at, own[:, :n2].astype(BF16)], axis=-1))
    u0 = t_both[:, :, LANES:]
    wr = jnp.concatenate([t_both[:, :, :LANES].astype(BF16), rt], axis=1)
    kv = _mm(vst, kh, form="tn")

    s = s_ref[...]
    for j in range(ns):
        gs = slice(j * n_pairs, (j + 1) * n_pairs)
        from_s = _mm(wr[gs], s.astype(BF16), form="nt")
        u = u0[gs] + from_s[:, :n2]
        u_b = u.astype(BF16)
        y_st = from_s[:, n2:] + _mm(a_rb[gs], u_b) + y0[gs]
        d_end = jnp.exp(jnp.stack([g_end[j][:, pair(pi)] for pi in range(n_pairs)]))
        s = s * d_end + _mm(u_b, bh[gs], form="tn") + kv[gs]
        y = y_st[:, :c] + y_st[:, c:]
        for pi in range(n_pairs):
            ybuf_ref[sub(j), pair(pi)] = y[pi]
    s_ref[...] = s

    y = ybuf_ref[...]
    inv_n = 1.0 / HEAD_DIM_RWKV
    mean = gsum(y) * inv_n
    dlt = y - mean
    var = gsum(dlt * dlt) * inv_n
    yn = dlt * lax.rsqrt(var + RWKV_GN_EPS) * lnw_ref[...] + lnb_ref[...]
    bonus = gsum(r * k2 * rk_ref[...]) * v
    y_ref[0] = ((yn + bonus) * g).astype(y_ref.dtype)


def _rwkv_mix(p3d, v_first, mu_pad, w0, wbw, a0, wba, wbg, v0, wbv, k_k, k_a, r_k, ln_w, ln_b, rows=256, chunk=64):
    bsz, t, _ = p3d.shape
    rows = min(rows, t)
    has_vres = v_first is not None
    n_lora = PREC["lora"][1]
    row = lambda z: z.reshape(1, -1).astype(F32)
    lane = jnp.arange(LANES)
    gmat = ((lane[:, None] // HEAD_DIM_RWKV) == (lane[None, :] // HEAD_DIM_RWKV)).astype(BF16)
    zeros = jnp.zeros((LORA_DECAY, W_RWKV), F32)
    wbw_pad = _split_bf16(jnp.concatenate([wbw, zeros], axis=0), n_lora)
    wba_pad = _split_bf16(jnp.concatenate([zeros, wba], axis=0), n_lora)
    const = lambda shape: pl.BlockSpec(shape, lambda b, c: (0,) * len(shape))
    lora_spec = const((n_lora, LANES, W_RWKV))
    p_spec = pl.BlockSpec((1, rows, RW_BLOCK), lambda b, c: (b, c, 0))
    seq_spec = pl.BlockSpec((1, rows, W_RWKV), lambda b, c: (b, c, 0))
    args = [p3d]
    specs = [p_spec]
    if has_vres:
        args.append(v_first)
        specs.append(seq_spec)
    args += [mu_pad, row(w0), wbw_pad, row(a0), wba_pad, _split_bf16(wbg, n_lora)]
    specs += [const((1, RW_BLOCK)), const((1, W_RWKV)), lora_spec, const((1, W_RWKV)), lora_spec, lora_spec]
    if has_vres:
        wbv_pad = jnp.concatenate([wbv, jnp.zeros((LANES - LORA_VRES, W_RWKV), F32)], axis=0)
        args += [row(v0), _split_bf16(wbv_pad, n_lora)]
        specs += [const((1, W_RWKV)), lora_spec]
    args += [row(k_k), row(k_a), row(r_k), row(ln_w), row(ln_b), gmat, _block_tri(rows, chunk)]
    specs += [const((1, W_RWKV))] * 5 + [const((LANES, LANES)), const((rows, rows))]
    y_shape = jax.ShapeDtypeStruct((bsz, t, W_RWKV), BF16)
    if has_vres:
        out_shape, out_specs = y_shape, seq_spec
    else:
        out_shape = (y_shape, jax.ShapeDtypeStruct((bsz, t, W_RWKV), F32))
        out_specs = (seq_spec, seq_spec)
    return pl.pallas_call(
        functools.partial(_rwkv_kernel, rows=rows, chunk=chunk, has_vres=has_vres),
        grid=(bsz, t // rows),
        in_specs=specs,
        out_specs=out_specs,
        out_shape=out_shape,
        scratch_shapes=[pltpu.VMEM((rows + 8, RW_BLOCK), F32),
                        pltpu.VMEM((W_RWKV // LANES, LANES, LANES), F32),
                        pltpu.VMEM((rows, W_RWKV), F32)],
        compiler_params=pltpu.CompilerParams(
            dimension_semantics=("parallel", "arbitrary"), vmem_limit_bytes=VMEM_LIMIT),
    )(*args)


def _gdn_kernel(x_ref, ab_ref, convw_ref, hp_ref, normw_ref, tri_ref, eye_ref, ones_ref,
                y_ref, buf_ref, s_ref, obuf_ref, *, rows, chunk):
    c = chunk
    ns = rows // c
    nh = N_HEADS_GDN
    wq = 3 * W_GDN
    p_ones, p_inv = PREC["ones"], PREC["inv"]
    head = lambda h: slice(h * LANES, (h + 1) * LANES)
    sub = lambda j: slice(j * c, (j + 1) * c)

    @pl.when(pl.program_id(1) == 0)
    def _():
        buf_ref[0:8, :] = jnp.zeros((8, wq), F32)
        s_ref[...] = jnp.zeros_like(s_ref)

    x = x_ref[0]
    xc = x[:, :wq]
    buf_ref[8:8 + rows, :] = xc
    conv = xc * convw_ref[3:4, :]
    for i in range(CONV_WIDTH - 1):
        conv = conv + buf_ref[5 + i:5 + i + rows, :] * convw_ref[i:i + 1, :]
    buf_ref[0:8, :] = xc[rows - 8:rows, :]
    qkv = conv * _sigmoid(conv)

    ab = ab_ref[0]
    g_all = -jnp.exp(hp_ref[0:1, :]) * _softplus(ab + hp_ref[1:2, :])
    beta_all = _sigmoid(ab)
    gc_all = _mm(tri_ref[...], g_all, p_ones)
    gc_t = _mm(eye_ref[...], gc_all, p_ones, "nt")
    ones = ones_ref[...]

    def l2n(z):
        return z * lax.rsqrt(_mm(z * z, ones, (2, 1)) + NORM_EPS)

    chains = [(j, h) for j in range(ns) for h in range(nh)]
    k_l, kb_l, q_l, vk_l, qe_l, ke_l, gcol_l, grow_l, gend_l = [], [], [], [], [], [], [], [], []
    for h in range(nh):
        q = l2n(qkv[:, head(h)]) * (HEAD_DIM_GDN ** -0.5)
        k = l2n(qkv[:, W_GDN + h * LANES:W_GDN + (h + 1) * LANES])
        v = qkv[:, 2 * W_GDN + h * LANES:2 * W_GDN + (h + 1) * LANES]
        gcol = gc_all[:, h:h + 1]
        beta = beta_all[:, nh + h:nh + h + 1]
        g_end = [gcol[j * c + c - 1:j * c + c, :] for j in range(ns)]
        e_gc = jnp.exp(gcol)
        e_end = jnp.exp(jnp.concatenate([g_end[j] - gcol[sub(j), :] for j in range(ns)], axis=0))
        kb = k * beta
        k_l.append(k.astype(BF16))
        kb_l.append(kb.astype(BF16))
        q_l.append(q.astype(BF16))
        vk_l.append(jnp.concatenate([v * beta, kb * e_gc], axis=1).astype(BF16))
        qe_l.append((q * e_gc).astype(BF16))
        ke_l.append((k * e_end).astype(BF16))
        gcol_l.append(gcol)
        grow_l.append(gc_t[h:h + 1, :])
        gend_l.append(g_end)
    gather = lambda lst: jnp.stack([lst[h][sub(j), :] for j, h in chains])
    k_b, kb_b, q_b, vk_b, qe_b, ke_b = (gather(l) for l in (k_l, kb_l, q_l, vk_l, qe_l, ke_l))
    gcol_b = gather(gcol_l)
    grow_b = jnp.stack([grow_l[h][:, sub(j)] for j, h in chains])

    row = lax.broadcasted_iota(jnp.int32, (1, c, c), 1)
    col = lax.broadcasted_iota(jnp.int32, (1, c, c), 2)
    decay = jnp.exp(jnp.where(row >= col, gcol_b - grow_b, -jnp.inf))
    both = _mm(jnp.concatenate([kb_b, q_b], axis=1), k_b, form="nt")
    kkt, qkt = both[:, :c], both[:, c:]
    t_inv = _unit_lower_inverse(-jnp.where(row > col, kkt * decay, 0.0), c // SOLVE_BLOCK, p_inv).astype(BF16)
    uw = _mm(t_inv, vk_b)
    u = uw[:, :, :LANES]
    wq_b = jnp.concatenate([uw[:, :, LANES:].astype(BF16), qe_b], axis=1)
    attn = (qkt * decay).astype(BF16)

    s = s_ref[...]
    for j in range(ns):
        gs = slice(j * nh, (j + 1) * nh)
        ws_qs = _mm(wq_b[gs], s.astype(BF16))
        v_new = (u[gs] - ws_qs[:, :c]).astype(BF16)
        o = ws_qs[:, c:] + _mm(attn[gs], v_new)
        d_end = jnp.exp(jnp.stack([gend_l[h][j] for h in range(nh)]))
        s = s * d_end + _mm(ke_b[gs], v_new, form="tn")
        for h in range(nh):
            obuf_ref[sub(j), head(h)] = o[h]
    s_ref[...] = s

    o = obuf_ref[...]
    ms = jnp.concatenate([_mm(o[:, head(h)] * o[:, head(h)], ones, (2, 1)) for h in range(nh)], axis=1)
    on = o * lax.rsqrt(ms * (1.0 / HEAD_DIM_GDN) + NORM_EPS) * normw_ref[...]
    z = x[:, wq:]
    y_ref[0] = (on * (z * _sigmoid(z))).astype(y_ref.dtype)


def _gdn_mix(p3d, conv_w, a_log, dt_bias, norm_w, rows=256, chunk=128):
    bsz, t, _ = p3d.shape
    rows = min(rows, t)
    pad = jnp.zeros((LANES - N_HEADS_GDN,), F32)
    hp = jnp.stack([jnp.concatenate([a_log.astype(F32), pad]), jnp.concatenate([dt_bias.astype(F32), pad])])
    eye = jnp.eye(LANES, dtype=BF16)
    ones = jnp.ones((LANES, LANES), BF16)
    normw = jnp.tile(norm_w.astype(F32), N_HEADS_GDN).reshape(1, W_GDN)
    const = lambda shape: pl.BlockSpec(shape, lambda b, c: (0,) * len(shape))
    return pl.pallas_call(
        functools.partial(_gdn_kernel, rows=rows, chunk=chunk),
        grid=(bsz, t // rows),
        in_specs=[
            pl.BlockSpec((1, rows, 4 * W_GDN), lambda b, c: (b, c, COL_GDN // (4 * W_GDN))),
            pl.BlockSpec((1, rows, LANES), lambda b, c: (b, c, COL_GDN_AB // LANES)),
            const((CONV_WIDTH, 3 * W_GDN)), const((2, LANES)), const((1, W_GDN)),
            const((rows, rows)), const((LANES, LANES)), const((LANES, LANES)),
        ],
        out_specs=pl.BlockSpec((1, rows, W_GDN), lambda b, c: (b, c, 0)),
        out_shape=jax.ShapeDtypeStruct((bsz, t, W_GDN), BF16),
        scratch_shapes=[pltpu.VMEM((rows + 8, 3 * W_GDN), F32),
                        pltpu.VMEM((N_HEADS_GDN, HEAD_DIM_GDN, HEAD_DIM_GDN), F32),
                        pltpu.VMEM((rows, W_GDN), F32)],
        compiler_params=pltpu.CompilerParams(
            dimension_semantics=("parallel", "arbitrary"), vmem_limit_bytes=VMEM_LIMIT),
    )(p3d, p3d, conv_w.astype(F32), hp, normw, _block_tri(rows, chunk), eye, ones)


def _diff_attn_kernel(q_ref, k_ref, v_ref, qw_ref, kw_ref, gmat_ref, eye_ref, lam_ref, subw_ref, o_ref,
                      qt_ref, kn_ref, vt_ref, *, tq, tk, lambda_init):
    t = k_ref.shape[1]
    gmat = gmat_ref[...]
    eye = eye_ref[...]

    def normed(ref, wref, rows):
        x = ref[0, rows, :]
        ms = _mm(x * x, gmat, PREC["gsum"]) * (1.0 / HEAD_DIM_DIFF)
        return (x * lax.rsqrt(ms + NORM_EPS) * wref[...]).astype(BF16)

    for r0 in range(0, t, tk):
        rows = slice(r0, r0 + tk)
        kn_ref[rows, :] = normed(k_ref, kw_ref, rows)
        qt_ref[:, rows] = _mm(eye, normed(q_ref, qw_ref, rows), form="nt").astype(BF16)
        vt_ref[:, rows] = _mm(eye, v_ref[0, rows, :].astype(BF16), form="nt").astype(BF16)

    feat = lax.broadcasted_iota(jnp.int32, (LANES, 1), 0)
    lv = lam_ref[...]
    lam = (jnp.exp(jnp.sum(lv[0:1] * lv[1:2], axis=-1, keepdims=True))
           - jnp.exp(jnp.sum(lv[2:3] * lv[3:4], axis=-1, keepdims=True)) + lambda_init)

    tiles = []
    for i in range(t // tq):
        q_end = (i + 1) * tq
        for k0 in range(0, q_end, tk):
            klen = min(tk, q_end - k0)
            tiles.append((i, k0, klen, k0 + klen > i * tq))

    q_maps = {}

    hq = tq // 2

    def scores(i, k0, klen, diag):
        if i not in q_maps:
            q_t = qt_ref[:, i * tq:(i + 1) * tq]
            zero = jnp.zeros_like(q_t)
            q_maps.clear()
            q_maps[i] = (jnp.where(feat < HEAD_DIM_DIFF, q_t, zero), jnp.where(feat >= HEAD_DIM_DIFF, q_t, zero))
        if not diag:
            k = kn_ref[k0:k0 + klen, :]
            return [[_dot(k, qm)] for qm in q_maps[i]]
        return [[_dot(kn_ref[k0:k0 + klen - hq * (1 - h), :], qm[:, h * hq:(h + 1) * hq]) for h in range(2)]
                for qm in q_maps[i]]

    tri_r = lax.broadcasted_iota(jnp.int32, (hq, hq), 0)
    tri_c = lax.broadcasted_iota(jnp.int32, (hq, hq), 1)
    visible = tri_r <= tri_c

    def update(s, v_t, prev):
        if prev is None:
            m_new = jnp.max(s, axis=0, keepdims=True)
            p = jnp.exp2(s - m_new)
            return m_new, jnp.sum(p, axis=0, keepdims=True), _dot(v_t, p.astype(BF16))
        m_prev, l_prev, acc_prev = prev
        m_new = jnp.maximum(m_prev, jnp.max(s, axis=0, keepdims=True))
        alpha = jnp.exp2(m_prev - m_new)
        p = jnp.exp2(s - m_new)
        return (m_new, alpha * l_prev + jnp.sum(p, axis=0, keepdims=True),
                alpha * acc_prev + _dot(v_t, p.astype(BF16)))

    state = None
    pending = scores(*tiles[0])
    for n, (i, k0, klen, diag) in enumerate(tiles):
        cur = pending
        if n + 1 < len(tiles):
            pending = scores(*tiles[n + 1])
        new_state = []
        for mp, parts in enumerate(cur):
            prev = None if k0 == 0 else state[mp]
            if not diag:
                new_state.append(update(parts[0], vt_ref[:, k0:k0 + klen], prev))
                continue
            halves = []
            for h, s in enumerate(parts):
                kl = s.shape[0]
                s_diag = jnp.where(visible, s[kl - hq:], -jnp.inf)
                s = s_diag if kl == hq else jnp.concatenate([s[:kl - hq], s_diag], axis=0)
                cols = slice(h * hq, (h + 1) * hq)
                prev_h = None if prev is None else tuple(z[:, cols] for z in prev)
                halves.append(update(s, vt_ref[:, k0:k0 + kl], prev_h))
            new_state.append(tuple(jnp.concatenate([halves[0][z], halves[1][z]], axis=1) for z in range(3)))
        state = new_state
        if k0 + klen == (i + 1) * tq:
            (_, l0, acc0), (_, l1, acc1) = state
            o_t = acc0 * (1.0 / l0) - lam * (acc1 * (1.0 / l1))
            ms = jnp.sum(o_t * o_t, axis=0, keepdims=True) * (1.0 / LANES)
            on_t = (o_t * lax.rsqrt(ms + NORM_EPS) * subw_ref[...] * (1.0 - lambda_init)).astype(BF16)
            o_ref[0, i * tq:(i + 1) * tq, :] = _mm(on_t, eye, form="tn").astype(o_ref.dtype)


def _diff_attn(p3d, q_norm_w, k_norm_w, lam_vecs, subln_w, lambda_init, tq=512, tk=1024):
    bsz, t, _ = p3d.shape
    tk = min(tk, t)
    tq = min(tq, tk)
    lane = jnp.arange(LANES)
    gmat = ((lane[:, None] // HEAD_DIM_DIFF) == (lane[None, :] // HEAD_DIM_DIFF)).astype(BF16)
    qw = (jnp.tile(q_norm_w.astype(F32), 2) * (HEAD_DIM_DIFF ** -0.5 * math.log2(math.e))).reshape(1, LANES)
    kw = jnp.tile(k_norm_w.astype(F32), 2).reshape(1, LANES)
    base = COL_DIFF // LANES
    sec = lambda j: pl.BlockSpec((1, t, LANES), lambda b, h: (b, 0, base + j * N_HEADS_DIFF + h))
    const = lambda shape: pl.BlockSpec(shape, lambda b, h: (0,) * len(shape))
    return pl.pallas_call(
        functools.partial(_diff_attn_kernel, tq=tq, tk=tk, lambda_init=lambda_init),
        grid=(bsz, N_HEADS_DIFF),
        in_specs=[sec(0), sec(1), sec(2), const((1, LANES)), const((1, LANES)),
                  const((LANES, LANES)), const((LANES, LANES)), const((4, HEAD_DIM_DIFF)), const((LANES, 1))],
        out_specs=pl.BlockSpec((1, t, LANES), lambda b, h: (b, 0, h)),
        out_shape=jax.ShapeDtypeStruct((bsz, t, W_DIFF), BF16),
        scratch_shapes=[pltpu.VMEM((LANES, t), BF16), pltpu.VMEM((t, LANES), BF16), pltpu.VMEM((LANES, t), BF16)],
        compiler_params=pltpu.CompilerParams(
            dimension_semantics=("parallel", "parallel"), vmem_limit_bytes=VMEM_LIMIT),
    )(p3d, p3d, p3d, qw, kw, gmat, jnp.eye(LANES, dtype=BF16), lam_vecs, subln_w.reshape(LANES, 1).astype(F32))


def _mix_ffn_kernel(x_ref, yr_ref, yg_ref, yd_ref, w1_ref, w2_ref, w3_ref, nw_ref, wg_ref, wu_ref, wo_ref,
                    o_ref, h_ref):
    @pl.when(pl.program_id(1) == 0)
    def _():
        x1 = (x_ref[...] + _dot(yr_ref[...], w1_ref[...]) + _dot(yg_ref[...], w2_ref[...])
              + _dot(yd_ref[...], w3_ref[...]))
        ms = jnp.mean(x1 * x1, axis=-1, keepdims=True)
        h_ref[...] = (x1 * lax.rsqrt(ms + NORM_EPS) * nw_ref[...]).astype(BF16)
        o_ref[...] = x1

    h = h_ref[...]
    gate = _dot(h, wg_ref[...])
    up = _dot(h, wu_ref[...])
    act = (gate * _sigmoid(gate) * up).astype(BF16)
    o_ref[...] += _dot(act, wo_ref[...])


def _mix_ffn(x2d, y_rw, y_gdn, y_diff, w_mix_bf16, nw, w_in_bf16, w_out_bf16, layer, tm=512, tf=512):
    n, d = x2d.shape
    hidden = w_out_bf16.shape[1]
    nf = hidden // tf
    rowblk = lambda w: pl.BlockSpec((tm, w), lambda i, f: (i, 0))
    resident = lambda rows, blk: pl.BlockSpec((None, rows, d), lambda i, f: (layer, blk, 0),
                                              pipeline_mode=pl.Buffered(1))
    return pl.pallas_call(
        _mix_ffn_kernel,
        grid=(n // tm, nf),
        in_specs=[
            rowblk(d), rowblk(W_RWKV), rowblk(W_GDN), rowblk(W_DIFF),
            resident(W_RWKV, 0), resident(W_GDN, 1), resident(W_DIFF, 1),
            pl.BlockSpec((1, d), lambda i, f: (0, 0)),
            pl.BlockSpec((None, d, tf), lambda i, f: (layer, 0, f)),
            pl.BlockSpec((None, d, tf), lambda i, f: (layer, 0, nf + f)),
            pl.BlockSpec((None, tf, d), lambda i, f: (layer, f, 0)),
        ],
        out_specs=pl.BlockSpec((tm, d), lambda i, f: (i, 0)),
        out_shape=jax.ShapeDtypeStruct((n, d), F32),
        scratch_shapes=[pltpu.VMEM((tm, d), BF16)],
        compiler_params=pltpu.CompilerParams(
            dimension_semantics=("parallel", "arbitrary"), vmem_limit_bytes=VMEM_LIMIT),
    )(x2d, y_rw, y_gdn, y_diff, w_mix_bf16, w_mix_bf16, w_mix_bf16, nw.reshape(1, d),
      w_in_bf16, w_in_bf16, w_out_bf16)


def _pack_in_proj(w_in_all, layer, w_vres_l):
    d = w_in_all.shape[1]
    vres = jnp.zeros((d, LANES), F32)
    if w_vres_l is not None:
        vres = vres.at[:, :LORA_VRES].set(w_vres_l)
    rb = 64
    return pl.pallas_call(
        _pack_w_kernel,
        grid=(d // rb,),
        in_specs=[pl.BlockSpec((None, rb, N_IN), lambda i: (layer, i, 0)),
                  pl.BlockSpec((rb, LANES), lambda i: (i, 0))],
        out_specs=(pl.BlockSpec((rb, COL_GDN), lambda i: (i, 0)),
                   pl.BlockSpec((rb, 4 * W_GDN), lambda i: (i, 0)),
                   pl.BlockSpec((rb, DIFF_COLS), lambda i: (i, 0))),
        out_shape=(jax.ShapeDtypeStruct((d, COL_GDN), BF16), jax.ShapeDtypeStruct((d, 4 * W_GDN), BF16),
                   jax.ShapeDtypeStruct((d, DIFF_COLS), BF16)),
        compiler_params=pltpu.CompilerParams(dimension_semantics=("parallel",), vmem_limit_bytes=VMEM_LIMIT),
    )(w_in_all, vres)


def _pack_w_kernel(w_ref, vres_ref, wa_ref, wg_ref, wd_ref):
    gdn0 = RWKV_COLS
    ab0 = gdn0 + 4 * W_GDN
    lane = lax.broadcasted_iota(jnp.int32, (1, LANES), 1)
    wa_ref[:, :RWKV_COLS] = w_ref[:, :RWKV_COLS].astype(BF16)
    wa_ref[:, COL_VRES:COL_GDN_AB] = vres_ref[...].astype(BF16)
    wa_ref[:, COL_GDN_AB:] = jnp.where(lane < 2 * N_HEADS_GDN, w_ref[:, ab0:ab0 + LANES], 0.0).astype(BF16)
    wg_ref[...] = w_ref[:, gdn0:ab0].astype(BF16)
    wd_ref[...] = w_ref[:, gdn0 + GDN_COLS:].astype(BF16)


def kernel(x, attn_norm_w, w_in, w_vres_a, mu_rwkv, mu_vres, rwkv_w0, rwkv_w_lora_b, rwkv_a0, rwkv_a_lora_b, rwkv_g_lora_b, rwkv_v0, rwkv_v_lora_b, rwkv_k_k, rwkv_k_a, rwkv_r_k, rwkv_ln_w, rwkv_ln_b, gdn_conv_w, gdn_A_log, gdn_dt_bias, gdn_norm_w, diff_q_norm_w, diff_k_norm_w, diff_lambda_q1, diff_lambda_k1, diff_lambda_q2, diff_lambda_k2, diff_subln_w, w_out, ffn_norm_w, w_ffn_in, w_ffn_out):
    bsz, t, d = x.shape
    depth = w_in.shape[0]
    x2d = x.reshape(bsz * t, d)
    v_first = None
    w_out_b, w_ffn_in_b, w_ffn_out_b = w_out.astype(BF16), w_ffn_in.astype(BF16), w_ffn_out.astype(BF16)
    for l in range(depth):
        w_proj = _pack_in_proj(w_in, l, None if l == 0 else w_vres_a[l - 1])
        mu_tail = jnp.zeros((RW_BLOCK - RWKV_COLS,), F32)
        if l > 0:
            mu_tail = mu_tail.at[:LORA_VRES].set(mu_vres[l - 1])
        mu_pad = jnp.concatenate([mu_rwkv[l], mu_tail]).reshape(1, RW_BLOCK)
        p3d = _norm_matmul(x2d, attn_norm_w[l], w_proj).reshape(bsz, t, N_PAD)
        rw_args = (rwkv_w0[l], rwkv_w_lora_b[l], rwkv_a0[l], rwkv_a_lora_b[l], rwkv_g_lora_b[l])
        rw_tail = (rwkv_k_k[l], rwkv_k_a[l], rwkv_r_k[l], rwkv_ln_w[l], rwkv_ln_b[l])
        if l == 0:
            y_rw, v_first = _rwkv_mix(p3d, None, mu_pad, *rw_args, None, None, *rw_tail)
        else:
            y_rw = _rwkv_mix(p3d, v_first, mu_pad, *rw_args, rwkv_v0[l - 1], rwkv_v_lora_b[l - 1], *rw_tail)
        y_gdn = _gdn_mix(p3d, gdn_conv_w[l], gdn_A_log[l], gdn_dt_bias[l], gdn_norm_w[l])
        lambda_init = 0.8 - 0.6 * math.exp(-0.3 * l)
        lam_vecs = jnp.stack([diff_lambda_q1[l], diff_lambda_k1[l], diff_lambda_q2[l], diff_lambda_k2[l]]).astype(F32)
        y_diff = _diff_attn(p3d, diff_q_norm_w[l], diff_k_norm_w[l], lam_vecs, diff_subln_w[l], lambda_init)
        n = bsz * t
        x2d = _mix_ffn(x2d, y_rw.reshape(n, W_RWKV), y_gdn.reshape(n, W_GDN), y_diff.reshape(n, W_DIFF),
                       w_out_b, ffn_norm_w[l], w_ffn_in_b, w_ffn_out_b, l)
    return x2d.reshape(bsz, t, d)
```
